```python
import jax, jax.numpy as jnp
from jax import lax
import numpy as np

D_MODEL = 2048
BATCH = 4
SEQ = 8192
DEPTH = 4
DEC_BATCH = 32
DEC_SEQ = 32
PAST_LEN = 1024

CHUNK = 64
N_EVEN = (DEPTH + 1) // 2
N_ODD = DEPTH // 2
A_WIDTH = D_MODEL // 2
A_HEADS = 8
A_DK = A_WIDTH // A_HEADS
A_DV = A_DK
B_WIDTH = D_MODEL - A_WIDTH
POOL_WINDOWS = (2, 4, 8, 16)
POOL_GROUP = B_WIDTH // len(POOL_WINDOWS)
POOL_PREV = max(POOL_WINDOWS) - 1
IN_WIDTH = 4 * A_WIDTH + B_WIDTH
C_HEADS = 16
C_HDIM = D_MODEL // C_HEADS
LEFT_CHUNKS = 8
WINDOW = LEFT_CHUNKS * CHUNK
BAND = WINDOW + CHUNK
MAX_REL = 128
REL_SIZE = CHUNK + MAX_REL
N_EXPERTS = 64
TOP_K = 8
N_GROUPS = 8
TOPK_GROUPS = 4
D_EXPERT = 512
ROUTE_SCALE = 2.5
DISPATCH_BLOCK = 128
ALPHA = (2 * DEPTH) ** 0.25
BETA = (8 * DEPTH) ** -0.25
LN_EPS = 1e-5
RMS_EPS = 1e-6

kernel_name = 'hybrid_streaming_encoder_step'

F32 = jnp.float32


def layer_norm(x, g, b):
    xf = x.astype(F32)
    mu = xf.mean(-1, keepdims=True)
    var = jnp.square(xf - mu).mean(-1, keepdims=True)
    return ((xf - mu) * lax.rsqrt(var + LN_EPS) * g + b).astype(x.dtype)


def hgrn_lower_bounds(lb_param):
    p = jax.nn.softmax(lb_param.astype(F32), axis=0)
    return jnp.cumsum(p, axis=0) - p[0]


def hgrn2_recurrence(q, k, v, logf, s0):
    b, h, t, dk = q.shape
    c = CHUNK if t % CHUNK == 0 else t
    n = t // c

    def to_chunks(a):
        return jnp.moveaxis(a.reshape(b, h, n, c, a.shape[-1]), 2, 0)

    causal = jnp.tril(jnp.ones((c, c), bool))

    def step(s, inp):
        qc, kc, vc, gc = inp
        cum = jnp.cumsum(gc, axis=2)
        o_inter = jnp.einsum('bhtd,bhde->bhte', qc * jnp.exp(cum), s)
        diff = cum[:, :, :, None, :] - cum[:, :, None, :, :]
        decay = jnp.exp(jnp.where(causal[:, :, None], diff, -jnp.inf))
        scores = jnp.einsum('bhtd,bhsd,bhtsd->bhts', qc, kc, decay)
        o_intra = jnp.einsum('bhts,bhse->bhte', scores, vc)
        last = cum[:, :, -1:, :]
        s_new = (jnp.exp(last[:, :, 0, :])[..., None] * s
                 + jnp.einsum('bhsd,bhse->bhde', kc * jnp.exp(last - cum), vc))
        return s_new, o_inter + o_intra

    s_fin, o = lax.scan(step, s0, (to_chunks(q), to_chunks(k), to_chunks(v), to_chunks(logf)))
    o = jnp.moveaxis(o, 0, 2).reshape(b, h, t, v.shape[-1])
    return o, s_fin


def pool_mixer(u, prev, pos0, pool_w, pool_scale):
    b, t, _ = u.shape
    ext = jnp.concatenate([prev.astype(u.dtype), u], axis=1)
    cs = jnp.cumsum(ext.astype(F32), axis=1)
    cs = jnp.concatenate([jnp.zeros((b, 1, B_WIDTH), F32), cs], axis=1)
    pos = pos0 + jnp.arange(t)
    hi = cs[:, POOL_PREV + 1:]
    pooled = []
    for gi, w in enumerate(POOL_WINDOWS):
        sl = slice(gi * POOL_GROUP, (gi + 1) * POOL_GROUP)
        lo = cs[:, POOL_PREV + 1 - w:POOL_PREV + 1 - w + t, sl]
        cnt = jnp.minimum(pos + 1, w).astype(F32)[None, :, None]
        pooled.append((hi[:, :, sl] - lo) / cnt)
    d = jnp.concatenate(pooled, axis=-1) - u.astype(F32)
    y = jnp.einsum('btgc,gcd->btgd', d.reshape(b, t, len(POOL_WINDOWS), POOL_GROUP), pool_w.astype(F32))
    y = y.reshape(b, t, B_WIDTH) * pool_scale
    return y.astype(u.dtype), ext[:, -POOL_PREV:]


def hgrn_pool_mixer(x, pos0, s0, pool_prev, w_in, lb, gnorm, pool_w, pool_scale, w_out):
    b, t, _ = x.shape
    proj = x @ w_in
    q, fl, iv, g, u = jnp.split(proj, [A_WIDTH, 2 * A_WIDTH, 3 * A_WIDTH, 4 * A_WIDTH], axis=-1)

    def heads(a):
        return jnp.swapaxes(a.astype(F32).reshape(b, t, A_HEADS, -1), 1, 2)

    lbh = lb.reshape(A_HEADS, 1, A_DK)
    f = lbh + (1.0 - lbh) * jax.nn.sigmoid(heads(fl))
    o, s_new = hgrn2_recurrence(jax.nn.silu(heads(q)), 1.0 - f, heads(iv), jnp.log(f), s0.astype(F32))
    o = jnp.swapaxes(o, 1, 2)
    o = o * lax.rsqrt(jnp.mean(o * o, axis=-1, keepdims=True) + RMS_EPS) * gnorm
    o_a = (o * jax.nn.silu(g.astype(F32).reshape(b, t, A_HEADS, A_DV))).reshape(b, t, A_WIDTH)
    o_b, pool_new = pool_mixer(u, pool_prev, pos0, pool_w, pool_scale)
    out = jnp.concatenate([o_a.astype(x.dtype), o_b], axis=-1) @ w_out
    return out, s_new, pool_new


def rel_bias_lookup(table, dist):
    idx = jnp.clip(dist, -(CHUNK - 1), MAX_REL) + (CHUNK - 1)
    return table[:, idx].astype(F32)


def band_attention_prompt(x, w_qkv, rel_bias, w_o):
    b, t, _ = x.shape
    nc = t // CHUNK
    qkv = (x @ w_qkv).reshape(b, t, 3, C_HEADS, C_HDIM)
    q, k, v = qkv[:, :, 0], qkv[:, :, 1], qkv[:, :, 2]
    pad = jnp.zeros((b, WINDOW, C_HEADS, C_HDIM), x.dtype)
    kp = jnp.concatenate([pad, k], axis=1)
    vp = jnp.concatenate([pad, v], axis=1)
    qc = q.reshape(b, nc, CHUNK, C_HEADS, C_HDIM)
    dist = jnp.arange(CHUNK)[:, None] + WINDOW - jnp.arange(BAND)[None, :]
    bias = rel_bias_lookup(rel_bias, dist)
    scale = C_HDIM ** -0.5

    def one_chunk(ci):
        qb = lax.dynamic_index_in_dim(qc, ci, axis=1, keepdims=False)
        kb = lax.dynamic_slice_in_dim(kp, ci * CHUNK, BAND, axis=1)
        vb = lax.dynamic_slice_in_dim(vp, ci * CHUNK, BAND, axis=1)
        s = jnp.einsum('bqhd,bkhd->bhqk', qb, kb).astype(F32) * scale + bias
        valid = ci * CHUNK - WINDOW + jnp.arange(BAND) >= 0
        p = jax.nn.softmax(jnp.where(valid, s, -jnp.inf), axis=-1)
        return jnp.einsum('bhqk,bkhd->bqhd', p.astype(vb.dtype), vb)

    o = lax.map(one_chunk, jnp.arange(nc))
    o = jnp.moveaxis(o, 0, 1).reshape(b, t, D_MODEL)
    keep = min(WINDOW, t)
    return o @ w_o, k[:, t - keep:], v[:, t - keep:]


def band_attention_sample(x, k_cache, v_cache, w_qkv, rel_bias, w_o):
    b, t, _ = x.shape
    lc = k_cache.shape[1]
    qkv = (x @ w_qkv).reshape(b, t, 3, C_HEADS, C_HDIM)
    q, k, v = qkv[:, :, 0], qkv[:, :, 1], qkv[:, :, 2]
    kk = jnp.concatenate([k_cache.astype(x.dtype), k], axis=1)
    vv = jnp.concatenate([v_cache.astype(x.dtype), v], axis=1)
    dist = jnp.arange(t)[:, None] + lc - jnp.arange(lc + t)[None, :]
    bias = rel_bias_lookup(rel_bias, dist)
    s = jnp.einsum('bqhd,bkhd->bhqk', q, kk).astype(F32) * (C_HDIM ** -0.5) + bias
    p = jax.nn.softmax(s, axis=-1)
    o = jnp.einsum('bhqk,bkhd->bqhd', p.astype(vv.dtype), vv).reshape(b, t, D_MODEL)
    return o @ w_o, k, v


def moe_ffn(h, w_router, router_bias, w_gu, w_dn, ws_gu, ws_dn):
    n, d = h.shape
    scores = jax.nn.sigmoid((h @ w_router).astype(F32))
    sel = (scores + router_bias).reshape(n, N_GROUPS, N_EXPERTS // N_GROUPS)
    grp_score = lax.top_k(sel, 2)[0].sum(-1)
    _, top_g = lax.top_k(grp_score, TOPK_GROUPS)
    keep = jnp.any(top_g[:, :, None] == jnp.arange(N_GROUPS)[None, None, :], axis=1)
    sel = jnp.where(keep[:, :, None], sel, -jnp.inf).reshape(n, N_EXPERTS)
    _, top_e = lax.top_k(sel, TOP_K)
    gate = jnp.take_along_axis(scores, top_e, axis=1)
    gate = gate / gate.sum(-1, keepdims=True) * ROUTE_SCALE

    nk = n * TOP_K
    e_flat = top_e.reshape(-1).astype(jnp.int32)
    tok_flat = jnp.repeat(jnp.arange(n, dtype=jnp.int32), TOP_K)
    g_flat = gate.reshape(-1)
    e_s, tok_s, g_s = lax.sort((e_flat, tok_flat, g_flat), num_keys=1, is_stable=True)
    counts = jnp.bincount(e_flat, length=N_EXPERTS)
    padded = (counts + DISPATCH_BLOCK - 1) // DISPATCH_BLOCK * DISPATCH_BLOCK
    start = jnp.cumsum(counts) - counts
    pend = jnp.cumsum(padded)
    pstart = pend - padded
    dest = pstart[e_s] + jnp.arange(nk) - start[e_s]
    n_blocks = (nk + N_EXPERTS * (DISPATCH_BLOCK - 1) + DISPATCH_BLOCK - 1) // DISPATCH_BLOCK
    rows = n_blocks * DISPATCH_BLOCK
    buf_tok = jnp.full((rows,), n, jnp.int32).at[dest].set(tok_s)
    buf_g = jnp.zeros((rows,), F32).at[dest].set(g_s)
    blk_e = jnp.clip(jnp.searchsorted(pend, jnp.arange(n_blocks) * DISPATCH_BLOCK, side='right'),
                     0, N_EXPERTS - 1)
    h_pad = jnp.concatenate([h, jnp.zeros((1, d), h.dtype)], axis=0)

    def step(acc, blk):
        e, toks, gs = blk
        xb = h_pad[toks]
        gu = xb @ w_gu[e]
        a = jax.nn.silu(gu[:, :D_EXPERT]) * gu[:, D_EXPERT:]
        y = (a @ w_dn[e]).astype(F32) * gs[:, None]
        return acc.at[toks].add(y), None

    acc, _ = lax.scan(step, jnp.zeros((n + 1, d), F32),
                      (blk_e, buf_tok.reshape(n_blocks, DISPATCH_BLOCK), buf_g.reshape(n_blocks, DISPATCH_BLOCK)))
    sgu = h @ ws_gu
    shared = (jax.nn.silu(sgu[:, :D_EXPERT]) * sgu[:, D_EXPERT:]) @ ws_dn
    return (acc[:n] + shared.astype(F32)).astype(h.dtype)


def run_trunk(x, pos0, hgrn_in, pool_in, k_in, v_in, ln_g, ln_b, w_in_even, lb_even, gnorm_even,
              pool_w_even, pool_scale_even, w_out_even, w_qkv_odd, rel_bias_odd, w_o_odd,
              w_router, router_bias, w_gu, w_dn, ws_gu, ws_dn):
    b, t, _ = x.shape
    lbs = hgrn_lower_bounds(lb_even)
    hgrn_out, pool_out, k_out, v_out = [], [], [], []
    for l in range(DEPTH):
        j = l // 2
        if l % 2 == 0:
            s0 = jnp.zeros((b, A_HEADS, A_DK, A_DV), F32) if hgrn_in is None else hgrn_in[j]
            prev = jnp.zeros((b, POOL_PREV, B_WIDTH), x.dtype) if pool_in is None else pool_in[j]
            h, s_new, p_new = hgrn_pool_mixer(x, pos0, s0, prev, w_in_even[j], lbs[j], gnorm_even[j],
                                              pool_w_even[j], pool_scale_even[j], w_out_even[j])
            hgrn_out.append(s_new.astype(x.dtype))
            pool_out.append(p_new)
        else:
            if k_in is None:
                h, k_new, v_new = band_attention_prompt(x, w_qkv_odd[j], rel_bias_odd[j], w_o_odd[j])
            else:
                h, k_new, v_new = band_attention_sample(x, k_in[j], v_in[j], w_qkv_odd[j],
                                                        rel_bias_odd[j], w_o_odd[j])
            k_out.append(k_new)
            v_out.append(v_new)
        x = layer_norm(ALPHA * x + h, ln_g[l, 0], ln_b[l, 0])
        ffn = moe_ffn(x.reshape(b * t, D_MODEL), w_router[l], router_bias[l], w_gu[l], w_dn[l],
                      ws_gu[l], ws_dn[l]).reshape(b, t, D_MODEL)
        x = layer_norm(ALPHA * x + ffn, ln_g[l, 1], ln_b[l, 1])
    return x, jnp.stack(hgrn_out), jnp.stack(pool_out), jnp.stack(k_out), jnp.stack(v_out)


def setup_inputs(seed: int = 0) -> dict:
    key = jax.random.key(seed)
    ks = jax.random.split(key, 24)

    def nrm(k, shape, scale=1.0):
        return jax.random.normal(k, shape, F32) * scale

    cache_len = min(WINDOW, PAST_LEN)
    return {
        'x_prompt': nrm(ks[0], (BATCH, SEQ, D_MODEL)),
        'x_sample': nrm(ks[1], (DEC_BATCH, DEC_SEQ, D_MODEL)),
        'state_hgrn': nrm(ks[2], (N_EVEN, DEC_BATCH, A_HEADS, A_DK, A_DV), 0.5),
        'state_pool': nrm(ks[3], (N_EVEN, DEC_BATCH, POOL_PREV, B_WIDTH)),
        'cache_k': nrm(ks[4], (N_ODD, DEC_BATCH, cache_len, C_HEADS, C_HDIM)),
        'cache_v': nrm(ks[5], (N_ODD, DEC_BATCH, cache_len, C_HEADS, C_HDIM)),
        'ln_g': 1.0 + nrm(ks[6], (DEPTH, 2, D_MODEL), 0.1),
        'ln_b': nrm(ks[7], (DEPTH, 2, D_MODEL), 0.02),
        'w_in_even': nrm(ks[8], (N_EVEN, D_MODEL, IN_WIDTH), D_MODEL ** -0.5),
        'lb_even': nrm(ks[9], (N_EVEN, A_WIDTH), 0.5),
        'gnorm_even': 1.0 + nrm(ks[10], (N_EVEN, A_DV), 0.1),
        'pool_w_even': nrm(ks[11], (N_EVEN, len(POOL_WINDOWS), POOL_GROUP, POOL_GROUP), POOL_GROUP ** -0.5),
        'pool_scale_even': 1.0 + nrm(ks[12], (N_EVEN, B_WIDTH), 0.1),
        'w_out_even': nrm(ks[13], (N_EVEN, D_MODEL, D_MODEL), BETA * D_MODEL ** -0.5),
        'w_qkv_odd': nrm(ks[14], (N_ODD, D_MODEL, 3 * D_MODEL), D_MODEL ** -0.5),
        'rel_bias_odd': nrm(ks[15], (N_ODD, C_HEADS, REL_SIZE), 0.5),
        'w_o_odd': nrm(ks[16], (N_ODD, D_MODEL, D_MODEL), BETA * D_MODEL ** -0.5),
        'w_router': nrm(ks[17], (DEPTH, D_MODEL, N_EXPERTS), D_MODEL ** -0.5),
        'router_bias': nrm(ks[18], (DEPTH, N_EXPERTS), 0.01),
        'w_gu': nrm(ks[19], (DEPTH, N_EXPERTS, D_MODEL, 2 * D_EXPERT), D_MODEL ** -0.5),
        'w_dn': nrm(ks[20], (DEPTH, N_EXPERTS, D_EXPERT, D_MODEL), BETA * D_EXPERT ** -0.5),
        'ws_gu': nrm(ks[21], (DEPTH, D_MODEL, 2 * D_EXPERT), D_MODEL ** -0.5),
        'ws_dn': nrm(ks[22], (DEPTH, D_EXPERT, D_MODEL), BETA * D_EXPERT ** -0.5),
    }


def reference(x_prompt, x_sample, state_hgrn, state_pool, cache_k, cache_v, ln_g, ln_b, w_in_even,
              lb_even, gnorm_even, pool_w_even, pool_scale_even, w_out_even, w_qkv_odd, rel_bias_odd,
              w_o_odd, w_router, router_bias, w_gu, w_dn, ws_gu, ws_dn):
    y_prompt, hgrn_p, pool_p, k_p, v_p = run_trunk(
        x_prompt, 0, None, None, None, None, ln_g, ln_b, w_in_even, lb_even, gnorm_even, pool_w_even,
        pool_scale_even, w_out_even, w_qkv_odd, rel_bias_odd, w_o_odd, w_router, router_bias, w_gu, w_dn,
        ws_gu, ws_dn)
    y_sample, hgrn_s, pool_s, k_s, v_s = run_trunk(
        x_sample, PAST_LEN, state_hgrn, state_pool, cache_k, cache_v, ln_g, ln_b, w_in_even, lb_even,
        gnorm_even, pool_w_even, pool_scale_even, w_out_even, w_qkv_odd, rel_bias_odd, w_o_odd, w_router,
        router_bias, w_gu, w_dn, ws_gu, ws_dn)
    return (y_prompt, y_sample, hgrn_p, pool_p, k_p, v_p, hgrn_s, pool_s, k_s, v_s)
```

```python
import functools

import numpy as np
import jax
import jax.numpy as jnp
from jax import lax
from jax.experimental import pallas as pl
from jax.experimental.pallas import tpu as pltpu

F32 = jnp.float32
BF16 = jnp.bfloat16
I32 = jnp.int32

CHUNK = 64
HEAD_DIM = 128
POOL_WINDOWS = (2, 4, 8, 16)
POOL_PREV = max(POOL_WINDOWS) - 1
POOL_CARRY = POOL_PREV + 1
LEFT_CHUNKS = 8
WINDOW = LEFT_CHUNKS * CHUNK
MAX_REL = 128
N_GROUPS = 8
TOPK_GROUPS = 4
TOP_K = 8
ROUTE_SCALE = 2.5
LN_EPS = 1e-5
RMS_EPS = 1e-6
PAST_LEN = 1024
NEG = -1e30

VMEM_LIMIT_BYTES = 56 * 1024 * 1024
MM_BLOCK_M = 1024
MM_BLOCK_N = 1024
LN_BLOCK_M = 512
ATTN_BLOCK_Q = 256
POOL_BLOCK_T = 512
ROUTER_BLOCK_T = 512
DISPATCH_BLOCK_T = 512
EXPERT_BLOCK = 256
COMBINE_BLOCK_T = 128


def _params(**kw):
    return pltpu.CompilerParams(vmem_limit_bytes=VMEM_LIMIT_BYTES, **kw)


def _sigmoid(x):
    return 1.0 / (1.0 + jnp.exp(-x))


def _silu(x):
    return x * _sigmoid(x)


def _dot(a, b):
    return jnp.dot(a, b, preferred_element_type=F32)


def _dot_nt(a, b):
    return lax.dot_general(a, b, (((1,), (1,)), ((), ())), preferred_element_type=F32)


def _dot_tn(a, b):
    return lax.dot_general(a, b, (((0,), (0,)), ((), ())), preferred_element_type=F32)


def _layer_norm(y, g, b):
    mu = jnp.mean(y, axis=-1, keepdims=True)
    yc = y - mu
    var = jnp.mean(yc * yc, axis=-1, keepdims=True)
    return yc * lax.rsqrt(var + LN_EPS) * g + b


def _mm_kernel(x_ref, w_ref, o_ref):
    o_ref[...] = _dot(x_ref[...], w_ref[...]).astype(o_ref.dtype)


def matmul(x, w, out_dtype=F32, bm=MM_BLOCK_M, bn=MM_BLOCK_N):
    m, k = x.shape
    n = w.shape[1]
    bm, bn = min(bm, m), min(bn, n)
    assert m % bm == 0 and n % bn == 0
    return pl.pallas_call(
        _mm_kernel,
        grid=(m // bm, n // bn),
        in_specs=[pl.BlockSpec((bm, k), lambda i, j: (i, 0)),
                  pl.BlockSpec((k, bn), lambda i, j: (0, j))],
        out_specs=pl.BlockSpec((bm, bn), lambda i, j: (i, j)),
        out_shape=jax.ShapeDtypeStruct((m, n), out_dtype),
        compiler_params=_params(),
        name="matmul",
    )(x, w)


def _mm_ln_kernel(alpha, a_ref, w_ref, res_ref, g_ref, b_ref, x_ref, xb_ref):
    y = alpha * res_ref[...] + _dot(a_ref[...], w_ref[...])
    out = _layer_norm(y, g_ref[...], b_ref[...])
    x_ref[...] = out
    xb_ref[...] = out.astype(BF16)


def matmul_residual_ln(a, w, res, g, b, alpha, bm=LN_BLOCK_M):
    m, k = a.shape
    d = w.shape[1]
    bm = min(bm, m)
    assert m % bm == 0
    return pl.pallas_call(
        functools.partial(_mm_ln_kernel, alpha),
        grid=(m // bm,),
        in_specs=[pl.BlockSpec((bm, k), lambda i: (i, 0)),
                  pl.BlockSpec((k, d), lambda i: (0, 0)),
                  pl.BlockSpec((bm, d), lambda i: (i, 0)),
                  pl.BlockSpec((1, d), lambda i: (0, 0)),
                  pl.BlockSpec((1, d), lambda i: (0, 0))],
        out_specs=[pl.BlockSpec((bm, d), lambda i: (i, 0)),
                   pl.BlockSpec((bm, d), lambda i: (i, 0))],
        out_shape=[jax.ShapeDtypeStruct((m, d), F32), jax.ShapeDtypeStruct((m, d), BF16)],
        compiler_params=_params(),
        name="matmul_residual_ln",
    )(a, w, res, g.reshape(1, d), b.reshape(1, d))


def _hgrn_tables(c):
    t = np.arange(c)
    levels = []
    b = c // 2
    while b >= 1:
        levels.append(b)
        b //= 2
    sel = [(t[None, :] <= t[:, None]).astype(np.float32)]
    masks = [np.eye(c, dtype=np.float32)]
    for b in levels:
        blk = t // b
        odd = (blk % 2) == 1
        ref_row = np.where(odd, blk * b - 1, blk * b + b - 1)
        sel.append((t[None, :] <= ref_row[:, None]).astype(np.float32))
        same = (t[:, None] // (2 * b)) == (t[None, :] // (2 * b))
        masks.append((same & odd[:, None] & (~odd)[None, :]).astype(np.float32))
    return np.concatenate(sel, axis=0), np.stack(masks, axis=0)


def _hgrn_kernel(n_heads, c, q_ref, f_ref, i_ref, g_ref, s0_ref, lb_ref, gn_ref, sel_ref, mask_ref,
                 o_ref, sfin_ref, state_ref):
    j = pl.program_id(1)

    @pl.when(j == 0)
    def _():
        state_ref[...] = s0_ref[0]

    n_lvl = mask_ref.shape[0]
    sel = sel_ref[...]
    for h in range(n_heads):
        hs = slice(h * HEAD_DIM, (h + 1) * HEAD_DIM)
        lb = lb_ref[:, hs]
        f = lb + (1.0 - lb) * _sigmoid(f_ref[:, hs])
        k = 1.0 - f
        q = _silu(q_ref[:, hs])
        v = i_ref[:, hs].astype(BF16)
        cums = jnp.dot(sel, jnp.log(f), precision=lax.Precision.HIGHEST, preferred_element_type=F32)
        cum = cums[0:c]
        st = state_ref[h]
        o = _dot_nt((q * jnp.exp(cum)).astype(BF16), st.astype(BF16))
        scores = _dot_nt(q.astype(BF16), k.astype(BF16)) * mask_ref[0]
        for l in range(1, n_lvl):
            w = jnp.exp(-jnp.abs(cum - cums[l * c:(l + 1) * c]))
            scores = scores + _dot_nt((q * w).astype(BF16), (k * w).astype(BF16)) * mask_ref[l]
        o = o + _dot(scores.astype(BF16), v)
        last = cum[c - 1:c, :]
        k_end = (k * jnp.exp(last - cum)).astype(BF16)
        state_ref[h] = st * jnp.exp(last) + _dot_tn(v, k_end)
        o = o * lax.rsqrt(jnp.mean(o * o, axis=-1, keepdims=True) + RMS_EPS) * gn_ref[...]
        o_ref[:, hs] = (o * _silu(g_ref[:, hs])).astype(o_ref.dtype)

    @pl.when(j == pl.num_programs(1) - 1)
    def _():
        sfin_ref[0] = state_ref[...]


def hgrn_mixer(proj, row0, batch, seq, s0_t, lb, gnorm):
    width = proj.shape[1] // 5
    n_heads = width // HEAD_DIM
    c = CHUNK if seq % CHUNK == 0 else seq
    n = seq // c
    assert row0 % c == 0
    blk0 = row0 // c
    sel, masks = _hgrn_tables(c)

    def sec(s):
        return pl.BlockSpec((c, width), lambda b, j: (blk0 + b * n + j, s))

    state_spec = pl.BlockSpec((1, n_heads, HEAD_DIM, HEAD_DIM), lambda b, j: (b, 0, 0, 0))
    return pl.pallas_call(
        functools.partial(_hgrn_kernel, n_heads, c),
        grid=(batch, n),
        in_specs=[sec(0), sec(1), sec(2), sec(3), state_spec,
                  pl.BlockSpec((1, width), lambda b, j: (0, 0)),
                  pl.BlockSpec((1, HEAD_DIM), lambda b, j: (0, 0)),
                  pl.BlockSpec(sel.shape, lambda b, j: (0, 0)),
                  pl.BlockSpec(masks.shape, lambda b, j: (0, 0, 0))],
        out_specs=[pl.BlockSpec((c, width), lambda b, j: (b * n + j, 0)), state_spec],
        out_shape=[jax.ShapeDtypeStruct((batch * seq, width), BF16),
                   jax.ShapeDtypeStruct((batch, n_heads, HEAD_DIM, HEAD_DIM), F32)],
        scratch_shapes=[pltpu.VMEM((n_heads, HEAD_DIM, HEAD_DIM), F32)],
        compiler_params=_params(),
        name="hgrn_mixer",
    )(proj, proj, proj, proj, s0_t, lb.reshape(1, width), gnorm.reshape(1, HEAD_DIM),
      jnp.asarray(sel), jnp.asarray(masks))


def _pool_kernel(pos0, bt, group, u_ref, prev_ref, w_ref, scale_ref, o_ref, ext_ref):
    j = pl.program_id(1)

    @pl.when(j == 0)
    def _():
        ext_ref[0:POOL_CARRY] = prev_ref[0]

    u = u_ref[...]
    ext_ref[POOL_CARRY:POOL_CARRY + bt] = u
    pos = pos0 + j * bt + lax.broadcasted_iota(I32, (bt, 1), 0)
    for gi, win in enumerate(POOL_WINDOWS):
        cs = slice(gi * group, (gi + 1) * group)
        s = ext_ref[:, cs]
        step = 1
        while step < win:
            s = s + pltpu.roll(s, step, axis=0)
            step *= 2
        cnt = jnp.minimum(pos + 1, win).astype(F32)
        d = s[POOL_CARRY:] / cnt - u[:, cs]
        y = _dot(d.astype(BF16), w_ref[gi]) * scale_ref[:, cs]
        o_ref[:, cs] = y.astype(o_ref.dtype)
    ext_ref[0:POOL_CARRY] = ext_ref[bt:bt + POOL_CARRY]


def pool_mixer(proj, row0, batch, seq, pos0, prev, pool_w, pool_scale):
    width = proj.shape[1] // 5
    group = width // len(POOL_WINDOWS)
    bt = min(POOL_BLOCK_T, seq)
    n = seq // bt
    assert seq % bt == 0 and row0 % bt == 0 and sum(POOL_WINDOWS) // 2 <= POOL_CARRY
    blk0 = row0 // bt
    return pl.pallas_call(
        functools.partial(_pool_kernel, pos0, bt, group),
        grid=(batch, n),
        in_specs=[pl.BlockSpec((bt, width), lambda b, j: (blk0 + b * n + j, 4)),
                  pl.BlockSpec((1, POOL_CARRY, width), lambda b, j: (b, 0, 0)),
                  pl.BlockSpec(pool_w.shape, lambda b, j: (0, 0, 0)),
                  pl.BlockSpec((1, width), lambda b, j: (0, 0))],
        out_specs=pl.BlockSpec((bt, width), lambda b, j: (b * n + j, 0)),
        out_shape=jax.ShapeDtypeStruct((batch * seq, width), BF16),
        scratch_shapes=[pltpu.VMEM((bt + POOL_CARRY, width), F32)],
        compiler_params=_params(),
        name="pool_mixer",
    )(proj, prev, pool_w, pool_scale.reshape(1, width))


def _rel_bias(table, dist):
    idx = jnp.clip(dist, -(CHUNK - 1), MAX_REL) + (CHUNK - 1)
    return table[:, idx].astype(F32)


def _attn_prompt_kernel(bq, scale, q_ref, k0_ref, k1_ref, k2_ref, v0_ref, v1_ref, v2_ref, bias_ref,
                        o_ref):
    i = pl.program_id(2)
    q = q_ref[...].astype(BF16)
    ks = (k0_ref, k1_ref, k2_ref)
    vs = (v0_ref, v1_ref, v2_ref)
    s = []
    for p in range(3):
        sp = _dot_nt(q, ks[p][...].astype(BF16)) * scale + bias_ref[0, :, p * bq:(p + 1) * bq]
        if p < 2:
            sp = jnp.where(i >= 2 - p, sp, NEG)
        s.append(sp)
    m = jnp.maximum(jnp.maximum(jnp.max(s[0], axis=-1, keepdims=True),
                                jnp.max(s[1], axis=-1, keepdims=True)),
                    jnp.max(s[2], axis=-1, keepdims=True))
    e = [jnp.exp(sp - m) for sp in s]
    inv = 1.0 / (jnp.sum(e[0], axis=-1, keepdims=True) + jnp.sum(e[1], axis=-1, keepdims=True)
                 + jnp.sum(e[2], axis=-1, keepdims=True))
    o = _dot((e[0] * inv).astype(BF16), vs[0][...].astype(BF16))
    o = o + _dot((e[1] * inv).astype(BF16), vs[1][...].astype(BF16))
    o = o + _dot((e[2] * inv).astype(BF16), vs[2][...].astype(BF16))
    o_ref[...] = o.astype(o_ref.dtype)


def band_attention_prompt(qkv, batch, seq, rel_bias):
    d = qkv.shape[1] // 3
    n_heads = d // HEAD_DIM
    bq = ATTN_BLOCK_Q
    assert seq % bq == 0 and WINDOW == 2 * bq and bq % CHUNK == 0
    nq = seq // bq
    qi = jnp.arange(bq)[:, None]
    kj = jnp.arange(3 * bq)[None, :]
    lo = (qi // CHUNK) * CHUNK
    in_band = (kj >= lo) & (kj < lo + WINDOW + CHUNK)
    bias = jnp.where(in_band[None], _rel_bias(rel_bias, qi + WINDOW - kj), NEG)

    def kv(sec, p):
        return pl.BlockSpec((bq, HEAD_DIM),
                            lambda h, b, i: (b * nq + jnp.maximum(i - 2 + p, 0), sec * n_heads + h))

    return pl.pallas_call(
        functools.partial(_attn_prompt_kernel, bq, HEAD_DIM ** -0.5),
        grid=(n_heads, batch, nq),
        in_specs=[pl.BlockSpec((bq, HEAD_DIM), lambda h, b, i: (b * nq + i, h)),
                  kv(1, 0), kv(1, 1), kv(1, 2), kv(2, 0), kv(2, 1), kv(2, 2),
                  pl.BlockSpec((1, bq, 3 * bq), lambda h, b, i: (h, 0, 0))],
        out_specs=pl.BlockSpec((bq, HEAD_DIM), lambda h, b, i: (b * nq + i, h)),
        out_shape=jax.ShapeDtypeStruct((batch * seq, d), BF16),
        compiler_params=_params(),
        name="band_attention_prompt",
    )(qkv, qkv, qkv, qkv, qkv, qkv, qkv, bias)


def _attn_sample_kernel(n_heads, lc, scale, qkv_ref, ck_ref, cv_ref, bias_ref, o_ref):
    d = n_heads * HEAD_DIM
    for h in range(n_heads):
        hs = slice(h * HEAD_DIM, (h + 1) * HEAD_DIM)
        q = qkv_ref[:, h * HEAD_DIM:(h + 1) * HEAD_DIM].astype(BF16)
        kn = qkv_ref[:, d + h * HEAD_DIM:d + (h + 1) * HEAD_DIM].astype(BF16)
        vn = qkv_ref[:, 2 * d + h * HEAD_DIM:2 * d + (h + 1) * HEAD_DIM].astype(BF16)
        kc = ck_ref[0, :, hs].astype(BF16)
        vc = cv_ref[0, :, hs].astype(BF16)
        sc = _dot_nt(q, kc) * scale + bias_ref[h, :, 0:lc]
        sn = _dot_nt(q, kn) * scale + bias_ref[h, :, lc:]
        m = jnp.maximum(jnp.max(sc, axis=-1, keepdims=True), jnp.max(sn, axis=-1, keepdims=True))
        ec = jnp.exp(sc - m)
        en = jnp.exp(sn - m)
        inv = 1.0 / (jnp.sum(ec, axis=-1, keepdims=True) + jnp.sum(en, axis=-1, keepdims=True))
        o = _dot((ec * inv).astype(BF16), vc) + _dot((en * inv).astype(BF16), vn)
        o_ref[:, hs] = o.astype(o_ref.dtype)


def band_attention_sample(qkv, row0, batch, seq, cache_k, cache_v, rel_bias):
    d = qkv.shape[1] // 3
    n_heads = d // HEAD_DIM
    lc = cache_k.shape[1]
    assert row0 % seq == 0 and lc % 128 == 0
    blk0 = row0 // seq
    dist = jnp.arange(seq)[:, None] + lc - jnp.arange(lc + seq)[None, :]
    bias = _rel_bias(rel_bias, dist)
    return pl.pallas_call(
        functools.partial(_attn_sample_kernel, n_heads, lc, HEAD_DIM ** -0.5),
        grid=(batch,),
        in_specs=[pl.BlockSpec((seq, 3 * d), lambda b: (blk0 + b, 0)),
                  pl.BlockSpec((1, lc, d), lambda b: (b, 0, 0)),
                  pl.BlockSpec((1, lc, d), lambda b: (b, 0, 0)),
                  pl.BlockSpec(bias.shape, lambda b: (0, 0, 0))],
        out_specs=pl.BlockSpec((seq, d), lambda b: (b, 0)),
        out_shape=jax.ShapeDtypeStruct((batch * seq, d), BF16),
        compiler_params=_params(),
        name="band_attention_sample",
    )(qkv, cache_k, cache_v, bias)


def _first_max(vals, idx, sentinel):
    m = jnp.max(vals, axis=0, keepdims=True)
    first = jnp.min(jnp.where(vals == m, idx, sentinel), axis=0, keepdims=True)
    return m, first


def _stack_rows(rows, lanes):
    ridx = lax.broadcasted_iota(I32, (len(rows), lanes), 0)
    out = jnp.zeros((len(rows), lanes), rows[0].dtype)
    for r, row in enumerate(rows):
        out = jnp.where(ridx == r, row, out)
    return out


def _router_kernel(n_experts, bt, x_ref, wt_ref, bias_ref, e_ref, gate_ref, rank_ref, cnt_ref,
                   carry_ref):
    i = pl.program_id(0)

    @pl.when(i == 0)
    def _():
        carry_ref[...] = jnp.zeros_like(carry_ref)

    per = n_experts // N_GROUPS
    logits = lax.dot_general(wt_ref[...], x_ref[...], (((1,), (1,)), ((), ())),
                             precision=lax.Precision.HIGHEST, preferred_element_type=F32)
    scores = _sigmoid(logits)
    sel = scores + bias_ref[:, 0:1]

    sub = lax.broadcasted_iota(I32, (per, bt), 0)
    grp_rows = []
    for g in range(N_GROUPS):
        blk = sel[g * per:(g + 1) * per]
        m1, i1 = _first_max(blk, sub, per)
        m2 = jnp.max(jnp.where(sub == i1, -jnp.inf, blk), axis=0, keepdims=True)
        grp_rows.append(m1 + m2)
    grp = _stack_rows(grp_rows, bt)
    gidx = lax.broadcasted_iota(I32, (N_GROUPS, bt), 0)
    keep = jnp.zeros((N_GROUPS, bt), F32)
    for _ in range(TOPK_GROUPS):
        _, first = _first_max(grp, gidx, N_GROUPS)
        hit = gidx == first
        keep = jnp.where(hit, 1.0, keep)
        grp = jnp.where(hit, -jnp.inf, grp)
    masked = jnp.concatenate(
        [jnp.where(keep[g:g + 1] > 0.0, sel[g * per:(g + 1) * per], -jnp.inf)
         for g in range(N_GROUPS)], axis=0)

    eidx = lax.broadcasted_iota(I32, (n_experts, bt), 0)
    chosen = jnp.zeros((n_experts, bt), F32)
    picks = []
    gates = []
    for _ in range(TOP_K):
        _, first = _first_max(masked, eidx, n_experts)
        hit = eidx == first
        picks.append(first)
        gates.append(jnp.sum(jnp.where(hit, scores, 0.0), axis=0, keepdims=True))
        chosen = jnp.where(hit, 1.0, chosen)
        masked = jnp.where(hit, -jnp.inf, masked)
    gate = _stack_rows(gates, bt)
    gate = gate / jnp.sum(gate, axis=0, keepdims=True) * ROUTE_SCALE

    earlier = (lax.broadcasted_iota(I32, (bt, bt), 0) < lax.broadcasted_iota(I32, (bt, bt), 1))
    before = _dot(chosen.astype(BF16), jnp.where(earlier, 1.0, 0.0).astype(BF16)) + carry_ref[:, 0:1]
    ranks = [jnp.sum(jnp.where(eidx == p, before, 0.0), axis=0, keepdims=True) for p in picks]

    e_ref[...] = _stack_rows(picks, bt)
    gate_ref[...] = gate
    rank_ref[...] = _stack_rows(ranks, bt).astype(I32)
    carry_ref[...] = carry_ref[...] + jnp.sum(chosen, axis=1, keepdims=True)
    cnt_ref[...] = carry_ref[...]


def moe_router(x, w_router, router_bias, bt=ROUTER_BLOCK_T):
    n, d = x.shape
    n_experts = w_router.shape[1]
    bt = min(bt, n)
    assert n % bt == 0 and n_experts % N_GROUPS == 0
    lanes = 128
    tok = pl.BlockSpec((TOP_K, bt), lambda i: (0, i))
    e, gate, rank, cnt = pl.pallas_call(
        functools.partial(_router_kernel, n_experts, bt),
        grid=(n // bt,),
        in_specs=[pl.BlockSpec((bt, d), lambda i: (i, 0)),
                  pl.BlockSpec((n_experts, d), lambda i: (0, 0)),
                  pl.BlockSpec((n_experts, lanes), lambda i: (0, 0))],
        out_specs=[tok, tok, tok, pl.BlockSpec((n_experts, lanes), lambda i: (0, 0))],
        out_shape=[jax.ShapeDtypeStruct((TOP_K, n), I32), jax.ShapeDtypeStruct((TOP_K, n), F32),
                   jax.ShapeDtypeStruct((TOP_K, n), I32),
                   jax.ShapeDtypeStruct((n_experts, lanes), F32)],
        scratch_shapes=[pltpu.VMEM((n_experts, lanes), F32)],
        compiler_params=_params(),
        name="moe_router",
    )(x, w_router.T, jnp.broadcast_to(router_bias.astype(F32)[:, None], (n_experts, lanes)))
    return e, gate, rank, cnt[:, 0].astype(I32)


def _dispatch_kernel(bt, n_experts, fill_start_ref, fill_len_ref, dest_ref, x_ref, xs_ref,
                     zero_ref, sem, fill_sem):
    i = pl.program_id(0)

    @pl.when(i == 0)
    def _():
        zero_ref[...] = jnp.zeros_like(zero_ref)

        def fill(e, carry):
            base = fill_start_ref[e]

            def zero_copy(r):
                return pltpu.make_async_copy(zero_ref.at[pl.ds(0, 1)],
                                             xs_ref.at[pl.ds(base + r, 1)], fill_sem)

            def start(r, c):
                zero_copy(r).start()
                return c

            def wait(r, c):
                zero_copy(r).wait()
                return c

            lax.fori_loop(0, fill_len_ref[e], start, 0)
            lax.fori_loop(0, fill_len_ref[e], wait, 0)
            return carry

        lax.fori_loop(0, n_experts, fill, 0)

    def row_copy(t, k):
        return pltpu.make_async_copy(x_ref.at[pl.ds(t, 1)],
                                     xs_ref.at[pl.ds(dest_ref[t * TOP_K + k], 1)], sem)

    def issue(t, carry):
        for k in range(TOP_K):
            row_copy(t, k).start()
        return carry

    def drain(t, carry):
        for k in range(TOP_K):
            row_copy(t, k).wait()
        return carry

    lax.fori_loop(0, bt, issue, 0)
    lax.fori_loop(0, bt, drain, 0)


def moe_dispatch(x, dest_flat, fill_start, fill_len, rows, bt=DISPATCH_BLOCK_T):
    n, d = x.shape
    bt = min(bt, n)
    assert n % bt == 0
    n_experts = fill_start.shape[0]
    return pl.pallas_call(
        functools.partial(_dispatch_kernel, bt, n_experts),
        grid_spec=pltpu.PrefetchScalarGridSpec(
            num_scalar_prefetch=2,
            grid=(n // bt,),
            in_specs=[pl.BlockSpec((bt * TOP_K,), lambda i, *_: (i,), memory_space=pltpu.SMEM),
                      pl.BlockSpec((bt, d), lambda i, *_: (i, 0))],
            out_specs=pl.BlockSpec(memory_space=pl.ANY),
            scratch_shapes=[pltpu.VMEM((8, d), F32),
                            pltpu.SemaphoreType.DMA, pltpu.SemaphoreType.DMA]),
        out_shape=jax.ShapeDtypeStruct((rows, d), F32),
        compiler_params=_params(),
        name="moe_dispatch",
    )(fill_start, fill_len, dest_flat, x)


def _expert_kernel(tile_e_ref, n_used_ref, xs_ref, wgu_ref, wdn_ref, ys_ref, wgu_b, wdn_b):
    i = pl.program_id(0)
    de = wdn_b.shape[0]
    changed = jnp.logical_or(i == 0, tile_e_ref[i] != tile_e_ref[jnp.maximum(i - 1, 0)])

    @pl.when(changed)
    def _():
        wgu_b[...] = wgu_ref[0].astype(BF16)
        wdn_b[...] = wdn_ref[0].astype(BF16)

    @pl.when(i < n_used_ref[0])
    def _():
        gu = _dot(xs_ref[...].astype(BF16), wgu_b[...])
        a = _silu(gu[:, :de]) * gu[:, de:]
        ys_ref[...] = _dot(a.astype(BF16), wdn_b[...])

    @pl.when(i >= n_used_ref[0])
    def _():
        ys_ref[...] = jnp.zeros_like(ys_ref)


def moe_experts(xs, tile_e, n_used, w_gu, w_dn, blk=EXPERT_BLOCK):
    rows, d = xs.shape
    n_tiles = rows // blk
    de = w_dn.shape[1]
    return pl.pallas_call(
        _expert_kernel,
        grid_spec=pltpu.PrefetchScalarGridSpec(
            num_scalar_prefetch=2,
            grid=(n_tiles,),
            in_specs=[pl.BlockSpec((blk, d), lambda i, te, nu: (jnp.minimum(i, nu[0] - 1), 0)),
                      pl.BlockSpec((1, d, 2 * de), lambda i, te, nu: (te[i], 0, 0)),
                      pl.BlockSpec((1, de, d), lambda i, te, nu: (te[i], 0, 0))],
            out_specs=pl.BlockSpec((blk, d), lambda i, te, nu: (i, 0)),
            scratch_shapes=[pltpu.VMEM((d, 2 * de), BF16), pltpu.VMEM((de, d), BF16)]),
        out_shape=jax.ShapeDtypeStruct((rows, d), F32),
        compiler_params=_params(),
        name="moe_experts",
    )(tile_e, n_used, xs, w_gu, w_dn)


def _combine_kernel(alpha, bt, dest_ref, ys_ref, gate_ref, x_ref, xb_ref, wsgu_ref, wsdn_ref,
                    g_ref, b_ref, xo_ref, xbo_ref, buf_ref, sem):
    de = wsdn_ref.shape[0]

    def row_copy(t, k):
        return pltpu.make_async_copy(ys_ref.at[pl.ds(dest_ref[t * TOP_K + k], 1)],
                                     buf_ref.at[k, pl.ds(t, 1)], sem)

    def issue(t, carry):
        for k in range(TOP_K):
            row_copy(t, k).start()
        return carry

    def drain(t, carry):
        for k in range(TOP_K):
            row_copy(t, k).wait()
        return carry

    lax.fori_loop(0, bt, issue, 0)
    sgu = _dot(xb_ref[...], wsgu_ref[...])
    shared = _dot((_silu(sgu[:, :de]) * sgu[:, de:]).astype(BF16), wsdn_ref[...])
    lax.fori_loop(0, bt, drain, 0)
    routed = gate_ref[:, 0:1] * buf_ref[0]
    for k in range(1, TOP_K):
        routed = routed + gate_ref[:, k:k + 1] * buf_ref[k]
    out = _layer_norm(alpha * x_ref[...] + (routed + shared), g_ref[...], b_ref[...])
    xo_ref[...] = out
    xbo_ref[...] = out.astype(BF16)


def moe_combine_ln(ys, dest_flat, gate_tm, x, xb, ws_gu, ws_dn, g, b, alpha, bt=COMBINE_BLOCK_T):
    n, d = x.shape
    bt = min(bt, n)
    assert n % bt == 0
    de = ws_dn.shape[0]
    row = pl.BlockSpec((bt, d), lambda i: (i, 0))
    vec = pl.BlockSpec((1, d), lambda i: (0, 0))
    return pl.pallas_call(
        functools.partial(_combine_kernel, alpha, bt),
        grid=(n // bt,),
        in_specs=[pl.BlockSpec((bt * TOP_K,), lambda i: (i,), memory_space=pltpu.SMEM),
                  pl.BlockSpec(memory_space=pl.ANY),
                  pl.BlockSpec((bt, TOP_K), lambda i: (i, 0)),
                  row, row,
                  pl.BlockSpec((d, 2 * de), lambda i: (0, 0)),
                  pl.BlockSpec((de, d), lambda i: (0, 0)),
                  vec, vec],
        out_specs=[row, row],
        out_shape=[jax.ShapeDtypeStruct((n, d), F32), jax.ShapeDtypeStruct((n, d), BF16)],
        scratch_shapes=[pltpu.VMEM((TOP_K, bt, d), F32), pltpu.SemaphoreType.DMA],
        compiler_params=_params(),
        name="moe_combine_ln",
    )(dest_flat, ys, gate_tm, x, xb, ws_gu, ws_dn, g.reshape(1, d), b.reshape(1, d))


def moe_ffn_ln(x, xb, w_router, router_bias, w_gu, w_dn, ws_gu, ws_dn, g, b, alpha):
    n, d = x.shape
    n_experts = w_router.shape[1]
    blk = EXPERT_BLOCK
    top_e, gate, rank, counts = moe_router(x, w_router, router_bias)
    padded = (counts + blk - 1) // blk * blk
    pend = jnp.cumsum(padded)
    pstart = pend - padded
    hot = top_e[:, :, None] == jnp.arange(n_experts, dtype=I32)
    dest = jnp.sum(jnp.where(hot, pstart, 0), axis=-1) + rank
    dest_flat = dest.T.reshape(-1)
    n_tiles = (n * TOP_K + n_experts * (blk - 1) + blk - 1) // blk
    tile_e = jnp.clip(jnp.searchsorted(pend, jnp.arange(n_tiles, dtype=I32) * blk, side='right'),
                      0, n_experts - 1).astype(I32)
    n_used = (pend[-1:] // blk).astype(I32)
    xs = moe_dispatch(x, dest_flat, (pstart + counts).astype(I32), (padded - counts).astype(I32),
                      n_tiles * blk)
    ys = moe_experts(xs, tile_e, n_used, w_gu, w_dn)
    return moe_combine_ln(ys, dest_flat, gate.T, x, xb, ws_gu, ws_dn, g, b, alpha)


def kernel(x_prompt, x_sample, state_hgrn, state_pool, cache_k, cache_v, ln_g, ln_b, w_in_even,
           lb_even, gnorm_even, pool_w_even, pool_scale_even, w_out_even, w_qkv_odd, rel_bias_odd,
           w_o_odd, w_router, router_bias, w_gu, w_dn, ws_gu, ws_dn):
    batch, seq, d = x_prompt.shape
    dec_batch, dec_seq, _ = x_sample.shape
    depth = ln_g.shape[0]
    n_p = batch * seq
    n_s = dec_batch * dec_seq
    width = w_in_even.shape[2] // 5
    n_heads_a = width // HEAD_DIM
    n_heads_c = d // HEAD_DIM
    alpha = (2 * depth) ** 0.25
    assert dec_seq >= POOL_PREV and seq >= max(POOL_PREV, WINDOW)

    x = jnp.concatenate([x_prompt.reshape(n_p, d), x_sample.reshape(n_s, d)], axis=0)
    xb = x.astype(BF16)

    p = jax.nn.softmax(lb_even.astype(F32), axis=0)
    lbs = jnp.cumsum(p, axis=0) - p[0]

    hgrn_p, pool_p, k_p, v_p, hgrn_s, pool_s, k_s, v_s = [], [], [], [], [], [], [], []
    for l in range(depth):
        j = l // 2
        if l % 2 == 0:
            proj = matmul(xb, w_in_even[j].astype(BF16))
            zero_state = jnp.zeros((batch, n_heads_a, HEAD_DIM, HEAD_DIM), F32)
            oa_p, sp = hgrn_mixer(proj, 0, batch, seq, zero_state, lbs[j], gnorm_even[j])
            oa_s, ss = hgrn_mixer(proj, n_p, dec_batch, dec_seq,
                                  jnp.swapaxes(state_hgrn[j].astype(F32), -1, -2), lbs[j],
                                  gnorm_even[j])
            pw = pool_w_even[j].astype(BF16)
            ob_p = pool_mixer(proj, 0, batch, seq, 0, jnp.zeros((batch, POOL_CARRY, width), F32),
                              pw, pool_scale_even[j])
            prev_s = jnp.pad(state_pool[j].astype(F32),
                             ((0, 0), (POOL_CARRY - POOL_PREV, 0), (0, 0)))
            ob_s = pool_mixer(proj, n_p, dec_batch, dec_seq, PAST_LEN, prev_s, pw,
                              pool_scale_even[j])
            mixed = jnp.concatenate([jnp.concatenate([oa_p, oa_s], axis=0),
                                     jnp.concatenate([ob_p, ob_s], axis=0)], axis=1)
            w_mix = w_out_even[j]
            u = proj[:, 4 * width:]
            hgrn_p.append(jnp.swapaxes(sp, -1, -2))
            hgrn_s.append(jnp.swapaxes(ss, -1, -2))
            pool_p.append(u[:n_p].reshape(batch, seq, width)[:, seq - POOL_PREV:])
            pool_s.append(u[n_p:].reshape(dec_batch, dec_seq, width)[:, dec_seq - POOL_PREV:])
        else:
            qkv = matmul(xb, w_qkv_odd[j].astype(BF16))
            lc = cache_k.shape[2]
            o_p = band_attention_prompt(qkv, batch, seq, rel_bias_odd[j])
            o_s = band_attention_sample(qkv, n_p, dec_batch, dec_seq,
                                        cache_k[j].reshape(dec_batch, lc, d),
                                        cache_v[j].reshape(dec_batch, lc, d), rel_bias_odd[j])
            mixed = jnp.concatenate([o_p, o_s], axis=0)
            w_mix = w_o_odd[j]
            kk = qkv[:, d:2 * d]
            vv = qkv[:, 2 * d:]
            keep = min(WINDOW, seq)
            k_p.append(kk[:n_p].reshape(batch, seq, n_heads_c, HEAD_DIM)[:, seq - keep:])
            v_p.append(vv[:n_p].reshape(batch, seq, n_heads_c, HEAD_DIM)[:, seq - keep:])
            k_s.append(kk[n_p:].reshape(dec_batch, dec_seq, n_heads_c, HEAD_DIM))
            v_s.append(vv[n_p:].reshape(dec_batch, dec_seq, n_heads_c, HEAD_DIM))
        x, xb = matmul_residual_ln(mixed, w_mix.astype(BF16), x, ln_g[l, 0], ln_b[l, 0], alpha)
        x, xb = moe_ffn_ln(x, xb, w_router[l], router_bias[l], w_gu[l], w_dn[l],
                           ws_gu[l].astype(BF16), ws_dn[l].astype(BF16), ln_g[l, 1], ln_b[l, 1],
                           alpha)

    y_p = x[:n_p].reshape(batch, seq, d)
    y_s = x[n_p:].reshape(dec_batch, dec_seq, d)
    return (y_p, y_s, jnp.stack(hgrn_p), jnp.stack(pool_p), jnp.stack(k_p), jnp.stack(v_p),
            jnp.stack(hgrn_s), jnp.stack(pool_s), jnp.stack(k_s), jnp.stack(v_s))
```

```python
import functools

import numpy as np
import jax
import jax.numpy as jnp
from jax import lax
from jax.experimental import pallas as pl
from jax.experimental.pallas import tpu as pltpu

F32 = jnp.float32
BF16 = jnp.bfloat16
I32 = jnp.int32

CHUNK = 64
HEAD_DIM = 128
POOL_WINDOWS = (2, 4, 8, 16)
POOL_PREV = max(POOL_WINDOWS) - 1
POOL_CARRY = POOL_PREV + 1
LEFT_CHUNKS = 8
WINDOW = LEFT_CHUNKS * CHUNK
MAX_REL = 128
N_GROUPS = 8
TOPK_GROUPS = 4
TOP_K = 8
ROUTE_SCALE = 2.5
LN_EPS = 1e-5
RMS_EPS = 1e-6
PAST_LEN = 1024
NEG = -1e30

VMEM_LIMIT_BYTES = 56 * 1024 * 1024
MM_BLOCK_M = 1024
MM_BLOCK_N = 1024
LN_BLOCK_M = 512
HGRN_BLOCK_T = 128
ATTN_BLOCK_Q = 256
ATTN_HEADS_PER_STEP = 4
POOL_BLOCK_T = 512
ROUTER_BLOCK_T = 512
DISPATCH_BLOCK_T = 128
EXPERT_BLOCK = 256
COMBINE_BLOCK_T = 128


def _params(**kw):
    return pltpu.CompilerParams(vmem_limit_bytes=VMEM_LIMIT_BYTES, **kw)


def _sigmoid(x):
    return 1.0 / (1.0 + jnp.exp(-x))


def _silu(x):
    return x * _sigmoid(x)


def _dot(a, b):
    return jnp.dot(a, b, preferred_element_type=F32)


def _dot_nt(a, b):
    return lax.dot_general(a, b, (((1,), (1,)), ((), ())), preferred_element_type=F32)


def _dot_tn(a, b):
    return lax.dot_general(a, b, (((0,), (0,)), ((), ())), preferred_element_type=F32)


def _layer_norm(y, g, b):
    mu = jnp.mean(y, axis=-1, keepdims=True)
    yc = y - mu
    var = jnp.mean(yc * yc, axis=-1, keepdims=True)
    return yc * lax.rsqrt(var + LN_EPS) * g + b


def _mm_kernel(x_ref, w_ref, o_ref):
    o_ref[...] = _dot(x_ref[...], w_ref[...]).astype(o_ref.dtype)


def matmul(x, w, out_dtype=F32, bm=MM_BLOCK_M, bn=MM_BLOCK_N):
    m, k = x.shape
    n = w.shape[1]
    bm, bn = min(bm, m), min(bn, n)
    assert m % bm == 0 and n % bn == 0
    return pl.pallas_call(
        _mm_kernel,
        grid=(m // bm, n // bn),
        in_specs=[pl.BlockSpec((bm, k), lambda i, j: (i, 0)),
                  pl.BlockSpec((k, bn), lambda i, j: (0, j))],
        out_specs=pl.BlockSpec((bm, bn), lambda i, j: (i, j)),
        out_shape=jax.ShapeDtypeStruct((m, n), out_dtype),
        compiler_params=_params(),
        name="matmul",
    )(x, w)


def _mm_ln_kernel(alpha, a_ref, w_ref, res_ref, g_ref, b_ref, x_ref, xb_ref):
    y = alpha * res_ref[...] + _dot(a_ref[...], w_ref[...])
    out = _layer_norm(y, g_ref[...], b_ref[...])
    x_ref[...] = out
    xb_ref[...] = out.astype(BF16)


def matmul_residual_ln(a, w, res, g, b, alpha, bm=LN_BLOCK_M):
    m, k = a.shape
    d = w.shape[1]
    bm = min(bm, m)
    assert m % bm == 0
    return pl.pallas_call(
        functools.partial(_mm_ln_kernel, alpha),
        grid=(m // bm,),
        in_specs=[pl.BlockSpec((bm, k), lambda i: (i, 0)),
                  pl.BlockSpec((k, d), lambda i: (0, 0)),
                  pl.BlockSpec((bm, d), lambda i: (i, 0)),
                  pl.BlockSpec((1, d), lambda i: (0, 0)),
                  pl.BlockSpec((1, d), lambda i: (0, 0))],
        out_specs=[pl.BlockSpec((bm, d), lambda i: (i, 0)),
                   pl.BlockSpec((bm, d), lambda i: (i, 0))],
        out_shape=[jax.ShapeDtypeStruct((m, d), F32), jax.ShapeDtypeStruct((m, d), BF16)],
        compiler_params=_params(),
        name="matmul_residual_ln",
    )(a, w, res, g.reshape(1, d), b.reshape(1, d))


def _hgrn_tables(c):
    t = np.arange(c)
    levels = []
    b = c // 2
    while b >= 1:
        levels.append(b)
        b //= 2
    sel = [(t[None, :] <= t[:, None]).astype(np.float32)]
    lvl = np.where(np.eye(c, dtype=bool), 0, -1).astype(np.int32)
    for l, b in enumerate(levels):
        blk = t // b
        odd = (blk % 2) == 1
        ref_row = np.where(odd, blk * b - 1, blk * b + b - 1)
        sel.append((t[None, :] <= ref_row[:, None]).astype(np.float32))
        same = (t[:, None] // (2 * b)) == (t[None, :] // (2 * b))
        lvl = np.where(same & odd[:, None] & (~odd)[None, :], l + 1, lvl)
    sel = np.concatenate(sel, axis=0)
    return np.concatenate([sel, sel, sel], axis=1), lvl


def _hgrn_kernel(n_heads, c, q_ref, f_ref, i_ref, g_ref, s0_ref, lb_ref, gn_ref, sel_ref, lvl_ref,
                 o_ref, sfin_ref, state_ref, cums_ref, qw_ref, kw_ref):
    j = pl.program_id(1)

    @pl.when(j == 0)
    def _():
        state_ref[...] = s0_ref[0]

    n_lvl = qw_ref.shape[0]
    lb = lb_ref[...]
    f = lb + (1.0 - lb) * _sigmoid(f_ref[...])
    k = 1.0 - f
    q = _silu(q_ref[...])
    lg = jnp.log(f)
    hi = lg.astype(BF16)
    rest = lg - hi.astype(F32)
    mid = rest.astype(BF16)
    lo = (rest - mid.astype(F32)).astype(BF16)
    cums_ref[...] = _dot(sel_ref[...], jnp.concatenate([hi, mid, lo], axis=0))
    cum = cums_ref[0:c]
    qw_ref[0] = q.astype(BF16)
    kw_ref[0] = k.astype(BF16)
    for l in range(1, n_lvl):
        w = jnp.exp(-jnp.abs(cum - cums_ref[l * c:(l + 1) * c]))
        qw_ref[l] = (q * w).astype(BF16)
        kw_ref[l] = (k * w).astype(BF16)
    last = cum[c - 1:c, :]
    q_in = (q * jnp.exp(cum)).astype(BF16)
    k_end = (k * jnp.exp(last - cum)).astype(BF16)
    decay = jnp.exp(last)
    lvl = lvl_ref[...]
    for h in range(n_heads):
        hs = slice(h * HEAD_DIM, (h + 1) * HEAD_DIM)
        st = state_ref[h]
        o = _dot_nt(q_in[:, hs], st.astype(BF16))
        scores = jnp.zeros((c, c), F32)
        for l in range(n_lvl):
            scores = jnp.where(lvl == l, _dot_nt(qw_ref[l, :, hs], kw_ref[l, :, hs]), scores)
        v = i_ref[:, hs].astype(BF16)
        o = o + _dot(scores.astype(BF16), v)
        state_ref[h] = st * decay[:, hs] + _dot_tn(v, k_end[:, hs])
        o = o * lax.rsqrt(jnp.mean(o * o, axis=-1, keepdims=True) + RMS_EPS) * gn_ref[...]
        o_ref[:, hs] = (o * _silu(g_ref[:, hs])).astype(o_ref.dtype)

    @pl.when(j == pl.num_programs(1) - 1)
    def _():
        sfin_ref[0] = state_ref[...]


def hgrn_mixer(proj, row0, batch, seq, s0_t, lb, gnorm):
    width = proj.shape[1] // 5
    n_heads = width // HEAD_DIM
    c = HGRN_BLOCK_T if seq % HGRN_BLOCK_T == 0 else seq
    n = seq // c
    assert row0 % c == 0 and c & (c - 1) == 0
    blk0 = row0 // c
    sel, lvl = _hgrn_tables(c)
    n_lvl = sel.shape[0] // c

    def sec(s):
        return pl.BlockSpec((c, width), lambda b, j: (blk0 + b * n + j, s))

    state_spec = pl.BlockSpec((1, n_heads, HEAD_DIM, HEAD_DIM), lambda b, j: (b, 0, 0, 0))
    return pl.pallas_call(
        functools.partial(_hgrn_kernel, n_heads, c),
        grid=(batch, n),
        in_specs=[sec(0), sec(1), sec(2), sec(3), state_spec,
                  pl.BlockSpec((1, width), lambda b, j: (0, 0)),
                  pl.BlockSpec((1, HEAD_DIM), lambda b, j: (0, 0)),
                  pl.BlockSpec(sel.shape, lambda b, j: (0, 0)),
                  pl.BlockSpec(lvl.shape, lambda b, j: (0, 0))],
        out_specs=[pl.BlockSpec((c, width), lambda b, j: (b * n + j, 0)), state_spec],
        out_shape=[jax.ShapeDtypeStruct((batch * seq, width), BF16),
                   jax.ShapeDtypeStruct((batch, n_heads, HEAD_DIM, HEAD_DIM), F32)],
        scratch_shapes=[pltpu.VMEM((n_heads, HEAD_DIM, HEAD_DIM), F32),
                        pltpu.VMEM((n_lvl * c, width), F32),
                        pltpu.VMEM((n_lvl, c, width), BF16),
                        pltpu.VMEM((n_lvl, c, width), BF16)],
        compiler_params=_params(),
        name="hgrn_mixer",
    )(proj, proj, proj, proj, s0_t, lb.reshape(1, width), gnorm.reshape(1, HEAD_DIM),
      jnp.asarray(sel).astype(BF16), jnp.asarray(lvl))


def _pool_kernel(pos0, bt, group, u_ref, prev_ref, w_ref, scale_ref, o_ref, ext_ref):
    j = pl.program_id(1)

    @pl.when(j == 0)
    def _():
        ext_ref[0:POOL_CARRY] = prev_ref[0]

    u = u_ref[...]
    ext_ref[POOL_CARRY:POOL_CARRY + bt] = u
    pos = pos0 + j * bt + lax.broadcasted_iota(I32, (bt, 1), 0)
    for gi, win in enumerate(POOL_WINDOWS):
        cs = slice(gi * group, (gi + 1) * group)
        s = ext_ref[:, cs]
        step = 1
        while step < win:
            s = s + pltpu.roll(s, step, axis=0)
            step *= 2
        cnt = jnp.minimum(pos + 1, win).astype(F32)
        d = s[POOL_CARRY:] / cnt - u[:, cs]
        y = _dot(d.astype(BF16), w_ref[gi]) * scale_ref[:, cs]
        o_ref[:, cs] = y.astype(o_ref.dtype)
    ext_ref[0:POOL_CARRY] = ext_ref[bt:bt + POOL_CARRY]


def pool_mixer(proj, row0, batch, seq, pos0, prev, pool_w, pool_scale):
    width = proj.shape[1] // 5
    group = width // len(POOL_WINDOWS)
    bt = min(POOL_BLOCK_T, seq)
    n = seq // bt
    assert seq % bt == 0 and row0 % bt == 0 and sum(POOL_WINDOWS) // 2 <= POOL_CARRY
    blk0 = row0 // bt
    return pl.pallas_call(
        functools.partial(_pool_kernel, pos0, bt, group),
        grid=(batch, n),
        in_specs=[pl.BlockSpec((bt, width), lambda b, j: (blk0 + b * n + j, 4)),
                  pl.BlockSpec((1, POOL_CARRY, width), lambda b, j: (b, 0, 0)),
                  pl.BlockSpec(pool_w.shape, lambda b, j: (0, 0, 0)),
                  pl.BlockSpec((1, width), lambda b, j: (0, 0))],
        out_specs=pl.BlockSpec((bt, width), lambda b, j: (b * n + j, 0)),
        out_shape=jax.ShapeDtypeStruct((batch * seq, width), BF16),
        scratch_shapes=[pltpu.VMEM((bt + POOL_CARRY, width), F32)],
        compiler_params=_params(),
        name="pool_mixer",
    )(proj, prev, pool_w, pool_scale.reshape(1, width))


def _rel_bias(table, dist):
    idx = jnp.clip(dist, -(CHUNK - 1), MAX_REL) + (CHUNK - 1)
    return table[:, idx].astype(F32)


def _attn_prompt_kernel(bq, hps, scale, q_ref, k0_ref, k1_ref, k2_ref, v0_ref, v1_ref, v2_ref,
                        vals_ref, o_ref, bias_ref):
    b = pl.program_id(1)
    i = pl.program_id(2)

    @pl.when(jnp.logical_and(b == 0, i == 0))
    def _():
        span = vals_ref.shape[-1]
        qi = lax.broadcasted_iota(I32, (bq, 3 * bq), 0)
        kj = lax.broadcasted_iota(I32, (bq, 3 * bq), 1)
        lo = (qi // CHUNK) * CHUNK
        in_band = jnp.logical_and(kj >= lo, kj < lo + WINDOW + CHUNK)
        for hh in range(hps):
            rows = jnp.broadcast_to(vals_ref[0, hh:hh + 1, :], (bq, span))
            toep = pltpu.roll(rows, span - (bq - 1), 1, stride=1, stride_axis=0)
            bias_ref[hh] = jnp.where(in_band, toep[:, :3 * bq], NEG)

    ks = (k0_ref, k1_ref, k2_ref)
    vs = (v0_ref, v1_ref, v2_ref)
    for hh in range(hps):
        hs = slice(hh * HEAD_DIM, (hh + 1) * HEAD_DIM)
        q = q_ref[:, hs]
        s = []
        for p in range(3):
            sp = _dot_nt(q, ks[p][:, hs]) * scale + bias_ref[hh, :, p * bq:(p + 1) * bq]
            if p < 2:
                sp = jnp.where(i >= 2 - p, sp, NEG)
            s.append(sp)
        m = jnp.maximum(jnp.maximum(jnp.max(s[0], axis=-1, keepdims=True),
                                    jnp.max(s[1], axis=-1, keepdims=True)),
                        jnp.max(s[2], axis=-1, keepdims=True))
        e = [jnp.exp(sp - m) for sp in s]
        inv = 1.0 / (jnp.sum(e[0], axis=-1, keepdims=True) + jnp.sum(e[1], axis=-1, keepdims=True)
                     + jnp.sum(e[2], axis=-1, keepdims=True))
        o = _dot((e[0] * inv).astype(BF16), vs[0][:, hs])
        o = o + _dot((e[1] * inv).astype(BF16), vs[1][:, hs])
        o = o + _dot((e[2] * inv).astype(BF16), vs[2][:, hs])
        o_ref[:, hs] = o.astype(o_ref.dtype)


def band_attention_prompt(qkv, batch, seq, rel_bias):
    d = qkv.shape[1] // 3
    n_heads = d // HEAD_DIM
    bq = ATTN_BLOCK_Q
    hps = min(ATTN_HEADS_PER_STEP, n_heads)
    assert seq % bq == 0 and WINDOW == 2 * bq and bq % CHUNK == 0 and n_heads % hps == 0
    nq = seq // bq
    ng = n_heads // hps
    offs = jnp.arange(4 * bq) - (bq - 1)
    vals = _rel_bias(rel_bias, WINDOW - offs).reshape(ng, hps, 4 * bq)

    def kv(sec, p):
        return pl.BlockSpec((bq, hps * HEAD_DIM),
                            lambda g, b, i: (b * nq + jnp.maximum(i - 2 + p, 0), sec * ng + g))

    return pl.pallas_call(
        functools.partial(_attn_prompt_kernel, bq, hps, HEAD_DIM ** -0.5),
        grid=(ng, batch, nq),
        in_specs=[pl.BlockSpec((bq, hps * HEAD_DIM), lambda g, b, i: (b * nq + i, g)),
                  kv(1, 0), kv(1, 1), kv(1, 2), kv(2, 0), kv(2, 1), kv(2, 2),
                  pl.BlockSpec((1, hps, 4 * bq), lambda g, b, i: (g, 0, 0))],
        out_specs=pl.BlockSpec((bq, hps * HEAD_DIM), lambda g, b, i: (b * nq + i, g)),
        out_shape=jax.ShapeDtypeStruct((batch * seq, d), BF16),
        scratch_shapes=[pltpu.VMEM((hps, bq, 3 * bq), F32)],
        compiler_params=_params(),
        name="band_attention_prompt",
    )(qkv, qkv, qkv, qkv, qkv, qkv, qkv, vals)


def _attn_sample_kernel(n_heads, lc, scale, qkv_ref, ck_ref, cv_ref, bias_ref, o_ref):
    d = n_heads * HEAD_DIM
    for h in range(n_heads):
        hs = slice(h * HEAD_DIM, (h + 1) * HEAD_DIM)
        q = qkv_ref[:, h * HEAD_DIM:(h + 1) * HEAD_DIM].astype(BF16)
        kn = qkv_ref[:, d + h * HEAD_DIM:d + (h + 1) * HEAD_DIM].astype(BF16)
        vn = qkv_ref[:, 2 * d + h * HEAD_DIM:2 * d + (h + 1) * HEAD_DIM].astype(BF16)
        kc = ck_ref[0, :, hs].astype(BF16)
        vc = cv_ref[0, :, hs].astype(BF16)
        sc = _dot_nt(q, kc) * scale + bias_ref[h, :, 0:lc]
        sn = _dot_nt(q, kn) * scale + bias_ref[h, :, lc:]
        m = jnp.maximum(jnp.max(sc, axis=-1, keepdims=True), jnp.max(sn, axis=-1, keepdims=True))
        ec = jnp.exp(sc - m)
        en = jnp.exp(sn - m)
        inv = 1.0 / (jnp.sum(ec, axis=-1, keepdims=True) + jnp.sum(en, axis=-1, keepdims=True))
        o = _dot((ec * inv).astype(BF16), vc) + _dot((en * inv).astype(BF16), vn)
        o_ref[:, hs] = o.astype(o_ref.dtype)


def band_attention_sample(qkv, row0, batch, seq, cache_k, cache_v, rel_bias):
    d = qkv.shape[1] // 3
    n_heads = d // HEAD_DIM
    lc = cache_k.shape[1]
    assert row0 % seq == 0 and lc % 128 == 0
    blk0 = row0 // seq
    dist = jnp.arange(seq)[:, None] + lc - jnp.arange(lc + seq)[None, :]
    bias = _rel_bias(rel_bias, dist)
    return pl.pallas_call(
        functools.partial(_attn_sample_kernel, n_heads, lc, HEAD_DIM ** -0.5),
        grid=(batch,),
        in_specs=[pl.BlockSpec((seq, 3 * d), lambda b: (blk0 + b, 0)),
                  pl.BlockSpec((1, lc, d), lambda b: (b, 0, 0)),
                  pl.BlockSpec((1, lc, d), lambda b: (b, 0, 0)),
                  pl.BlockSpec(bias.shape, lambda b: (0, 0, 0))],
        out_specs=pl.BlockSpec((seq, d), lambda b: (b, 0)),
        out_shape=jax.ShapeDtypeStruct((batch * seq, d), BF16),
        compiler_params=_params(),
        name="band_attention_sample",
    )(qkv, cache_k, cache_v, bias)


def _first_max(vals, idx, sentinel):
    m = jnp.max(vals, axis=0, keepdims=True)
    first = jnp.min(jnp.where(vals == m, idx, sentinel), axis=0, keepdims=True)
    return m, first


def _stack_rows(rows, lanes):
    ridx = lax.broadcasted_iota(I32, (len(rows), lanes), 0)
    out = jnp.zeros((len(rows), lanes), rows[0].dtype)
    for r, row in enumerate(rows):
        out = jnp.where(ridx == r, row, out)
    return out


def _router_kernel(n_experts, bt, x_ref, wt_ref, bias_ref, e_ref, gate_ref, rank_ref, cnt_ref,
                   carry_ref):
    i = pl.program_id(0)

    @pl.when(i == 0)
    def _():
        carry_ref[...] = jnp.zeros_like(carry_ref)

    per = n_experts // N_GROUPS
    logits = lax.dot_general(wt_ref[...], x_ref[...], (((1,), (1,)), ((), ())),
                             precision=lax.Precision.HIGHEST, preferred_element_type=F32)
    scores = _sigmoid(logits)
    sel = scores + bias_ref[:, 0:1]

    sub = lax.broadcasted_iota(I32, (per, bt), 0)
    grp_rows = []
    for g in range(N_GROUPS):
        blk = sel[g * per:(g + 1) * per]
        m1, i1 = _first_max(blk, sub, per)
        m2 = jnp.max(jnp.where(sub == i1, -jnp.inf, blk), axis=0, keepdims=True)
        grp_rows.append(m1 + m2)
    grp = _stack_rows(grp_rows, bt)
    gidx = lax.broadcasted_iota(I32, (N_GROUPS, bt), 0)
    keep = jnp.zeros((N_GROUPS, bt), F32)
    for _ in range(TOPK_GROUPS):
        _, first = _first_max(grp, gidx, N_GROUPS)
        hit = gidx == first
        keep = jnp.where(hit, 1.0, keep)
        grp = jnp.where(hit, -jnp.inf, grp)
    masked = jnp.concatenate(
        [jnp.where(keep[g:g + 1] > 0.0, sel[g * per:(g + 1) * per], -jnp.inf)
         for g in range(N_GROUPS)], axis=0)

    eidx = lax.broadcasted_iota(I32, (n_experts, bt), 0)
    chosen = jnp.zeros((n_experts, bt), F32)
    picks = []
    gates = []
    for _ in range(TOP_K):
        _, first = _first_max(masked, eidx, n_experts)
        hit = eidx == first
        picks.append(first)
        gates.append(jnp.sum(jnp.where(hit, scores, 0.0), axis=0, keepdims=True))
        chosen = jnp.where(hit, 1.0, chosen)
        masked = jnp.where(hit, -jnp.inf, masked)
    gate = _stack_rows(gates, bt)
    gate = gate / jnp.sum(gate, axis=0, keepdims=True) * ROUTE_SCALE

    earlier = (lax.broadcasted_iota(I32, (bt, bt), 0) < lax.broadcasted_iota(I32, (bt, bt), 1))
    before = _dot(chosen.astype(BF16), jnp.where(earlier, 1.0, 0.0).astype(BF16)) + carry_ref[:, 0:1]
    ranks = [jnp.sum(jnp.where(eidx == p, before, 0.0), axis=0, keepdims=True) for p in picks]

    e_ref[...] = _stack_rows(picks, bt)
    gate_ref[...] = gate
    rank_ref[...] = _stack_rows(ranks, bt).astype(I32)
    carry_ref[...] = carry_ref[...] + jnp.sum(chosen, axis=1, keepdims=True)
    cnt_ref[...] = carry_ref[...]


def moe_router(x, w_router, router_bias, bt=ROUTER_BLOCK_T):
    n, d = x.shape
    n_experts = w_router.shape[1]
    bt = min(bt, n)
    assert n % bt == 0 and n_experts % N_GROUPS == 0
    lanes = 128
    tok = pl.BlockSpec((TOP_K, bt), lambda i: (0, i))
    e, gate, rank, cnt = pl.pallas_call(
        functools.partial(_router_kernel, n_experts, bt),
        grid=(n // bt,),
        in_specs=[pl.BlockSpec((bt, d), lambda i: (i, 0)),
                  pl.BlockSpec((n_experts, d), lambda i: (0, 0)),
                  pl.BlockSpec((n_experts, lanes), lambda i: (0, 0))],
        out_specs=[tok, tok, tok, pl.BlockSpec((n_experts, lanes), lambda i: (0, 0))],
        out_shape=[jax.ShapeDtypeStruct((TOP_K, n), I32), jax.ShapeDtypeStruct((TOP_K, n), F32),
                   jax.ShapeDtypeStruct((TOP_K, n), I32),
                   jax.ShapeDtypeStruct((n_experts, lanes), F32)],
        scratch_shapes=[pltpu.VMEM((n_experts, lanes), F32)],
        compiler_params=_params(),
        name="moe_router",
    )(x, w_router.T, jnp.broadcast_to(router_bias.astype(F32)[:, None], (n_experts, lanes)))
    return e, gate, rank, cnt[:, 0].astype(I32)


def _dispatch_kernel(bt, n_experts, fill_start_ref, fill_len_ref, dest_ref, x_ref, xs_ref,
                     zero_ref, sem, fill_sem):
    i = pl.program_id(0)

    @pl.when(i == 0)
    def _():
        zero_ref[...] = jnp.zeros_like(zero_ref)

        def fill(e, carry):
            base = fill_start_ref[e]

            def zero_copy(r):
                return pltpu.make_async_copy(zero_ref.at[pl.ds(0, 1)],
                                             xs_ref.at[pl.ds(base + r, 1)], fill_sem)

            def start(r, c):
                zero_copy(r).start()
                return c

            def wait(r, c):
                zero_copy(r).wait()
                return c

            lax.fori_loop(0, fill_len_ref[e], start, 0)
            lax.fori_loop(0, fill_len_ref[e], wait, 0)
            return carry

        lax.fori_loop(0, n_experts, fill, 0)

    def row_copy(t, k):
        return pltpu.make_async_copy(x_ref.at[pl.ds(t, 1)],
                                     xs_ref.at[pl.ds(dest_ref[t * TOP_K + k], 1)], sem)

    def drain(t, carry):
        for k in range(TOP_K):
            row_copy(t, k).wait()
        return carry

    for t in range(bt):
        for k in range(TOP_K):
            row_copy(t, k).start()
    lax.fori_loop(0, bt, drain, 0)


def moe_dispatch(x, dest_flat, fill_start, fill_len, rows, bt=DISPATCH_BLOCK_T):
    n, d = x.shape
    bt = min(bt, n)
    assert n % bt == 0
    n_experts = fill_start.shape[0]
    return pl.pallas_call(
        functools.partial(_dispatch_kernel, bt, n_experts),
        grid_spec=pltpu.PrefetchScalarGridSpec(
            num_scalar_prefetch=2,
            grid=(n // bt,),
            in_specs=[pl.BlockSpec((bt * TOP_K,), lambda i, *_: (i,), memory_space=pltpu.SMEM),
                      pl.BlockSpec((bt, d), lambda i, *_: (i, 0))],
            out_specs=pl.BlockSpec(memory_space=pl.ANY),
            scratch_shapes=[pltpu.VMEM((8, d), F32),
                            pltpu.SemaphoreType.DMA, pltpu.SemaphoreType.DMA]),
        out_shape=jax.ShapeDtypeStruct((rows, d), F32),
        compiler_params=_params(),
        name="moe_dispatch",
    )(fill_start, fill_len, dest_flat, x)


def _expert_kernel(tile_e_ref, n_used_ref, xs_ref, wgu_ref, wdn_ref, ys_ref, wgu_b, wdn_b):
    i = pl.program_id(0)
    de = wdn_b.shape[0]
    changed = jnp.logical_or(i == 0, tile_e_ref[i] != tile_e_ref[jnp.maximum(i - 1, 0)])

    @pl.when(changed)
    def _():
        wgu_b[...] = wgu_ref[0].astype(BF16)
        wdn_b[...] = wdn_ref[0].astype(BF16)

    @pl.when(i < n_used_ref[0])
    def _():
        gu = _dot(xs_ref[...].astype(BF16), wgu_b[...])
        a = _silu(gu[:, :de]) * gu[:, de:]
        ys_ref[...] = _dot(a.astype(BF16), wdn_b[...])

    @pl.when(i >= n_used_ref[0])
    def _():
        ys_ref[...] = jnp.zeros_like(ys_ref)


def moe_experts(xs, tile_e, n_used, w_gu, w_dn, blk=EXPERT_BLOCK):
    rows, d = xs.shape
    n_tiles = rows // blk
    de = w_dn.shape[1]
    return pl.pallas_call(
        _expert_kernel,
        grid_spec=pltpu.PrefetchScalarGridSpec(
            num_scalar_prefetch=2,
            grid=(n_tiles,),
            in_specs=[pl.BlockSpec((blk, d), lambda i, te, nu: (jnp.minimum(i, nu[0] - 1), 0)),
                      pl.BlockSpec((1, d, 2 * de), lambda i, te, nu: (te[i], 0, 0)),
                      pl.BlockSpec((1, de, d), lambda i, te, nu: (te[i], 0, 0))],
            out_specs=pl.BlockSpec((blk, d), lambda i, te, nu: (i, 0)),
            scratch_shapes=[pltpu.VMEM((d, 2 * de), BF16), pltpu.VMEM((de, d), BF16)]),
        out_shape=jax.ShapeDtypeStruct((rows, d), F32),
        compiler_params=_params(),
        name="moe_experts",
    )(tile_e, n_used, xs, w_gu, w_dn)


def _combine_kernel(alpha, bt, dest_ref, dest_next_ref, ys_ref, gate_ref, x_ref, xb_ref, wsgu_ref,
                    wsdn_ref, g_ref, b_ref, xo_ref, xbo_ref, buf_a, buf_b, sem_a, sem_b):
    i = pl.program_id(0)
    de = wsdn_ref.shape[0]
    per_tile = bt * TOP_K

    def row_copy(idx_ref, base, buf, sem, t, k):
        return pltpu.make_async_copy(ys_ref.at[pl.ds(idx_ref[base + t * TOP_K + k], 1)],
                                     buf.at[k, pl.ds(t, 1)], sem)

    def issue_unrolled(idx_ref, base, buf, sem):
        for t in range(bt):
            for k in range(TOP_K):
                row_copy(idx_ref, base, buf, sem, t, k).start()

    def drain(idx_ref, base, buf, sem):
        def body(t, carry):
            for k in range(TOP_K):
                row_copy(idx_ref, base, buf, sem, t, k).wait()
            return carry
        lax.fori_loop(0, bt, body, 0)

    def compute(r0, buf):
        rows = pl.ds(r0, bt)
        sgu = _dot(xb_ref[rows, :], wsgu_ref[...])
        shared = _dot((_silu(sgu[:, :de]) * sgu[:, de:]).astype(BF16), wsdn_ref[...])
        routed = gate_ref[rows, 0:1] * buf[0]
        for k in range(1, TOP_K):
            routed = routed + gate_ref[rows, k:k + 1] * buf[k]
        out = _layer_norm(alpha * x_ref[rows, :] + (routed + shared), g_ref[...], b_ref[...])
        xo_ref[rows, :] = out
        xbo_ref[rows, :] = out.astype(BF16)

    @pl.when(i == 0)
    def _():
        def body(t, carry):
            for k in range(TOP_K):
                row_copy(dest_ref, 0, buf_a, sem_a, t, k).start()
            return carry
        lax.fori_loop(0, bt, body, 0)

    drain(dest_ref, 0, buf_a, sem_a)
    issue_unrolled(dest_ref, per_tile, buf_b, sem_b)
    compute(0, buf_a)
    drain(dest_ref, per_tile, buf_b, sem_b)
    issue_unrolled(dest_next_ref, 0, buf_a, sem_a)
    compute(bt, buf_b)

    @pl.when(i == pl.num_programs(0) - 1)
    def _():
        drain(dest_next_ref, 0, buf_a, sem_a)


def moe_combine_ln(ys, dest_flat, gate_tm, x, xb, ws_gu, ws_dn, g, b, alpha, bt=COMBINE_BLOCK_T):
    n, d = x.shape
    bt = min(bt, n // 2)
    assert n % (2 * bt) == 0
    n_tiles = n // bt
    de = ws_dn.shape[0]
    row = pl.BlockSpec((2 * bt, d), lambda i: (i, 0))
    vec = pl.BlockSpec((1, d), lambda i: (0, 0))
    return pl.pallas_call(
        functools.partial(_combine_kernel, alpha, bt),
        grid=(n_tiles // 2,),
        in_specs=[pl.BlockSpec((2 * bt * TOP_K,), lambda i: (i,), memory_space=pltpu.SMEM),
                  pl.BlockSpec((bt * TOP_K,), lambda i: (jnp.minimum(2 * i + 2, n_tiles - 1),),
                               memory_space=pltpu.SMEM),
                  pl.BlockSpec(memory_space=pl.ANY),
                  pl.BlockSpec((2 * bt, TOP_K), lambda i: (i, 0)),
                  row, row,
                  pl.BlockSpec((d, 2 * de), lambda i: (0, 0)),
                  pl.BlockSpec((de, d), lambda i: (0, 0)),
                  vec, vec],
        out_specs=[row, row],
        out_shape=[jax.ShapeDtypeStruct((n, d), F32), jax.ShapeDtypeStruct((n, d), BF16)],
        scratch_shapes=[pltpu.VMEM((TOP_K, bt, d), F32), pltpu.VMEM((TOP_K, bt, d), F32),
                        pltpu.SemaphoreType.DMA, pltpu.SemaphoreType.DMA],
        compiler_params=_params(),
        name="moe_combine_ln",
    )(dest_flat, dest_flat, ys, gate_tm, x, xb, ws_gu, ws_dn, g.reshape(1, d), b.reshape(1, d))


def moe_ffn_ln(x, xb, w_router, router_bias, w_gu, w_dn, ws_gu, ws_dn, g, b, alpha):
    n, d = x.shape
    n_experts = w_router.shape[1]
    blk = EXPERT_BLOCK
    top_e, gate, rank, counts = moe_router(x, w_router, router_bias)
    padded = (counts + blk - 1) // blk * blk
    pend = jnp.cumsum(padded)
    pstart = pend - padded
    hot = top_e[:, :, None] == jnp.arange(n_experts, dtype=I32)
    dest = jnp.sum(jnp.where(hot, pstart, 0), axis=-1) + rank
    dest_flat = dest.T.reshape(-1)
    n_tiles = (n * TOP_K + n_experts * (blk - 1) + blk - 1) // blk
    tile_row0 = jnp.arange(n_tiles, dtype=I32) * blk
    tile_e = jnp.minimum(jnp.sum((pend[None, :] <= tile_row0[:, None]).astype(I32), axis=1),
                         n_experts - 1)
    n_used = (pend[-1:] // blk).astype(I32)
    xs = moe_dispatch(x, dest_flat, (pstart + counts).astype(I32), (padded - counts).astype(I32),
                      n_tiles * blk)
    ys = moe_experts(xs, tile_e, n_used, w_gu, w_dn)
    return moe_combine_ln(ys, dest_flat, gate.T, x, xb, ws_gu, ws_dn, g, b, alpha)


def kernel(x_prompt, x_sample, state_hgrn, state_pool, cache_k, cache_v, ln_g, ln_b, w_in_even,
           lb_even, gnorm_even, pool_w_even, pool_scale_even, w_out_even, w_qkv_odd, rel_bias_odd,
           w_o_odd, w_router, router_bias, w_gu, w_dn, ws_gu, ws_dn):
    batch, seq, d = x_prompt.shape
    dec_batch, dec_seq, _ = x_sample.shape
    depth = ln_g.shape[0]
    n_p = batch * seq
    n_s = dec_batch * dec_seq
    width = w_in_even.shape[2] // 5
    n_heads_a = width // HEAD_DIM
    n_heads_c = d // HEAD_DIM
    alpha = (2 * depth) ** 0.25
    assert dec_seq >= POOL_PREV and seq >= max(POOL_PREV, WINDOW)

    x = jnp.concatenate([x_prompt.reshape(n_p, d), x_sample.reshape(n_s, d)], axis=0)
    xb = x.astype(BF16)

    p = jax.nn.softmax(lb_even.astype(F32), axis=0)
    lbs = jnp.cumsum(p, axis=0) - p[0]

    hgrn_p, pool_p, k_p, v_p, hgrn_s, pool_s, k_s, v_s = [], [], [], [], [], [], [], []
    for l in range(depth):
        j = l // 2
        if l % 2 == 0:
            proj = matmul(xb, w_in_even[j].astype(BF16))
            zero_state = jnp.zeros((batch, n_heads_a, HEAD_DIM, HEAD_DIM), F32)
            oa_p, sp = hgrn_mixer(proj, 0, batch, seq, zero_state, lbs[j], gnorm_even[j])
            oa_s, ss = hgrn_mixer(proj, n_p, dec_batch, dec_seq,
                                  jnp.swapaxes(state_hgrn[j].astype(F32), -1, -2), lbs[j],
                                  gnorm_even[j])
            pw = pool_w_even[j].astype(BF16)
            ob_p = pool_mixer(proj, 0, batch, seq, 0, jnp.zeros((batch, POOL_CARRY, width), F32),
                              pw, pool_scale_even[j])
            prev_s = jnp.pad(state_pool[j].astype(F32),
                             ((0, 0), (POOL_CARRY - POOL_PREV, 0), (0, 0)))
            ob_s = pool_mixer(proj, n_p, dec_batch, dec_seq, PAST_LEN, prev_s, pw,
                              pool_scale_even[j])
            mixed = jnp.concatenate([jnp.concatenate([oa_p, oa_s], axis=0),
                                     jnp.concatenate([ob_p, ob_s], axis=0)], axis=1)
            w_mix = w_out_even[j]
            hgrn_p.append(jnp.swapaxes(sp, -1, -2))
            hgrn_s.append(jnp.swapaxes(ss, -1, -2))
            pool_p.append(proj[:n_p].reshape(batch, seq, 5 * width)[:, seq - POOL_PREV:, 4 * width:])
            pool_s.append(proj[n_p:].reshape(dec_batch, dec_seq, 5 * width)
                          [:, dec_seq - POOL_PREV:, 4 * width:])
        else:
            qkv = matmul(xb, w_qkv_odd[j].astype(BF16), out_dtype=BF16)
            lc = cache_k.shape[2]
            o_p = band_attention_prompt(qkv, batch, seq, rel_bias_odd[j])
            o_s = band_attention_sample(qkv, n_p, dec_batch, dec_seq,
                                        cache_k[j].reshape(dec_batch, lc, d),
                                        cache_v[j].reshape(dec_batch, lc, d), rel_bias_odd[j])
            mixed = jnp.concatenate([o_p, o_s], axis=0)
            w_mix = w_o_odd[j]
            keep = min(WINDOW, seq)
            tail_p = qkv[:n_p].reshape(batch, seq, 3 * d)[:, seq - keep:].astype(F32)
            new_s = qkv[n_p:].reshape(dec_batch, dec_seq, 3 * d).astype(F32)
            k_p.append(tail_p[:, :, d:2 * d].reshape(batch, keep, n_heads_c, HEAD_DIM))
            v_p.append(tail_p[:, :, 2 * d:].reshape(batch, keep, n_heads_c, HEAD_DIM))
            k_s.append(new_s[:, :, d:2 * d].reshape(dec_batch, dec_seq, n_heads_c, HEAD_DIM))
            v_s.append(new_s[:, :, 2 * d:].reshape(dec_batch, dec_seq, n_heads_c, HEAD_DIM))
        x, xb = matmul_residual_ln(mixed, w_mix.astype(BF16), x, ln_g[l, 0], ln_b[l, 0], alpha)
        x, xb = moe_ffn_ln(x, xb, w_router[l], router_bias[l], w_gu[l], w_dn[l],
                           ws_gu[l].astype(BF16), ws_dn[l].astype(BF16), ln_g[l, 1], ln_b[l, 1],
                           alpha)

    y_p = x[:n_p].reshape(batch, seq, d)
    y_s = x[n_p:].reshape(dec_batch, dec_seq, d)
    return (y_p, y_s, jnp.stack(hgrn_p), jnp.stack(pool_p), jnp.stack(k_p), jnp.stack(v_p),
            jnp.stack(hgrn_s), jnp.stack(pool_s), jnp.stack(k_s), jnp.stack(v_s))
```

```python
import functools

import numpy as np
import jax
import jax.numpy as jnp
from jax import lax
from jax.experimental import pallas as pl
from jax.experimental.pallas import tpu as pltpu

F32 = jnp.float32
BF16 = jnp.bfloat16
I32 = jnp.int32
U32 = jnp.uint32

CHUNK = 64
HEAD_DIM = 128
POOL_WINDOWS = (2, 4, 8, 16)
POOL_PREV = max(POOL_WINDOWS) - 1
POOL_CARRY = POOL_PREV + 1
LEFT_CHUNKS = 8
WINDOW = LEFT_CHUNKS * CHUNK
MAX_REL = 128
N_GROUPS = 8
TOPK_GROUPS = 4
TOP_K = 8
ROUTE_SCALE = 2.5
LN_EPS = 1e-5
RMS_EPS = 1e-6
PAST_LEN = 1024
NEG = -1e30

VMEM_LIMIT_BYTES = 56 * 1024 * 1024
MM_BLOCK_M = 1024
MM_BLOCK_N = 1024
LN_BLOCK_M = 512
HGRN_BLOCK_T = 128
ATTN_BLOCK_Q = 256
ATTN_HEADS_PER_STEP = 4
POOL_BLOCK_T = 512
ROUTER_BLOCK_T = 512
DISPATCH_BLOCK_T = 128
EXPERT_BLOCK = 512
COMBINE_BLOCK_T = 128


def _params(**kw):
    return pltpu.CompilerParams(vmem_limit_bytes=VMEM_LIMIT_BYTES, **kw)


def _sigmoid(x):
    return 1.0 / (1.0 + jnp.exp(-x))


def _silu(x):
    return x * _sigmoid(x)


def _dot(a, b):
    return jnp.dot(a, b, preferred_element_type=F32)


def _dot_nt(a, b):
    return lax.dot_general(a, b, (((1,), (1,)), ((), ())), preferred_element_type=F32)


def _dot_tn(a, b):
    return lax.dot_general(a, b, (((0,), (0,)), ((), ())), preferred_element_type=F32)


def _layer_norm(y, g, b):
    mu = jnp.mean(y, axis=-1, keepdims=True)
    yc = y - mu
    var = jnp.mean(yc * yc, axis=-1, keepdims=True)
    return yc * lax.rsqrt(var + LN_EPS) * g + b


def _mm_kernel(x_ref, w_ref, o_ref):
    o_ref[...] = _dot(x_ref[...], w_ref[...]).astype(o_ref.dtype)


def matmul(x, w, out_dtype=F32, bm=MM_BLOCK_M, bn=MM_BLOCK_N):
    m, k = x.shape
    n = w.shape[1]
    bm, bn = min(bm, m), min(bn, n)
    assert m % bm == 0 and n % bn == 0
    return pl.pallas_call(
        _mm_kernel,
        grid=(m // bm, n // bn),
        in_specs=[pl.BlockSpec((bm, k), lambda i, j: (i, 0)),
                  pl.BlockSpec((k, bn), lambda i, j: (0, j))],
        out_specs=pl.BlockSpec((bm, bn), lambda i, j: (i, j)),
        out_shape=jax.ShapeDtypeStruct((m, n), out_dtype),
        compiler_params=_params(),
        name="matmul",
    )(x, w)


def _pack_pairs(x):
    n = x.shape[1] // 2
    lo = lax.bitcast_convert_type(x[:, :n].astype(jnp.bfloat16).astype(F32), U32)
    hi = lax.bitcast_convert_type(x[:, n:].astype(jnp.bfloat16).astype(F32), U32)
    return (lo >> 16) | hi


def _unpack_pairs(u):
    lo = lax.bitcast_convert_type(u << 16, F32)
    hi = lax.bitcast_convert_type(u & jnp.uint32(0xFFFF0000), F32)
    return lo, hi


def _mm_ln_kernel(alpha, a_ref, w_ref, res_ref, g_ref, b_ref, x_ref, xb_ref, xpk_ref):
    y = alpha * res_ref[...] + _dot(a_ref[...], w_ref[...])
    out = _layer_norm(y, g_ref[...], b_ref[...])
    x_ref[...] = out
    xb_ref[...] = out.astype(BF16)
    xpk_ref[...] = _pack_pairs(out)


def matmul_residual_ln(a, w, res, g, b, alpha, bm=LN_BLOCK_M):
    m, k = a.shape
    d = w.shape[1]
    bm = min(bm, m)
    assert m % bm == 0
    return pl.pallas_call(
        functools.partial(_mm_ln_kernel, alpha),
        grid=(m // bm,),
        in_specs=[pl.BlockSpec((bm, k), lambda i: (i, 0)),
                  pl.BlockSpec((k, d), lambda i: (0, 0)),
                  pl.BlockSpec((bm, d), lambda i: (i, 0)),
                  pl.BlockSpec((1, d), lambda i: (0, 0)),
                  pl.BlockSpec((1, d), lambda i: (0, 0))],
        out_specs=[pl.BlockSpec((bm, d), lambda i: (i, 0)),
                   pl.BlockSpec((bm, d), lambda i: (i, 0)),
                   pl.BlockSpec((bm, d // 2), lambda i: (i, 0))],
        out_shape=[jax.ShapeDtypeStruct((m, d), F32), jax.ShapeDtypeStruct((m, d), BF16),
                   jax.ShapeDtypeStruct((m, d // 2), U32)],
        compiler_params=_params(),
        name="matmul_residual_ln",
    )(a, w, res, g.reshape(1, d), b.reshape(1, d))


def _hgrn_tables(c):
    t = np.arange(c)
    levels = []
    b = c // 2
    while b >= 1:
        levels.append(b)
        b //= 2
    sel = [(t[None, :] <= t[:, None]).astype(np.float32)]
    lvl = np.where(np.eye(c, dtype=bool), 0, -1).astype(np.int32)
    for l, b in enumerate(levels):
        blk = t // b
        odd = (blk % 2) == 1
        ref_row = np.where(odd, blk * b - 1, blk * b + b - 1)
        sel.append((t[None, :] <= ref_row[:, None]).astype(np.float32))
        same = (t[:, None] // (2 * b)) == (t[None, :] // (2 * b))
        lvl = np.where(same & odd[:, None] & (~odd)[None, :], l + 1, lvl)
    sel = np.concatenate(sel, axis=0)
    return np.concatenate([sel, sel, sel], axis=1), lvl


def _hgrn_kernel(n_heads, c, q_ref, f_ref, i_ref, g_ref, s0_ref, lb_ref, gn_ref, sel_ref, lvl_ref,
                 o_ref, sfin_ref, state_ref, cums_ref, qw_ref, kw_ref):
    j = pl.program_id(1)

    @pl.when(j == 0)
    def _():
        state_ref[...] = s0_ref[0]

    n_lvl = qw_ref.shape[0]
    lb = lb_ref[...]
    f = lb + (1.0 - lb) * _sigmoid(f_ref[...])
    k = 1.0 - f
    q = _silu(q_ref[...])
    lg = jnp.log(f)
    hi = lg.astype(BF16)
    rest = lg - hi.astype(F32)
    mid = rest.astype(BF16)
    lo = (rest - mid.astype(F32)).astype(BF16)
    cums_ref[...] = _dot(sel_ref[...], jnp.concatenate([hi, mid, lo], axis=0))
    cum = cums_ref[0:c]
    qw_ref[0] = q.astype(BF16)
    kw_ref[0] = k.astype(BF16)
    for l in range(1, n_lvl):
        w = jnp.exp(-jnp.abs(cum - cums_ref[l * c:(l + 1) * c]))
        qw_ref[l] = (q * w).astype(BF16)
        kw_ref[l] = (k * w).astype(BF16)
    last = cum[c - 1:c, :]
    q_in = (q * jnp.exp(cum)).astype(BF16)
    k_end = (k * jnp.exp(last - cum)).astype(BF16)
    decay = jnp.exp(last)
    lvl = lvl_ref[...]
    for h in range(n_heads):
        hs = slice(h * HEAD_DIM, (h + 1) * HEAD_DIM)
        st = state_ref[h]
        o = _dot_nt(q_in[:, hs], st.astype(BF16))
        scores = jnp.zeros((c, c), F32)
        for l in range(n_lvl):
            scores = jnp.where(lvl == l, _dot_nt(qw_ref[l, :, hs], kw_ref[l, :, hs]), scores)
        v = i_ref[:, hs].astype(BF16)
        o = o + _dot(scores.astype(BF16), v)
        state_ref[h] = st * decay[:, hs] + _dot_tn(v, k_end[:, hs])
        o = o * lax.rsqrt(jnp.mean(o * o, axis=-1, keepdims=True) + RMS_EPS) * gn_ref[...]
        o_ref[:, hs] = (o * _silu(g_ref[:, hs])).astype(o_ref.dtype)

    @pl.when(j == pl.num_programs(1) - 1)
    def _():
        sfin_ref[0] = state_ref[...]


def hgrn_mixer(proj, row0, batch, seq, s0_t, lb, gnorm):
    width = proj.shape[1] // 5
    n_heads = width // HEAD_DIM
    c = HGRN_BLOCK_T if seq % HGRN_BLOCK_T == 0 else seq
    n = seq // c
    assert row0 % c == 0 and c & (c - 1) == 0
    blk0 = row0 // c
    sel, lvl = _hgrn_tables(c)
    n_lvl = sel.shape[0] // c

    def sec(s):
        return pl.BlockSpec((c, width), lambda b, j: (blk0 + b * n + j, s))

    state_spec = pl.BlockSpec((1, n_heads, HEAD_DIM, HEAD_DIM), lambda b, j: (b, 0, 0, 0))
    return pl.pallas_call(
        functools.partial(_hgrn_kernel, n_heads, c),
        grid=(batch, n),
        in_specs=[sec(0), sec(1), sec(2), sec(3), state_spec,
                  pl.BlockSpec((1, width), lambda b, j: (0, 0)),
                  pl.BlockSpec((1, HEAD_DIM), lambda b, j: (0, 0)),
                  pl.BlockSpec(sel.shape, lambda b, j: (0, 0)),
                  pl.BlockSpec(lvl.shape, lambda b, j: (0, 0))],
        out_specs=[pl.BlockSpec((c, width), lambda b, j: (b * n + j, 0)), state_spec],
        out_shape=[jax.ShapeDtypeStruct((batch * seq, width), BF16),
                   jax.ShapeDtypeStruct((batch, n_heads, HEAD_DIM, HEAD_DIM), F32)],
        scratch_shapes=[pltpu.VMEM((n_heads, HEAD_DIM, HEAD_DIM), F32),
                        pltpu.VMEM((n_lvl * c, width), F32),
                        pltpu.VMEM((n_lvl, c, width), BF16),
                        pltpu.VMEM((n_lvl, c, width), BF16)],
        compiler_params=_params(),
        name="hgrn_mixer",
    )(proj, proj, proj, proj, s0_t, lb.reshape(1, width), gnorm.reshape(1, HEAD_DIM),
      jnp.asarray(sel).astype(BF16), jnp.asarray(lvl))


def _pool_kernel(pos0, bt, group, u_ref, prev_ref, w_ref, scale_ref, o_ref, ext_ref):
    j = pl.program_id(1)

    @pl.when(j == 0)
    def _():
        ext_ref[0:POOL_CARRY] = prev_ref[0]

    u = u_ref[...]
    ext_ref[POOL_CARRY:POOL_CARRY + bt] = u
    pos = pos0 + j * bt + lax.broadcasted_iota(I32, (bt, 1), 0)
    for gi, win in enumerate(POOL_WINDOWS):
        cs = slice(gi * group, (gi + 1) * group)
        s = ext_ref[:, cs]
        step = 1
        while step < win:
            s = s + pltpu.roll(s, step, axis=0)
            step *= 2
        cnt = jnp.minimum(pos + 1, win).astype(F32)
        d = s[POOL_CARRY:] / cnt - u[:, cs]
        y = _dot(d.astype(BF16), w_ref[gi]) * scale_ref[:, cs]
        o_ref[:, cs] = y.astype(o_ref.dtype)
    ext_ref[0:POOL_CARRY] = ext_ref[bt:bt + POOL_CARRY]


def pool_mixer(proj, row0, batch, seq, pos0, prev, pool_w, pool_scale):
    width = proj.shape[1] // 5
    group = width // len(POOL_WINDOWS)
    bt = min(POOL_BLOCK_T, seq)
    n = seq // bt
    assert seq % bt == 0 and row0 % bt == 0 and sum(POOL_WINDOWS) // 2 <= POOL_CARRY
    blk0 = row0 // bt
    return pl.pallas_call(
        functools.partial(_pool_kernel, pos0, bt, group),
        grid=(batch, n),
        in_specs=[pl.BlockSpec((bt, width), lambda b, j: (blk0 + b * n + j, 4)),
                  pl.BlockSpec((1, POOL_CARRY, width), lambda b, j: (b, 0, 0)),
                  pl.BlockSpec(pool_w.shape, lambda b, j: (0, 0, 0)),
                  pl.BlockSpec((1, width), lambda b, j: (0, 0))],
        out_specs=pl.BlockSpec((bt, width), lambda b, j: (b * n + j, 0)),
        out_shape=jax.ShapeDtypeStruct((batch * seq, width), BF16),
        scratch_shapes=[pltpu.VMEM((bt + POOL_CARRY, width), F32)],
        compiler_params=_params(),
        name="pool_mixer",
    )(proj, prev, pool_w, pool_scale.reshape(1, width))


def _rel_bias(table, dist):
    idx = jnp.clip(dist, -(CHUNK - 1), MAX_REL) + (CHUNK - 1)
    return table[:, idx].astype(F32)


def _attn_prompt_kernel(bq, hps, scale, q_ref, k0_ref, k1_ref, k2_ref, v0_ref, v1_ref, v2_ref,
                        vals_ref, o_ref, bias_ref):
    b = pl.program_id(1)
    i = pl.program_id(2)

    @pl.when(jnp.logical_and(b == 0, i == 0))
    def _():
        span = vals_ref.shape[-1]
        qi = lax.broadcasted_iota(I32, (bq, 3 * bq), 0)
        kj = lax.broadcasted_iota(I32, (bq, 3 * bq), 1)
        lo = (qi // CHUNK) * CHUNK
        in_band = jnp.logical_and(kj >= lo, kj < lo + WINDOW + CHUNK)
        for hh in range(hps):
            rows = jnp.broadcast_to(vals_ref[0, hh:hh + 1, :], (bq, span))
            toep = pltpu.roll(rows, span - (bq - 1), 1, stride=1, stride_axis=0)
            bias_ref[hh] = jnp.where(in_band, toep[:, :3 * bq], NEG)

    ks = (k0_ref, k1_ref, k2_ref)
    vs = (v0_ref, v1_ref, v2_ref)
    for hh in range(hps):
        hs = slice(hh * HEAD_DIM, (hh + 1) * HEAD_DIM)
        q = q_ref[:, hs]
        s = []
        for p in range(3):
            sp = _dot_nt(q, ks[p][:, hs]) * scale + bias_ref[hh, :, p * bq:(p + 1) * bq]
            if p < 2:
                sp = jnp.where(i >= 2 - p, sp, NEG)
            s.append(sp)
        m = jnp.maximum(jnp.maximum(jnp.max(s[0], axis=-1, keepdims=True),
                                    jnp.max(s[1], axis=-1, keepdims=True)),
                        jnp.max(s[2], axis=-1, keepdims=True))
        e = [jnp.exp(sp - m) for sp in s]
        inv = 1.0 / (jnp.sum(e[0], axis=-1, keepdims=True) + jnp.sum(e[1], axis=-1, keepdims=True)
                     + jnp.sum(e[2], axis=-1, keepdims=True))
        o = _dot((e[0] * inv).astype(BF16), vs[0][:, hs])
        o = o + _dot((e[1] * inv).astype(BF16), vs[1][:, hs])
        o = o + _dot((e[2] * inv).astype(BF16), vs[2][:, hs])
        o_ref[:, hs] = o.astype(o_ref.dtype)


def band_attention_prompt(qkv, batch, seq, rel_bias):
    d = qkv.shape[1] // 3
    n_heads = d // HEAD_DIM
    bq = ATTN_BLOCK_Q
    hps = min(ATTN_HEADS_PER_STEP, n_heads)
    assert seq % bq == 0 and WINDOW == 2 * bq and bq % CHUNK == 0 and n_heads % hps == 0
    nq = seq // bq
    ng = n_heads // hps
    offs = jnp.arange(4 * bq) - (bq - 1)
    vals = _rel_bias(rel_bias, WINDOW - offs).reshape(ng, hps, 4 * bq)

    def kv(sec, p):
        return pl.BlockSpec((bq, hps * HEAD_DIM),
                            lambda g, b, i: (b * nq + jnp.maximum(i - 2 + p, 0), sec * ng + g))

    return pl.pallas_call(
        functools.partial(_attn_prompt_kernel, bq, hps, HEAD_DIM ** -0.5),
        grid=(ng, batch, nq),
        in_specs=[pl.BlockSpec((bq, hps * HEAD_DIM), lambda g, b, i: (b * nq + i, g)),
                  kv(1, 0), kv(1, 1), kv(1, 2), kv(2, 0), kv(2, 1), kv(2, 2),
                  pl.BlockSpec((1, hps, 4 * bq), lambda g, b, i: (g, 0, 0))],
        out_specs=pl.BlockSpec((bq, hps * HEAD_DIM), lambda g, b, i: (b * nq + i, g)),
        out_shape=jax.ShapeDtypeStruct((batch * seq, d), BF16),
        scratch_shapes=[pltpu.VMEM((hps, bq, 3 * bq), F32)],
        compiler_params=_params(),
        name="band_attention_prompt",
    )(qkv, qkv, qkv, qkv, qkv, qkv, qkv, vals)


def _attn_sample_kernel(n_heads, lc, scale, qkv_ref, ck_ref, cv_ref, bias_ref, o_ref):
    d = n_heads * HEAD_DIM
    for h in range(n_heads):
        hs = slice(h * HEAD_DIM, (h + 1) * HEAD_DIM)
        q = qkv_ref[:, h * HEAD_DIM:(h + 1) * HEAD_DIM].astype(BF16)
        kn = qkv_ref[:, d + h * HEAD_DIM:d + (h + 1) * HEAD_DIM].astype(BF16)
        vn = qkv_ref[:, 2 * d + h * HEAD_DIM:2 * d + (h + 1) * HEAD_DIM].astype(BF16)
        kc = ck_ref[0, 0, :, h, :].astype(BF16)
        vc = cv_ref[0, 0, :, h, :].astype(BF16)
        sc = _dot_nt(q, kc) * scale + bias_ref[h, :, 0:lc]
        sn = _dot_nt(q, kn) * scale + bias_ref[h, :, lc:]
        m = jnp.maximum(jnp.max(sc, axis=-1, keepdims=True), jnp.max(sn, axis=-1, keepdims=True))
        ec = jnp.exp(sc - m)
        en = jnp.exp(sn - m)
        inv = 1.0 / (jnp.sum(ec, axis=-1, keepdims=True) + jnp.sum(en, axis=-1, keepdims=True))
        o = _dot((ec * inv).astype(BF16), vc) + _dot((en * inv).astype(BF16), vn)
        o_ref[:, hs] = o.astype(o_ref.dtype)


def band_attention_sample(qkv, row0, batch, seq, cache_k, cache_v, layer, rel_bias):
    d = qkv.shape[1] // 3
    n_heads = d // HEAD_DIM
    lc = cache_k.shape[2]
    assert row0 % seq == 0 and lc % 128 == 0
    blk0 = row0 // seq
    dist = jnp.arange(seq)[:, None] + lc - jnp.arange(lc + seq)[None, :]
    bias = _rel_bias(rel_bias, dist)
    cache_spec = pl.BlockSpec((1, 1, lc, n_heads, HEAD_DIM), lambda b: (layer, b, 0, 0, 0))
    return pl.pallas_call(
        functools.partial(_attn_sample_kernel, n_heads, lc, HEAD_DIM ** -0.5),
        grid=(batch,),
        in_specs=[pl.BlockSpec((seq, 3 * d), lambda b: (blk0 + b, 0)),
                  cache_spec, cache_spec,
                  pl.BlockSpec(bias.shape, lambda b: (0, 0, 0))],
        out_specs=pl.BlockSpec((seq, d), lambda b: (b, 0)),
        out_shape=jax.ShapeDtypeStruct((batch * seq, d), BF16),
        compiler_params=_params(),
        name="band_attention_sample",
    )(qkv, cache_k, cache_v, bias)


def _first_max(vals, idx, sentinel):
    m = jnp.max(vals, axis=0, keepdims=True)
    first = jnp.min(jnp.where(vals == m, idx, sentinel), axis=0, keepdims=True)
    return m, first


def _stack_rows(rows, lanes):
    ridx = lax.broadcasted_iota(I32, (len(rows), lanes), 0)
    out = jnp.zeros((len(rows), lanes), rows[0].dtype)
    for r, row in enumerate(rows):
        out = jnp.where(ridx == r, row, out)
    return out


def _router_kernel(n_experts, bt, x_ref, wt_ref, bias_ref, e_ref, gate_ref, rank_ref, cnt_ref,
                   carry_ref):
    i = pl.program_id(0)

    @pl.when(i == 0)
    def _():
        carry_ref[...] = jnp.zeros_like(carry_ref)

    per = n_experts // N_GROUPS
    logits = lax.dot_general(wt_ref[...], x_ref[...], (((1,), (1,)), ((), ())),
                             precision=lax.Precision.HIGHEST, preferred_element_type=F32)
    scores = _sigmoid(logits)
    sel = scores + bias_ref[:, 0:1]

    sub = lax.broadcasted_iota(I32, (per, bt), 0)
    grp_rows = []
    for g in range(N_GROUPS):
        blk = sel[g * per:(g + 1) * per]
        m1, i1 = _first_max(blk, sub, per)
        m2 = jnp.max(jnp.where(sub == i1, -jnp.inf, blk), axis=0, keepdims=True)
        grp_rows.append(m1 + m2)
    grp = _stack_rows(grp_rows, bt)
    gidx = lax.broadcasted_iota(I32, (N_GROUPS, bt), 0)
    keep = jnp.zeros((N_GROUPS, bt), F32)
    for _ in range(TOPK_GROUPS):
        _, first = _first_max(grp, gidx, N_GROUPS)
        hit = gidx == first
        keep = jnp.where(hit, 1.0, keep)
        grp = jnp.where(hit, -jnp.inf, grp)
    masked = jnp.concatenate(
        [jnp.where(keep[g:g + 1] > 0.0, sel[g * per:(g + 1) * per], -jnp.inf)
         for g in range(N_GROUPS)], axis=0)

    eidx = lax.broadcasted_iota(I32, (n_experts, bt), 0)
    chosen = jnp.zeros((n_experts, bt), F32)
    picks = []
    gates = []
    for _ in range(TOP_K):
        _, first = _first_max(masked, eidx, n_experts)
        hit = eidx == first
        picks.append(first)
        gates.append(jnp.sum(jnp.where(hit, scores, 0.0), axis=0, keepdims=True))
        chosen = jnp.where(hit, 1.0, chosen)
        masked = jnp.where(hit, -jnp.inf, masked)
    gate = _stack_rows(gates, bt)
    gate = gate / jnp.sum(gate, axis=0, keepdims=True) * ROUTE_SCALE

    earlier = (lax.broadcasted_iota(I32, (bt, bt), 0) < lax.broadcasted_iota(I32, (bt, bt), 1))
    before = _dot(chosen.astype(BF16), jnp.where(earlier, 1.0, 0.0).astype(BF16)) + carry_ref[:, 0:1]
    ranks = [jnp.sum(jnp.where(eidx == p, before, 0.0), axis=0, keepdims=True) for p in picks]

    e_ref[...] = _stack_rows(picks, bt)
    gate_ref[...] = gate
    rank_ref[...] = _stack_rows(ranks, bt).astype(I32)
    carry_ref[...] = carry_ref[...] + jnp.sum(chosen, axis=1, keepdims=True)
    cnt_ref[...] = carry_ref[...]


def moe_router(x, w_router, router_bias, bt=ROUTER_BLOCK_T):
    n, d = x.shape
    n_experts = w_router.shape[1]
    bt = min(bt, n)
    assert n % bt == 0 and n_experts % N_GROUPS == 0
    lanes = 128
    tok = pl.BlockSpec((TOP_K, bt), lambda i: (0, i))
    e, gate, rank, cnt = pl.pallas_call(
        functools.partial(_router_kernel, n_experts, bt),
        grid=(n // bt,),
        in_specs=[pl.BlockSpec((bt, d), lambda i: (i, 0)),
                  pl.BlockSpec((n_experts, d), lambda i: (0, 0)),
                  pl.BlockSpec((n_experts, lanes), lambda i: (0, 0))],
        out_specs=[tok, tok, tok, pl.BlockSpec((n_experts, lanes), lambda i: (0, 0))],
        out_shape=[jax.ShapeDtypeStruct((TOP_K, n), I32), jax.ShapeDtypeStruct((TOP_K, n), F32),
                   jax.ShapeDtypeStruct((TOP_K, n), I32),
                   jax.ShapeDtypeStruct((n_experts, lanes), F32)],
        scratch_shapes=[pltpu.VMEM((n_experts, lanes), F32)],
        compiler_params=_params(),
        name="moe_router",
    )(x, w_router.T, jnp.broadcast_to(router_bias.astype(F32)[:, None], (n_experts, lanes)))
    return e, gate, rank, cnt[:, 0].astype(I32)


def _dispatch_kernel(bt, n_experts, fill_start_ref, fill_len_ref, dest_ref, x_ref, xs_ref,
                     zero_ref, sem, fill_sem):
    i = pl.program_id(0)

    @pl.when(i == 0)
    def _():
        zero_ref[...] = jnp.zeros_like(zero_ref)

        def fill(e, carry):
            base = fill_start_ref[e]

            def zero_copy(r):
                return pltpu.make_async_copy(zero_ref.at[pl.ds(0, 1)],
                                             xs_ref.at[pl.ds(base + r, 1)], fill_sem)

            def start(r, c):
                zero_copy(r).start()
                return c

            def wait(r, c):
                zero_copy(r).wait()
                return c

            lax.fori_loop(0, fill_len_ref[e], start, 0)
            lax.fori_loop(0, fill_len_ref[e], wait, 0)
            return carry

        lax.fori_loop(0, n_experts, fill, 0)

    def row_copy(t, k):
        return pltpu.make_async_copy(x_ref.at[pl.ds(t, 1)],
                                     xs_ref.at[pl.ds(dest_ref[t * TOP_K + k], 1)], sem)

    def drain(t, carry):
        for k in range(TOP_K):
            row_copy(t, k).wait()
        return carry

    for t in range(bt):
        for k in range(TOP_K):
            row_copy(t, k).start(priority=k % 2)
    lax.fori_loop(0, bt, drain, 0)


def moe_dispatch(x, dest_flat, fill_start, fill_len, rows, bt=DISPATCH_BLOCK_T):
    n, d = x.shape
    bt = min(bt, n)
    assert n % bt == 0
    n_experts = fill_start.shape[0]
    return pl.pallas_call(
        functools.partial(_dispatch_kernel, bt, n_experts),
        grid_spec=pltpu.PrefetchScalarGridSpec(
            num_scalar_prefetch=2,
            grid=(n // bt,),
            in_specs=[pl.BlockSpec((bt * TOP_K,), lambda i, *_: (i,), memory_space=pltpu.SMEM),
                      pl.BlockSpec((bt, d), lambda i, *_: (i, 0))],
            out_specs=pl.BlockSpec(memory_space=pl.ANY),
            scratch_shapes=[pltpu.VMEM((8, d), x.dtype),
                            pltpu.SemaphoreType.DMA, pltpu.SemaphoreType.DMA]),
        out_shape=jax.ShapeDtypeStruct((rows, d), x.dtype),
        compiler_params=_params(),
        name="moe_dispatch",
    )(fill_start, fill_len, dest_flat, x)


def _expert_kernel(tile_e_ref, n_used_ref, xs_ref, wgu_ref, wdn_ref, ys_ref, wgu_b, wdn_b):
    i = pl.program_id(0)
    de = wdn_b.shape[0]
    changed = jnp.logical_or(i == 0, tile_e_ref[i] != tile_e_ref[jnp.maximum(i - 1, 0)])

    @pl.when(changed)
    def _():
        wgu_b[...] = wgu_ref[0, 0].astype(BF16)
        wdn_b[...] = wdn_ref[0, 0].astype(BF16)

    @pl.when(i < n_used_ref[0])
    def _():
        lo, hi = _unpack_pairs(xs_ref[...])
        x = jnp.concatenate([lo.astype(BF16), hi.astype(BF16)], axis=1)
        gu = _dot(x, wgu_b[...])
        a = _silu(gu[:, :de]) * gu[:, de:]
        ys_ref[...] = _pack_pairs(_dot(a.astype(BF16), wdn_b[...]))

    @pl.when(i >= n_used_ref[0])
    def _():
        ys_ref[...] = jnp.zeros_like(ys_ref)


def moe_experts(xs, tile_e, n_used, w_gu, w_dn, layer, blk=EXPERT_BLOCK):
    rows, half = xs.shape
    d = 2 * half
    n_tiles = rows // blk
    de = w_dn.shape[2]
    return pl.pallas_call(
        _expert_kernel,
        grid_spec=pltpu.PrefetchScalarGridSpec(
            num_scalar_prefetch=2,
            grid=(n_tiles,),
            in_specs=[pl.BlockSpec((blk, half), lambda i, te, nu: (jnp.minimum(i, nu[0] - 1), 0)),
                      pl.BlockSpec((1, 1, d, 2 * de), lambda i, te, nu: (layer, te[i], 0, 0)),
                      pl.BlockSpec((1, 1, de, d), lambda i, te, nu: (layer, te[i], 0, 0))],
            out_specs=pl.BlockSpec((blk, half), lambda i, te, nu: (i, 0)),
            scratch_shapes=[pltpu.VMEM((d, 2 * de), BF16), pltpu.VMEM((de, d), BF16)]),
        out_shape=jax.ShapeDtypeStruct((rows, half), U32),
        compiler_params=_params(),
        name="moe_experts",
    )(tile_e, n_used, xs, w_gu, w_dn)


def _combine_kernel(alpha, bt, dest_ref, dest_next_ref, ys_ref, gate_ref, x_ref, xb_ref, wsgu_ref,
                    wsdn_ref, g_ref, b_ref, xo_ref, xbo_ref, buf_a, buf_b, sem_a, sem_b):
    i = pl.program_id(0)
    de = wsdn_ref.shape[0]
    per_tile = bt * TOP_K

    def row_copy(idx_ref, base, buf, sem, t, k):
        return pltpu.make_async_copy(ys_ref.at[pl.ds(idx_ref[base + t * TOP_K + k], 1)],
                                     buf.at[k, pl.ds(t, 1)], sem)

    def issue_unrolled(idx_ref, base, buf, sem):
        for t in range(bt):
            for k in range(TOP_K):
                row_copy(idx_ref, base, buf, sem, t, k).start(priority=k % 2)

    def drain(idx_ref, base, buf, sem):
        def body(t, carry):
            for k in range(TOP_K):
                row_copy(idx_ref, base, buf, sem, t, k).wait()
            return carry
        lax.fori_loop(0, bt, body, 0)

    def compute(r0, buf):
        rows = pl.ds(r0, bt)
        sgu = _dot(xb_ref[rows, :], wsgu_ref[...])
        shared = _dot((_silu(sgu[:, :de]) * sgu[:, de:]).astype(BF16), wsdn_ref[...])
        routed_lo, routed_hi = None, None
        for k in range(TOP_K):
            gate = gate_ref[rows, k:k + 1]
            lo, hi = _unpack_pairs(buf[k])
            routed_lo = gate * lo if k == 0 else routed_lo + gate * lo
            routed_hi = gate * hi if k == 0 else routed_hi + gate * hi
        routed = jnp.concatenate([routed_lo, routed_hi], axis=1)
        out = _layer_norm(alpha * x_ref[rows, :] + (routed + shared), g_ref[...], b_ref[...])
        xo_ref[rows, :] = out
        xbo_ref[rows, :] = out.astype(BF16)

    @pl.when(i == 0)
    def _():
        def body(t, carry):
            for k in range(TOP_K):
                row_copy(dest_ref, 0, buf_a, sem_a, t, k).start()
            return carry
        lax.fori_loop(0, bt, body, 0)

    drain(dest_ref, 0, buf_a, sem_a)
    issue_unrolled(dest_ref, per_tile, buf_b, sem_b)
    compute(0, buf_a)
    drain(dest_ref, per_tile, buf_b, sem_b)
    issue_unrolled(dest_next_ref, 0, buf_a, sem_a)
    compute(bt, buf_b)

    @pl.when(i == pl.num_programs(0) - 1)
    def _():
        drain(dest_next_ref, 0, buf_a, sem_a)


def moe_combine_ln(ys, dest_flat, gate_tm, x, xb, ws_gu, ws_dn, g, b, alpha, bt=COMBINE_BLOCK_T):
    n, d = x.shape
    bt = min(bt, n // 2)
    assert n % (2 * bt) == 0
    n_tiles = n // bt
    de = ws_dn.shape[0]
    row = pl.BlockSpec((2 * bt, d), lambda i: (i, 0))
    vec = pl.BlockSpec((1, d), lambda i: (0, 0))
    return pl.pallas_call(
        functools.partial(_combine_kernel, alpha, bt),
        grid=(n_tiles // 2,),
        in_specs=[pl.BlockSpec((2 * bt * TOP_K,), lambda i: (i,), memory_space=pltpu.SMEM),
                  pl.BlockSpec((bt * TOP_K,), lambda i: (jnp.minimum(2 * i + 2, n_tiles - 1),),
                               memory_space=pltpu.SMEM),
                  pl.BlockSpec(memory_space=pl.ANY),
                  pl.BlockSpec((2 * bt, TOP_K), lambda i: (i, 0)),
                  row, row,
                  pl.BlockSpec((d, 2 * de), lambda i: (0, 0)),
                  pl.BlockSpec((de, d), lambda i: (0, 0)),
                  vec, vec],
        out_specs=[row, row],
        out_shape=[jax.ShapeDtypeStruct((n, d), F32), jax.ShapeDtypeStruct((n, d), BF16)],
        scratch_shapes=[pltpu.VMEM((TOP_K, bt, d // 2), U32), pltpu.VMEM((TOP_K, bt, d // 2), U32),
                        pltpu.SemaphoreType.DMA, pltpu.SemaphoreType.DMA],
        compiler_params=_params(),
        name="moe_combine_ln",
    )(dest_flat, dest_flat, ys, gate_tm, x, xb, ws_gu, ws_dn, g.reshape(1, d), b.reshape(1, d))


def moe_ffn_ln(x, xb, xpk, w_router, router_bias, w_gu, w_dn, layer, ws_gu, ws_dn, g, b, alpha):
    n, d = x.shape
    n_experts = w_router.shape[1]
    blk = EXPERT_BLOCK
    top_e, gate, rank, counts = moe_router(x, w_router, router_bias)
    padded = (counts + blk - 1) // blk * blk
    pend = jnp.cumsum(padded)
    pstart = pend - padded
    hot = top_e[:, :, None] == jnp.arange(n_experts, dtype=I32)
    dest = jnp.sum(jnp.where(hot, pstart, 0), axis=-1) + rank
    dest_flat = dest.T.reshape(-1)
    n_tiles = (n * TOP_K + n_experts * (blk - 1) + blk - 1) // blk
    tile_row0 = jnp.arange(n_tiles, dtype=I32) * blk
    tile_e = jnp.minimum(jnp.sum((pend[None, :] <= tile_row0[:, None]).astype(I32), axis=1),
                         n_experts - 1)
    n_used = (pend[-1:] // blk).astype(I32)
    xs = moe_dispatch(xpk, dest_flat, (pstart + counts).astype(I32), (padded - counts).astype(I32),
                      n_tiles * blk)
    ys = moe_experts(xs, tile_e, n_used, w_gu, w_dn, layer, blk)
    return moe_combine_ln(ys, dest_flat, gate.T, x, xb, ws_gu, ws_dn, g, b, alpha)


def kernel(x_prompt, x_sample, state_hgrn, state_pool, cache_k, cache_v, ln_g, ln_b, w_in_even,
           lb_even, gnorm_even, pool_w_even, pool_scale_even, w_out_even, w_qkv_odd, rel_bias_odd,
           w_o_odd, w_router, router_bias, w_gu, w_dn, ws_gu, ws_dn):
    batch, seq, d = x_prompt.shape
    dec_batch, dec_seq, _ = x_sample.shape
    depth = ln_g.shape[0]
    n_p = batch * seq
    n_s = dec_batch * dec_seq
    width = w_in_even.shape[2] // 5
    n_heads_a = width // HEAD_DIM
    n_heads_c = d // HEAD_DIM
    alpha = (2 * depth) ** 0.25
    assert dec_seq >= POOL_PREV and seq >= max(POOL_PREV, WINDOW)

    x = jnp.concatenate([x_prompt.reshape(n_p, d), x_sample.reshape(n_s, d)], axis=0)
    xb = x.astype(BF16)

    p = jax.nn.softmax(lb_even.astype(F32), axis=0)
    lbs = jnp.cumsum(p, axis=0) - p[0]

    hgrn_p, pool_p, k_p, v_p, hgrn_s, pool_s, k_s, v_s = [], [], [], [], [], [], [], []
    for l in range(depth):
        j = l // 2
        if l % 2 == 0:
            proj = matmul(xb, w_in_even[j].astype(BF16))
            zero_state = jnp.zeros((batch, n_heads_a, HEAD_DIM, HEAD_DIM), F32)
            oa_p, sp = hgrn_mixer(proj, 0, batch, seq, zero_state, lbs[j], gnorm_even[j])
            oa_s, ss = hgrn_mixer(proj, n_p, dec_batch, dec_seq,
                                  jnp.swapaxes(state_hgrn[j].astype(F32), -1, -2), lbs[j],
                                  gnorm_even[j])
            pw = pool_w_even[j].astype(BF16)
            ob_p = pool_mixer(proj, 0, batch, seq, 0, jnp.zeros((batch, POOL_CARRY, width), F32),
                              pw, pool_scale_even[j])
            prev_s = jnp.pad(state_pool[j].astype(F32),
                             ((0, 0), (POOL_CARRY - POOL_PREV, 0), (0, 0)))
            ob_s = pool_mixer(proj, n_p, dec_batch, dec_seq, PAST_LEN, prev_s, pw,
                              pool_scale_even[j])
            mixed = jnp.concatenate([jnp.concatenate([oa_p, oa_s], axis=0),
                                     jnp.concatenate([ob_p, ob_s], axis=0)], axis=1)
            w_mix = w_out_even[j]
            hgrn_p.append(jnp.swapaxes(sp, -1, -2))
            hgrn_s.append(jnp.swapaxes(ss, -1, -2))
            pool_p.append(jnp.stack([proj[(b + 1) * seq - POOL_PREV:(b + 1) * seq, 4 * width:]
                                     for b in range(batch)]))
            pool_s.append(proj[n_p:].reshape(dec_batch, dec_seq, 5 * width)
                          [:, dec_seq - POOL_PREV:, 4 * width:])
        else:
            qkv = matmul(xb, w_qkv_odd[j].astype(BF16), out_dtype=BF16)
            o_p = band_attention_prompt(qkv, batch, seq, rel_bias_odd[j])
            o_s = band_attention_sample(qkv, n_p, dec_batch, dec_seq, cache_k, cache_v, j,
                                        rel_bias_odd[j])
            mixed = jnp.concatenate([o_p, o_s], axis=0)
            w_mix = w_o_odd[j]
            keep = min(WINDOW, seq)
            tail_p = jnp.stack([qkv[(b + 1) * seq - keep:(b + 1) * seq]
                                for b in range(batch)]).astype(F32)
            new_s = qkv[n_p:].reshape(dec_batch, dec_seq, 3 * d).astype(F32)
            k_p.append(tail_p[:, :, d:2 * d].reshape(batch, keep, n_heads_c, HEAD_DIM))
            v_p.append(tail_p[:, :, 2 * d:].reshape(batch, keep, n_heads_c, HEAD_DIM))
            k_s.append(new_s[:, :, d:2 * d].reshape(dec_batch, dec_seq, n_heads_c, HEAD_DIM))
            v_s.append(new_s[:, :, 2 * d:].reshape(dec_batch, dec_seq, n_heads_c, HEAD_DIM))
        x, xb, xpk = matmul_residual_ln(mixed, w_mix.astype(BF16), x, ln_g[l, 0], ln_b[l, 0], alpha)
        x, xb = moe_ffn_ln(x, xb, xpk, w_router[l], router_bias[l], w_gu, w_dn, l,
                           ws_gu[l].astype(BF16), ws_dn[l].astype(BF16), ln_g[l, 1], ln_b[l, 1],
                           alpha)

    y_p = x[:n_p].reshape(batch, seq, d)
    y_s = x[n_p:].reshape(dec_batch, dec_seq, d)
    return (y_p, y_s, jnp.stack(hgrn_p), jnp.stack(pool_p), jnp.stack(k_p), jnp.stack(v_p),
            jnp.stack(hgrn_s), jnp.stack(pool_s), jnp.stack(k_s), jnp.stack(v_s))
```

```python
import functools

import numpy as np
import jax
import jax.numpy as jnp
from jax import lax
from jax.experimental import pallas as pl
from jax.experimental.pallas import tpu as pltpu

F32 = jnp.float32
BF16 = jnp.bfloat16
I32 = jnp.int32
U32 = jnp.uint32

CHUNK = 64
HEAD_DIM = 128
LANES = 128
POOL_WINDOWS = (2, 4, 8, 16)
POOL_PREV = max(POOL_WINDOWS) - 1
POOL_CARRY = POOL_PREV + 1
LEFT_CHUNKS = 8
WINDOW = LEFT_CHUNKS * CHUNK
MAX_REL = 128
N_GROUPS = 8
TOPK_GROUPS = 4
TOP_K = 8
ROUTE_SCALE = 2.5
LN_EPS = 1e-5
RMS_EPS = 1e-6
PAST_LEN = 1024
NEG = -1e30

VMEM_LIMIT_BYTES = 56 * 1024 * 1024
MM_BLOCK_M = 1024
MM_BLOCK_N = 1024
LN_BLOCK_M = 512
HGRN_BLOCK_T = 128
ATTN_BLOCK_Q = 256
ATTN_HEADS_PER_STEP = 4
POOL_BLOCK_T = 512
ROUTER_BLOCK_T = 512
DISPATCH_BLOCK_T = 128
EXPERT_BLOCK = 512
COMBINE_BLOCK_T = 128


def _params(**kw):
    return pltpu.CompilerParams(vmem_limit_bytes=VMEM_LIMIT_BYTES, **kw)


def _sigmoid(x):
    return 1.0 / (1.0 + jnp.exp(-x))


def _silu(x):
    return x * _sigmoid(x)


def _dot(a, b):
    return jnp.dot(a, b, preferred_element_type=F32)


def _dot_nt(a, b):
    return lax.dot_general(a, b, (((1,), (1,)), ((), ())), preferred_element_type=F32)


def _dot_tn(a, b):
    return lax.dot_general(a, b, (((0,), (0,)), ((), ())), preferred_element_type=F32)


def _layer_norm(y, g, b):
    mu = jnp.mean(y, axis=-1, keepdims=True)
    yc = y - mu
    var = jnp.mean(yc * yc, axis=-1, keepdims=True)
    return yc * lax.rsqrt(var + LN_EPS) * g + b


def _mm_kernel(x_ref, w_ref, o_ref):
    o_ref[...] = _dot(x_ref[...], w_ref[...]).astype(o_ref.dtype)


def matmul(x, w, out_dtype=F32, bm=MM_BLOCK_M, bn=MM_BLOCK_N):
    m, k = x.shape
    n = w.shape[1]
    bm, bn = min(bm, m), min(bn, n)
    assert m % bm == 0 and n % bn == 0
    return pl.pallas_call(
        _mm_kernel,
        grid=(m // bm, n // bn),
        in_specs=[pl.BlockSpec((bm, k), lambda i, j: (i, 0)),
                  pl.BlockSpec((k, bn), lambda i, j: (0, j))],
        out_specs=pl.BlockSpec((bm, bn), lambda i, j: (i, j)),
        out_shape=jax.ShapeDtypeStruct((m, n), out_dtype),
        compiler_params=_params(),
        name="matmul",
    )(x, w)


def _pack_pairs(x):
    n = x.shape[1] // 2
    lo = lax.bitcast_convert_type(x[:, :n].astype(jnp.bfloat16).astype(F32), U32)
    hi = lax.bitcast_convert_type(x[:, n:].astype(jnp.bfloat16).astype(F32), U32)
    return (lo >> 16) | hi


def _unpack_pairs(u):
    lo = lax.bitcast_convert_type(u << 16, F32)
    hi = lax.bitcast_convert_type(u & jnp.uint32(0xFFFF0000), F32)
    return lo, hi


def _store_row_tiles(ref, u):
    rows = u.shape[0]
    sub = ref.shape[0] // rows
    for c in range(sub):
        ref[pl.ds(c, rows, stride=sub), :] = u[:, c * LANES:(c + 1) * LANES]


def _load_row_tiles(ref, rows):
    sub = ref.shape[0] // rows
    return jnp.concatenate([ref[pl.ds(c, rows, stride=sub), :] for c in range(sub)], axis=1)


def _mm_ln_kernel(alpha, a_ref, w_ref, res_ref, g_ref, b_ref, x_ref, xb_ref, xpk_ref):
    y = alpha * res_ref[...] + _dot(a_ref[...], w_ref[...])
    out = _layer_norm(y, g_ref[...], b_ref[...])
    x_ref[...] = out
    xb_ref[...] = out.astype(BF16)
    _store_row_tiles(xpk_ref, _pack_pairs(out))


def matmul_residual_ln(a, w, res, g, b, alpha, bm=LN_BLOCK_M):
    m, k = a.shape
    d = w.shape[1]
    bm = min(bm, m)
    assert m % bm == 0 and d % (2 * LANES) == 0
    sub = d // 2 // LANES
    return pl.pallas_call(
        functools.partial(_mm_ln_kernel, alpha),
        grid=(m // bm,),
        in_specs=[pl.BlockSpec((bm, k), lambda i: (i, 0)),
                  pl.BlockSpec((k, d), lambda i: (0, 0)),
                  pl.BlockSpec((bm, d), lambda i: (i, 0)),
                  pl.BlockSpec((1, d), lambda i: (0, 0)),
                  pl.BlockSpec((1, d), lambda i: (0, 0))],
        out_specs=[pl.BlockSpec((bm, d), lambda i: (i, 0)),
                   pl.BlockSpec((bm, d), lambda i: (i, 0)),
                   pl.BlockSpec((bm * sub, LANES), lambda i: (i, 0))],
        out_shape=[jax.ShapeDtypeStruct((m, d), F32), jax.ShapeDtypeStruct((m, d), BF16),
                   jax.ShapeDtypeStruct((m * sub, LANES), U32)],
        compiler_params=_params(),
        name="matmul_residual_ln",
    )(a, w, res, g.reshape(1, d), b.reshape(1, d))


def _hgrn_tables(c):
    t = np.arange(c)
    levels = []
    b = c // 2
    while b >= 1:
        levels.append(b)
        b //= 2
    sel = [(t[None, :] <= t[:, None]).astype(np.float32)]
    lvl = np.where(np.eye(c, dtype=bool), 0, -1).astype(np.int32)
    for l, b in enumerate(levels):
        blk = t // b
        odd = (blk % 2) == 1
        ref_row = np.where(odd, blk * b - 1, blk * b + b - 1)
        sel.append((t[None, :] <= ref_row[:, None]).astype(np.float32))
        same = (t[:, None] // (2 * b)) == (t[None, :] // (2 * b))
        lvl = np.where(same & odd[:, None] & (~odd)[None, :], l + 1, lvl)
    sel = np.concatenate(sel, axis=0)
    return np.concatenate([sel, sel, sel], axis=1), lvl


def _hgrn_kernel(n_heads, c, q_ref, f_ref, i_ref, g_ref, s0_ref, lb_ref, gn_ref, sel_ref, lvl_ref,
                 o_ref, sfin_ref, state_ref, cums_ref, qw_ref, kw_ref):
    j = pl.program_id(1)

    @pl.when(j == 0)
    def _():
        state_ref[...] = s0_ref[0]

    n_lvl = qw_ref.shape[0]
    lb = lb_ref[...]
    f = lb + (1.0 - lb) * _sigmoid(f_ref[...])
    k = 1.0 - f
    q = _silu(q_ref[...])
    lg = jnp.log(f)
    hi = lg.astype(BF16)
    rest = lg - hi.astype(F32)
    mid = rest.astype(BF16)
    lo = (rest - mid.astype(F32)).astype(BF16)
    cums_ref[...] = _dot(sel_ref[...], jnp.concatenate([hi, mid, lo], axis=0))
    cum = cums_ref[0:c]
    qw_ref[0] = q.astype(BF16)
    kw_ref[0] = k.astype(BF16)
    for l in range(1, n_lvl):
        w = jnp.exp(-jnp.abs(cum - cums_ref[l * c:(l + 1) * c]))
        qw_ref[l] = (q * w).astype(BF16)
        kw_ref[l] = (k * w).astype(BF16)
    last = cum[c - 1:c, :]
    q_in = (q * jnp.exp(cum)).astype(BF16)
    k_end = (k * jnp.exp(last - cum)).astype(BF16)
    decay = jnp.exp(last)
    lvl = lvl_ref[...]
    for h in range(n_heads):
        hs = slice(h * HEAD_DIM, (h + 1) * HEAD_DIM)
        st = state_ref[h]
        o = _dot_nt(q_in[:, hs], st.astype(BF16))
        scores = jnp.zeros((c, c), F32)
        for l in range(n_lvl):
            scores = jnp.where(lvl == l, _dot_nt(qw_ref[l, :, hs], kw_ref[l, :, hs]), scores)
        v = i_ref[:, hs].astype(BF16)
        o = o + _dot(scores.astype(BF16), v)
        state_ref[h] = st * decay[:, hs] + _dot_tn(v, k_end[:, hs])
        o = o * lax.rsqrt(jnp.mean(o * o, axis=-1, keepdims=True) + RMS_EPS) * gn_ref[...]
        o_ref[:, hs] = (o * _silu(g_ref[:, hs])).astype(o_ref.dtype)

    @pl.when(j == pl.num_programs(1) - 1)
    def _():
        sfin_ref[0] = state_ref[...]


def hgrn_mixer(proj, row0, batch, seq, s0_t, lb, gnorm):
    width = proj.shape[1] // 5
    n_heads = width // HEAD_DIM
    c = HGRN_BLOCK_T if seq % HGRN_BLOCK_T == 0 else seq
    n = seq // c
    assert row0 % c == 0 and c & (c - 1) == 0
    blk0 = row0 // c
    sel, lvl = _hgrn_tables(c)
    n_lvl = sel.shape[0] // c

    def sec(s):
        return pl.BlockSpec((c, width), lambda b, j: (blk0 + b * n + j, s))

    state_spec = pl.BlockSpec((1, n_heads, HEAD_DIM, HEAD_DIM), lambda b, j: (b, 0, 0, 0))
    return pl.pallas_call(
        functools.partial(_hgrn_kernel, n_heads, c),
        grid=(batch, n),
        in_specs=[sec(0), sec(1), sec(2), sec(3), state_spec,
                  pl.BlockSpec((1, width), lambda b, j: (0, 0)),
                  pl.BlockSpec((1, HEAD_DIM), lambda b, j: (0, 0)),
                  pl.BlockSpec(sel.shape, lambda b, j: (0, 0)),
                  pl.BlockSpec(lvl.shape, lambda b, j: (0, 0))],
        out_specs=[pl.BlockSpec((c, width), lambda b, j: (b * n + j, 0)), state_spec],
        out_shape=[jax.ShapeDtypeStruct((batch * seq, width), BF16),
                   jax.ShapeDtypeStruct((batch, n_heads, HEAD_DIM, HEAD_DIM), F32)],
        scratch_shapes=[pltpu.VMEM((n_heads, HEAD_DIM, HEAD_DIM), F32),
                        pltpu.VMEM((n_lvl * c, width), F32),
                        pltpu.VMEM((n_lvl, c, width), BF16),
                        pltpu.VMEM((n_lvl, c, width), BF16)],
        compiler_params=_params(),
        name="hgrn_mixer",
    )(proj, proj, proj, proj, s0_t, lb.reshape(1, width), gnorm.reshape(1, HEAD_DIM),
      jnp.asarray(sel).astype(BF16), jnp.asarray(lvl))


def _pool_kernel(pos0, bt, group, u_ref, prev_ref, w_ref, scale_ref, o_ref, ext_ref):
    j = pl.program_id(1)

    @pl.when(j == 0)
    def _():
        ext_ref[0:POOL_CARRY] = prev_ref[0]

    u = u_ref[...]
    ext_ref[POOL_CARRY:POOL_CARRY + bt] = u
    pos = pos0 + j * bt + lax.broadcasted_iota(I32, (bt, 1), 0)
    for gi, win in enumerate(POOL_WINDOWS):
        cs = slice(gi * group, (gi + 1) * group)
        s = ext_ref[:, cs]
        step = 1
        while step < win:
            s = s + pltpu.roll(s, step, axis=0)
            step *= 2
        cnt = jnp.minimum(pos + 1, win).astype(F32)
        d = s[POOL_CARRY:] / cnt - u[:, cs]
        y = _dot(d.astype(BF16), w_ref[gi]) * scale_ref[:, cs]
        o_ref[:, cs] = y.astype(o_ref.dtype)
    ext_ref[0:POOL_CARRY] = ext_ref[bt:bt + POOL_CARRY]


def pool_mixer(proj, row0, batch, seq, pos0, prev, pool_w, pool_scale):
    width = proj.shape[1] // 5
    group = width // len(POOL_WINDOWS)
    bt = min(POOL_BLOCK_T, seq)
    n = seq // bt
    assert seq % bt == 0 and row0 % bt == 0 and sum(POOL_WINDOWS) // 2 <= POOL_CARRY
    blk0 = row0 // bt
    return pl.pallas_call(
        functools.partial(_pool_kernel, pos0, bt, group),
        grid=(batch, n),
        in_specs=[pl.BlockSpec((bt, width), lambda b, j: (blk0 + b * n + j, 4)),
                  pl.BlockSpec((1, POOL_CARRY, width), lambda b, j: (b, 0, 0)),
                  pl.BlockSpec(pool_w.shape, lambda b, j: (0, 0, 0)),
                  pl.BlockSpec((1, width), lambda b, j: (0, 0))],
        out_specs=pl.BlockSpec((bt, width), lambda b, j: (b * n + j, 0)),
        out_shape=jax.ShapeDtypeStruct((batch * seq, width), BF16),
        scratch_shapes=[pltpu.VMEM((bt + POOL_CARRY, width), F32)],
        compiler_params=_params(),
        name="pool_mixer",
    )(proj, prev, pool_w, pool_scale.reshape(1, width))


def _rel_bias(table, dist):
    idx = jnp.clip(dist, -(CHUNK - 1), MAX_REL) + (CHUNK - 1)
    return table[:, idx].astype(F32)


def _attn_prompt_kernel(bq, hps, scale, q_ref, k0_ref, k1_ref, k2_ref, v0_ref, v1_ref, v2_ref,
                        vals_ref, o_ref, bias_ref):
    b = pl.program_id(1)
    i = pl.program_id(2)

    @pl.when(jnp.logical_and(b == 0, i == 0))
    def _():
        span = vals_ref.shape[-1]
        qi = lax.broadcasted_iota(I32, (bq, 3 * bq), 0)
        kj = lax.broadcasted_iota(I32, (bq, 3 * bq), 1)
        lo = (qi // CHUNK) * CHUNK
        in_band = jnp.logical_and(kj >= lo, kj < lo + WINDOW + CHUNK)
        for hh in range(hps):
            rows = jnp.broadcast_to(vals_ref[0, hh:hh + 1, :], (bq, span))
            toep = pltpu.roll(rows, span - (bq - 1), 1, stride=1, stride_axis=0)
            bias_ref[hh] = jnp.where(in_band, toep[:, :3 * bq], NEG)

    ks = (k0_ref, k1_ref, k2_ref)
    vs = (v0_ref, v1_ref, v2_ref)
    for hh in range(hps):
        hs = slice(hh * HEAD_DIM, (hh + 1) * HEAD_DIM)
        q = q_ref[:, hs]
        s = []
        for p in range(3):
            sp = _dot_nt(q, ks[p][:, hs]) * scale + bias_ref[hh, :, p * bq:(p + 1) * bq]
            if p < 2:
                sp = jnp.where(i >= 2 - p, sp, NEG)
            s.append(sp)
        m = jnp.maximum(jnp.maximum(jnp.max(s[0], axis=-1, keepdims=True),
                                    jnp.max(s[1], axis=-1, keepdims=True)),
                        jnp.max(s[2], axis=-1, keepdims=True))
        e = [jnp.exp(sp - m) for sp in s]
        inv = 1.0 / (jnp.sum(e[0], axis=-1, keepdims=True) + jnp.sum(e[1], axis=-1, keepdims=True)
                     + jnp.sum(e[2], axis=-1, keepdims=True))
        o = _dot((e[0] * inv).astype(BF16), vs[0][:, hs])
        o = o + _dot((e[1] * inv).astype(BF16), vs[1][:, hs])
        o = o + _dot((e[2] * inv).astype(BF16), vs[2][:, hs])
        o_ref[:, hs] = o.astype(o_ref.dtype)


def band_attention_prompt(qkv, batch, seq, rel_bias):
    d = qkv.shape[1] // 3
    n_heads = d // HEAD_DIM
    bq = ATTN_BLOCK_Q
    hps = min(ATTN_HEADS_PER_STEP, n_heads)
    assert seq % bq == 0 and WINDOW == 2 * bq and bq % CHUNK == 0 and n_heads % hps == 0
    nq = seq // bq
    ng = n_heads // hps
    offs = jnp.arange(4 * bq) - (bq - 1)
    vals = _rel_bias(rel_bias, WINDOW - offs).reshape(ng, hps, 4 * bq)

    def kv(sec, p):
        return pl.BlockSpec((bq, hps * HEAD_DIM),
                            lambda g, b, i: (b * nq + jnp.maximum(i - 2 + p, 0), sec * ng + g))

    return pl.pallas_call(
        functools.partial(_attn_prompt_kernel, bq, hps, HEAD_DIM ** -0.5),
        grid=(ng, batch, nq),
        in_specs=[pl.BlockSpec((bq, hps * HEAD_DIM), lambda g, b, i: (b * nq + i, g)),
                  kv(1, 0), kv(1, 1), kv(1, 2), kv(2, 0), kv(2, 1), kv(2, 2),
                  pl.BlockSpec((1, hps, 4 * bq), lambda g, b, i: (g, 0, 0))],
        out_specs=pl.BlockSpec((bq, hps * HEAD_DIM), lambda g, b, i: (b * nq + i, g)),
        out_shape=jax.ShapeDtypeStruct((batch * seq, d), BF16),
        scratch_shapes=[pltpu.VMEM((hps, bq, 3 * bq), F32)],
        compiler_params=_params(),
        name="band_attention_prompt",
    )(qkv, qkv, qkv, qkv, qkv, qkv, qkv, vals)


def _attn_sample_kernel(n_heads, lc, scale, qkv_ref, ck_ref, cv_ref, bias_ref, o_ref):
    d = n_heads * HEAD_DIM
    for h in range(n_heads):
        hs = slice(h * HEAD_DIM, (h + 1) * HEAD_DIM)
        q = qkv_ref[:, h * HEAD_DIM:(h + 1) * HEAD_DIM].astype(BF16)
        kn = qkv_ref[:, d + h * HEAD_DIM:d + (h + 1) * HEAD_DIM].astype(BF16)
        vn = qkv_ref[:, 2 * d + h * HEAD_DIM:2 * d + (h + 1) * HEAD_DIM].astype(BF16)
        kc = ck_ref[0, 0, :, h, :].astype(BF16)
        vc = cv_ref[0, 0, :, h, :].astype(BF16)
        sc = _dot_nt(q, kc) * scale + bias_ref[h, :, 0:lc]
        sn = _dot_nt(q, kn) * scale + bias_ref[h, :, lc:]
        m = jnp.maximum(jnp.max(sc, axis=-1, keepdims=True), jnp.max(sn, axis=-1, keepdims=True))
        ec = jnp.exp(sc - m)
        en = jnp.exp(sn - m)
        inv = 1.0 / (jnp.sum(ec, axis=-1, keepdims=True) + jnp.sum(en, axis=-1, keepdims=True))
        o = _dot((ec * inv).astype(BF16), vc) + _dot((en * inv).astype(BF16), vn)
        o_ref[:, hs] = o.astype(o_ref.dtype)


def band_attention_sample(qkv, row0, batch, seq, cache_k, cache_v, layer, rel_bias):
    d = qkv.shape[1] // 3
    n_heads = d // HEAD_DIM
    lc = cache_k.shape[2]
    assert row0 % seq == 0 and lc % 128 == 0
    blk0 = row0 // seq
    dist = jnp.arange(seq)[:, None] + lc - jnp.arange(lc + seq)[None, :]
    bias = _rel_bias(rel_bias, dist)
    cache_spec = pl.BlockSpec((1, 1, lc, n_heads, HEAD_DIM), lambda b: (layer, b, 0, 0, 0))
    return pl.pallas_call(
        functools.partial(_attn_sample_kernel, n_heads, lc, HEAD_DIM ** -0.5),
        grid=(batch,),
        in_specs=[pl.BlockSpec((seq, 3 * d), lambda b: (blk0 + b, 0)),
                  cache_spec, cache_spec,
                  pl.BlockSpec(bias.shape, lambda b: (0, 0, 0))],
        out_specs=pl.BlockSpec((seq, d), lambda b: (b, 0)),
        out_shape=jax.ShapeDtypeStruct((batch * seq, d), BF16),
        compiler_params=_params(),
        name="band_attention_sample",
    )(qkv, cache_k, cache_v, bias)


def _first_max(vals, idx, sentinel):
    m = jnp.max(vals, axis=0, keepdims=True)
    first = jnp.min(jnp.where(vals == m, idx, sentinel), axis=0, keepdims=True)
    return m, first


def _stack_rows(rows, lanes):
    ridx = lax.broadcasted_iota(I32, (len(rows), lanes), 0)
    out = jnp.zeros((len(rows), lanes), rows[0].dtype)
    for r, row in enumerate(rows):
        out = jnp.where(ridx == r, row, out)
    return out


def _router_kernel(n_experts, bt, x_ref, wt_ref, bias_ref, e_ref, gate_ref, rank_ref, cnt_ref,
                   carry_ref):
    i = pl.program_id(0)

    @pl.when(i == 0)
    def _():
        carry_ref[...] = jnp.zeros_like(carry_ref)

    per = n_experts // N_GROUPS
    logits = lax.dot_general(wt_ref[...], x_ref[...], (((1,), (1,)), ((), ())),
                             precision=lax.Precision.HIGHEST, preferred_element_type=F32)
    scores = _sigmoid(logits)
    sel = scores + bias_ref[:, 0:1]

    sub = lax.broadcasted_iota(I32, (per, bt), 0)
    grp_rows = []
    for g in range(N_GROUPS):
        blk = sel[g * per:(g + 1) * per]
        m1, i1 = _first_max(blk, sub, per)
        m2 = jnp.max(jnp.where(sub == i1, -jnp.inf, blk), axis=0, keepdims=True)
        grp_rows.append(m1 + m2)
    grp = _stack_rows(grp_rows, bt)
    gidx = lax.broadcasted_iota(I32, (N_GROUPS, bt), 0)
    keep = jnp.zeros((N_GROUPS, bt), F32)
    for _ in range(TOPK_GROUPS):
        _, first = _first_max(grp, gidx, N_GROUPS)
        hit = gidx == first
        keep = jnp.where(hit, 1.0, keep)
        grp = jnp.where(hit, -jnp.inf, grp)
    masked = jnp.concatenate(
        [jnp.where(keep[g:g + 1] > 0.0, sel[g * per:(g + 1) * per], -jnp.inf)
         for g in range(N_GROUPS)], axis=0)

    eidx = lax.broadcasted_iota(I32, (n_experts, bt), 0)
    chosen = jnp.zeros((n_experts, bt), F32)
    picks = []
    gates = []
    for _ in range(TOP_K):
        _, first = _first_max(masked, eidx, n_experts)
        hit = eidx == first
        picks.append(first)
        gates.append(jnp.sum(jnp.where(hit, scores, 0.0), axis=0, keepdims=True))
        chosen = jnp.where(hit, 1.0, chosen)
        masked = jnp.where(hit, -jnp.inf, masked)
    gate = _stack_rows(gates, bt)
    gate = gate / jnp.sum(gate, axis=0, keepdims=True) * ROUTE_SCALE

    earlier = (lax.broadcasted_iota(I32, (bt, bt), 0) < lax.broadcasted_iota(I32, (bt, bt), 1))
    before = _dot(chosen.astype(BF16), jnp.where(earlier, 1.0, 0.0).astype(BF16)) + carry_ref[:, 0:1]
    ranks = [jnp.sum(jnp.where(eidx == p, before, 0.0), axis=0, keepdims=True) for p in picks]

    e_ref[...] = _stack_rows(picks, bt)
    gate_ref[...] = gate
    rank_ref[...] = _stack_rows(ranks, bt).astype(I32)
    carry_ref[...] = carry_ref[...] + jnp.sum(chosen, axis=1, keepdims=True)
    cnt_ref[...] = carry_ref[...]


def moe_router(x, w_router, router_bias, bt=ROUTER_BLOCK_T):
    n, d = x.shape
    n_experts = w_router.shape[1]
    bt = min(bt, n)
    assert n % bt == 0 and n_experts % N_GROUPS == 0
    lanes = 128
    tok = pl.BlockSpec((TOP_K, bt), lambda i: (0, i))
    e, gate, rank, cnt = pl.pallas_call(
        functools.partial(_router_kernel, n_experts, bt),
        grid=(n // bt,),
        in_specs=[pl.BlockSpec((bt, d), lambda i: (i, 0)),
                  pl.BlockSpec((n_experts, d), lambda i: (0, 0)),
                  pl.BlockSpec((n_experts, lanes), lambda i: (0, 0))],
        out_specs=[tok, tok, tok, pl.BlockSpec((n_experts, lanes), lambda i: (0, 0))],
        out_shape=[jax.ShapeDtypeStruct((TOP_K, n), I32), jax.ShapeDtypeStruct((TOP_K, n), F32),
                   jax.ShapeDtypeStruct((TOP_K, n), I32),
                   jax.ShapeDtypeStruct((n_experts, lanes), F32)],
        scratch_shapes=[pltpu.VMEM((n_experts, lanes), F32)],
        compiler_params=_params(),
        name="moe_router",
    )(x, w_router.T, jnp.broadcast_to(router_bias.astype(F32)[:, None], (n_experts, lanes)))
    return e, gate, rank, cnt[:, 0].astype(I32)


def _expert_kernel(blk, tile_e_ref, tok_ref, tok_next_ref, xpk_ref, wgu_ref, wdn_ref, ys_ref,
                   wgu_b, wdn_b, xbuf, xbf, sems):
    i = pl.program_id(0)
    de = wdn_b.shape[0]
    half = xbf.shape[1] // 2
    sub = xpk_ref.shape[1]
    slot = i % 2
    off = slot * blk

    def row_copy(idx_ref, base, dst_slot, r):
        return pltpu.make_async_copy(xpk_ref.at[idx_ref[base + r]],
                                     xbuf.at[dst_slot, pl.ds(r * sub, sub)], sems.at[dst_slot])

    def drain(dst_slot):
        pltpu.make_async_copy(xbuf.at[1 - dst_slot], xbuf.at[dst_slot], sems.at[dst_slot]).wait()

    @pl.when(i == 0)
    def _():
        def body(r, carry):
            row_copy(tok_ref, 0, 0, r).start()
            return carry
        lax.fori_loop(0, blk, body, 0)

    changed = jnp.logical_or(i == 0, tile_e_ref[i] != tile_e_ref[jnp.maximum(i - 1, 0)])

    @pl.when(changed)
    def _():
        wgu_b[...] = wgu_ref[0, 0].astype(BF16)
        wdn_b[...] = wdn_ref[0, 0].astype(BF16)

    drain(slot)
    lo, hi = _unpack_pairs(_load_row_tiles(xbuf.at[slot], blk))
    xbf[:, :half] = lo.astype(BF16)
    xbf[:, half:] = hi.astype(BF16)
    for r in range(blk):
        row_copy(tok_next_ref, blk - off, 1 - slot, r).start(priority=r % 2)
    gu = _dot(xbf[...], wgu_b[...])
    a = _silu(gu[:, :de]) * gu[:, de:]
    _store_row_tiles(ys_ref, _pack_pairs(_dot(a.astype(BF16), wdn_b[...])))

    @pl.when(i == pl.num_programs(0) - 1)
    def _():
        drain(1 - slot)


def moe_experts(xpk, row_tok, tile_e, w_gu, w_dn, layer, blk=EXPERT_BLOCK):
    n, sub, _ = xpk.shape
    d = 2 * sub * LANES
    rows = row_tok.shape[0]
    n_tiles = rows // blk
    assert rows % (2 * blk) == 0
    de = w_dn.shape[2]
    last_pair = n_tiles // 2 - 1
    return pl.pallas_call(
        functools.partial(_expert_kernel, blk),
        grid_spec=pltpu.PrefetchScalarGridSpec(
            num_scalar_prefetch=1,
            grid=(n_tiles,),
            in_specs=[pl.BlockSpec((2 * blk,), lambda i, te: (i // 2,), memory_space=pltpu.SMEM),
                      pl.BlockSpec((2 * blk,), lambda i, te: (jnp.minimum((i + 1) // 2, last_pair),),
                                   memory_space=pltpu.SMEM),
                      pl.BlockSpec(memory_space=pl.ANY),
                      pl.BlockSpec((1, 1, d, 2 * de), lambda i, te: (layer, te[i], 0, 0)),
                      pl.BlockSpec((1, 1, de, d), lambda i, te: (layer, te[i], 0, 0))],
            out_specs=pl.BlockSpec((blk * sub, LANES), lambda i, te: (i, 0)),
            scratch_shapes=[pltpu.VMEM((d, 2 * de), BF16), pltpu.VMEM((de, d), BF16),
                            pltpu.VMEM((2, blk * sub, LANES), U32), pltpu.VMEM((blk, d), BF16),
                            pltpu.SemaphoreType.DMA((2,))]),
        out_shape=jax.ShapeDtypeStruct((rows * sub, LANES), U32),
        compiler_params=_params(),
        name="moe_experts",
    )(tile_e, row_tok, row_tok, xpk, w_gu, w_dn)


def _combine_kernel(alpha, bt, dest_ref, dest_next_ref, ys_ref, gate_ref, x_ref, xb_ref, wsgu_ref,
                    wsdn_ref, g_ref, b_ref, xo_ref, xbo_ref, buf_a, buf_b, sem_a, sem_b):
    i = pl.program_id(0)
    de = wsdn_ref.shape[0]
    sub = ys_ref.shape[1]
    per_tile = bt * TOP_K

    def row_copy(idx_ref, base, buf, sem, t, k):
        return pltpu.make_async_copy(ys_ref.at[idx_ref[base + t * TOP_K + k]],
                                     buf.at[k, pl.ds(t * sub, sub)], sem)

    def issue_unrolled(idx_ref, base, buf, sem):
        for t in range(bt):
            for k in range(TOP_K):
                row_copy(idx_ref, base, buf, sem, t, k).start(priority=k % 2)

    def drain(buf, other, sem):
        pltpu.make_async_copy(other, buf, sem).wait()

    def compute(r0, buf):
        rows = pl.ds(r0, bt)
        sgu = _dot(xb_ref[rows, :], wsgu_ref[...])
        shared = _dot((_silu(sgu[:, :de]) * sgu[:, de:]).astype(BF16), wsdn_ref[...])
        routed_lo, routed_hi = None, None
        for k in range(TOP_K):
            gate = gate_ref[rows, k:k + 1]
            lo, hi = _unpack_pairs(_load_row_tiles(buf.at[k], bt))
            routed_lo = gate * lo if k == 0 else routed_lo + gate * lo
            routed_hi = gate * hi if k == 0 else routed_hi + gate * hi
        routed = jnp.concatenate([routed_lo, routed_hi], axis=1)
        out = _layer_norm(alpha * x_ref[rows, :] + (routed + shared), g_ref[...], b_ref[...])
        xo_ref[rows, :] = out
        xbo_ref[rows, :] = out.astype(BF16)

    @pl.when(i == 0)
    def _():
        def body(t, carry):
            for k in range(TOP_K):
                row_copy(dest_ref, 0, buf_a, sem_a, t, k).start()
            return carry
        lax.fori_loop(0, bt, body, 0)

    drain(buf_a, buf_b, sem_a)
    issue_unrolled(dest_ref, per_tile, buf_b, sem_b)
    compute(0, buf_a)
    drain(buf_b, buf_a, sem_b)
    issue_unrolled(dest_next_ref, 0, buf_a, sem_a)
    compute(bt, buf_b)

    @pl.when(i == pl.num_programs(0) - 1)
    def _():
        drain(buf_a, buf_b, sem_a)


def moe_combine_ln(ys, dest_flat, gate_tm, x, xb, ws_gu, ws_dn, g, b, alpha, bt=COMBINE_BLOCK_T):
    n, d = x.shape
    bt = min(bt, n // 2)
    assert n % (2 * bt) == 0
    n_tiles = n // bt
    de = ws_dn.shape[0]
    sub = ys.shape[1]
    row = pl.BlockSpec((2 * bt, d), lambda i: (i, 0))
    vec = pl.BlockSpec((1, d), lambda i: (0, 0))
    return pl.pallas_call(
        functools.partial(_combine_kernel, alpha, bt),
        grid=(n_tiles // 2,),
        in_specs=[pl.BlockSpec((2 * bt * TOP_K,), lambda i: (i,), memory_space=pltpu.SMEM),
                  pl.BlockSpec((bt * TOP_K,), lambda i: (jnp.minimum(2 * i + 2, n_tiles - 1),),
                               memory_space=pltpu.SMEM),
                  pl.BlockSpec(memory_space=pl.ANY),
                  pl.BlockSpec((2 * bt, TOP_K), lambda i: (i, 0)),
                  row, row,
                  pl.BlockSpec((d, 2 * de), lambda i: (0, 0)),
                  pl.BlockSpec((de, d), lambda i: (0, 0)),
                  vec, vec],
        out_specs=[row, row],
        out_shape=[jax.ShapeDtypeStruct((n, d), F32), jax.ShapeDtypeStruct((n, d), BF16)],
        scratch_shapes=[pltpu.VMEM((TOP_K, bt * sub, LANES), U32),
                        pltpu.VMEM((TOP_K, bt * sub, LANES), U32),
                        pltpu.SemaphoreType.DMA, pltpu.SemaphoreType.DMA],
        compiler_params=_params(),
        name="moe_combine_ln",
    )(dest_flat, dest_flat, ys, gate_tm, x, xb, ws_gu, ws_dn, g.reshape(1, d), b.reshape(1, d))


def moe_ffn_ln(x, xb, xpk, w_router, router_bias, w_gu, w_dn, layer, ws_gu, ws_dn, g, b, alpha):
    n, d = x.shape
    n_experts = w_router.shape[1]
    blk = EXPERT_BLOCK
    top_e, gate, rank, counts = moe_router(x, w_router, router_bias)
    padded = (counts + blk - 1) // blk * blk
    pend = jnp.cumsum(padded)
    pstart = pend - padded
    hot = top_e[:, :, None] == jnp.arange(n_experts, dtype=I32)
    dest = jnp.sum(jnp.where(hot, pstart, 0), axis=-1) + rank
    dest_flat = dest.T.reshape(-1)
    n_tiles = (n * TOP_K + n_experts * (blk - 1) + 2 * blk - 1) // (2 * blk) * 2
    tile_row0 = jnp.arange(n_tiles, dtype=I32) * blk
    tile_e = jnp.minimum(jnp.sum((pend[None, :] <= tile_row0[:, None]).astype(I32), axis=1),
                         n_experts - 1)
    row_tok = jnp.zeros((n_tiles * blk,), I32).at[dest_flat].set(
        jnp.arange(n * TOP_K, dtype=I32) // TOP_K, unique_indices=True)
    sub = d // 2 // LANES
    ys = moe_experts(xpk.reshape(n, sub, LANES), row_tok, tile_e, w_gu, w_dn, layer, blk)
    ys = ys.reshape(n_tiles * blk, sub, LANES)
    return moe_combine_ln(ys, dest_flat, gate.T, x, xb, ws_gu, ws_dn, g, b, alpha)


def kernel(x_prompt, x_sample, state_hgrn, state_pool, cache_k, cache_v, ln_g, ln_b, w_in_even,
           lb_even, gnorm_even, pool_w_even, pool_scale_even, w_out_even, w_qkv_odd, rel_bias_odd,
           w_o_odd, w_router, router_bias, w_gu, w_dn, ws_gu, ws_dn):
    batch, seq, d = x_prompt.shape
    dec_batch, dec_seq, _ = x_sample.shape
    depth = ln_g.shape[0]
    n_p = batch * seq
    n_s = dec_batch * dec_seq
    width = w_in_even.shape[2] // 5
    n_heads_a = width // HEAD_DIM
    n_heads_c = d // HEAD_DIM
    alpha = (2 * depth) ** 0.25
    assert dec_seq >= POOL_PREV and seq >= max(POOL_PREV, WINDOW)

    x = jnp.concatenate([x_prompt.reshape(n_p, d), x_sample.reshape(n_s, d)], axis=0)
    xb = x.astype(BF16)

    p = jax.nn.softmax(lb_even.astype(F32), axis=0)
    lbs = jnp.cumsum(p, axis=0) - p[0]

    hgrn_p, pool_p, k_p, v_p, hgrn_s, pool_s, k_s, v_s = [], [], [], [], [], [], [], []
    for l in range(depth):
        j = l // 2
        if l % 2 == 0:
            proj = matmul(xb, w_in_even[j].astype(BF16))
            zero_state = jnp.zeros((batch, n_heads_a, HEAD_DIM, HEAD_DIM), F32)
            oa_p, sp = hgrn_mixer(proj, 0, batch, seq, zero_state, lbs[j], gnorm_even[j])
            oa_s, ss = hgrn_mixer(proj, n_p, dec_batch, dec_seq,
                                  jnp.swapaxes(state_hgrn[j].astype(F32), -1, -2), lbs[j],
                                  gnorm_even[j])
            pw = pool_w_even[j].astype(BF16)
            ob_p = pool_mixer(proj, 0, batch, seq, 0, jnp.zeros((batch, POOL_CARRY, width), F32),
                              pw, pool_scale_even[j])
            prev_s = jnp.pad(state_pool[j].astype(F32),
                             ((0, 0), (POOL_CARRY - POOL_PREV, 0), (0, 0)))
            ob_s = pool_mixer(proj, n_p, dec_batch, dec_seq, PAST_LEN, prev_s, pw,
                              pool_scale_even[j])
            mixed = jnp.concatenate([jnp.concatenate([oa_p, oa_s], axis=0),
                                     jnp.concatenate([ob_p, ob_s], axis=0)], axis=1)
            w_mix = w_out_even[j]
            hgrn_p.append(jnp.swapaxes(sp, -1, -2))
            hgrn_s.append(jnp.swapaxes(ss, -1, -2))
            pool_p.append(jnp.stack([proj[(b + 1) * seq - POOL_PREV:(b + 1) * seq, 4 * width:]
                                     for b in range(batch)]))
            pool_s.append(proj[n_p:].reshape(dec_batch, dec_seq, 5 * width)
                          [:, dec_seq - POOL_PREV:, 4 * width:])
        else:
            qkv = matmul(xb, w_qkv_odd[j].astype(BF16), out_dtype=BF16)
            o_p = band_attention_prompt(qkv, batch, seq, rel_bias_odd[j])
            o_s = band_attention_sample(qkv, n_p, dec_batch, dec_seq, cache_k, cache_v, j,
                                        rel_bias_odd[j])
            mixed = jnp.concatenate([o_p, o_s], axis=0)
            w_mix = w_o_odd[j]
            keep = min(WINDOW, seq)
            tail_p = jnp.stack([qkv[(b + 1) * seq - keep:(b + 1) * seq]
                                for b in range(batch)]).astype(F32)
            new_s = qkv[n_p:].reshape(dec_batch, dec_seq, 3 * d).astype(F32)
            k_p.append(tail_p[:, :, d:2 * d].reshape(batch, keep, n_heads_c, HEAD_DIM))
            v_p.append(tail_p[:, :, 2 * d:].reshape(batch, keep, n_heads_c, HEAD_DIM))
            k_s.append(new_s[:, :, d:2 * d].reshape(dec_batch, dec_seq, n_heads_c, HEAD_DIM))
            v_s.append(new_s[:, :, 2 * d:].reshape(dec_batch, dec_seq, n_heads_c, HEAD_DIM))
        x, xb, xpk = matmul_residual_ln(mixed, w_mix.astype(BF16), x, ln_g[l, 0], ln_b[l, 0], alpha)
        x, xb = moe_ffn_ln(x, xb, xpk, w_router[l], router_bias[l], w_gu, w_dn, l,
                           ws_gu[l].astype(BF16), ws_dn[l].astype(BF16), ln_g[l, 1], ln_b[l, 1],
                           alpha)

    y_p = x[:n_p].reshape(batch, seq, d)
    y_s = x[n_p:].reshape(dec_batch, dec_seq, d)
    return (y_p, y_s, jnp.stack(hgrn_p), jnp.stack(pool_p), jnp.stack(k_p), jnp.stack(v_p),
            jnp.stack(hgrn_s), jnp.stack(pool_s), jnp.stack(k_s), jnp.stack(v_s))
```

```python
import functools

import numpy as np
import jax
import jax.numpy as jnp
from jax import lax
from jax.experimental import pallas as pl
from jax.experimental.pallas import tpu as pltpu

F32 = jnp.float32
BF16 = jnp.bfloat16
I32 = jnp.int32
U32 = jnp.uint32

CHUNK = 64
HEAD_DIM = 128
LANES = 128
POOL_WINDOWS = (2, 4, 8, 16)
POOL_PREV = max(POOL_WINDOWS) - 1
POOL_CARRY = POOL_PREV + 1
LEFT_CHUNKS = 8
WINDOW = LEFT_CHUNKS * CHUNK
MAX_REL = 128
N_GROUPS = 8
TOPK_GROUPS = 4
TOP_K = 8
ROUTE_SCALE = 2.5
LN_EPS = 1e-5
RMS_EPS = 1e-6
PAST_LEN = 1024
NEG = -1e30

VMEM_LIMIT_BYTES = 56 * 1024 * 1024
MM_BLOCK_M = 1024
MM_BLOCK_N = 1024
LN_BLOCK_M = 512
HGRN_BLOCK_T = 128
ATTN_BLOCK_Q = 256
ATTN_HEADS_PER_STEP = 4
POOL_BLOCK_T = 512
ROUTER_BLOCK_T = 512
DISPATCH_BLOCK_T = 128
EXPERT_BLOCK = 512
COMBINE_BLOCK_T = 128


def _params(**kw):
    return pltpu.CompilerParams(vmem_limit_bytes=VMEM_LIMIT_BYTES, **kw)


def _sigmoid(x):
    return 1.0 / (1.0 + jnp.exp(-x))


def _silu(x):
    return x * _sigmoid(x)


def _dot(a, b):
    return jnp.dot(a, b, preferred_element_type=F32)


def _dot_nt(a, b):
    return lax.dot_general(a, b, (((1,), (1,)), ((), ())), preferred_element_type=F32)


def _dot_tn(a, b):
    return lax.dot_general(a, b, (((0,), (0,)), ((), ())), preferred_element_type=F32)


def _layer_norm(y, g, b):
    mu = jnp.mean(y, axis=-1, keepdims=True)
    yc = y - mu
    var = jnp.mean(yc * yc, axis=-1, keepdims=True)
    return yc * lax.rsqrt(var + LN_EPS) * g + b


def _mm_kernel(x_ref, w_ref, o_ref):
    o_ref[...] = _dot(x_ref[...], w_ref[...]).astype(o_ref.dtype)


def matmul(x, w, out_dtype=F32, bm=MM_BLOCK_M, bn=MM_BLOCK_N):
    m, k = x.shape
    n = w.shape[1]
    bm, bn = min(bm, m), min(bn, n)
    assert m % bm == 0 and n % bn == 0
    return pl.pallas_call(
        _mm_kernel,
        grid=(m // bm, n // bn),
        in_specs=[pl.BlockSpec((bm, k), lambda i, j: (i, 0)),
                  pl.BlockSpec((k, bn), lambda i, j: (0, j))],
        out_specs=pl.BlockSpec((bm, bn), lambda i, j: (i, j)),
        out_shape=jax.ShapeDtypeStruct((m, n), out_dtype),
        compiler_params=_params(),
        name="matmul",
    )(x, w)


def _pack_pairs(x):
    n = x.shape[1] // 2
    lo = lax.bitcast_convert_type(x[:, :n].astype(jnp.bfloat16).astype(F32), U32)
    hi = lax.bitcast_convert_type(x[:, n:].astype(jnp.bfloat16).astype(F32), U32)
    return (lo >> 16) | hi


def _unpack_pairs(u):
    lo = lax.bitcast_convert_type(u << 16, F32)
    hi = lax.bitcast_convert_type(u & jnp.uint32(0xFFFF0000), F32)
    return lo, hi


def _store_row_tiles(ref, u):
    rows = u.shape[0]
    sub = ref.shape[0] // rows
    for c in range(sub):
        ref[pl.ds(c, rows, stride=sub), :] = u[:, c * LANES:(c + 1) * LANES]


def _load_row_tiles(ref, rows):
    sub = ref.shape[0] // rows
    return jnp.concatenate([ref[pl.ds(c, rows, stride=sub), :] for c in range(sub)], axis=1)


def _mm_ln_kernel(alpha, a_ref, w_ref, res_ref, g_ref, b_ref, x_ref, xb_ref, xpk_ref):
    y = alpha * res_ref[...] + _dot(a_ref[...], w_ref[...])
    out = _layer_norm(y, g_ref[...], b_ref[...])
    x_ref[...] = out
    xb_ref[...] = out.astype(BF16)
    _store_row_tiles(xpk_ref, _pack_pairs(out))


def matmul_residual_ln(a, w, res, g, b, alpha, bm=LN_BLOCK_M):
    m, k = a.shape
    d = w.shape[1]
    bm = min(bm, m)
    assert m % bm == 0 and d % (2 * LANES) == 0
    sub = d // 2 // LANES
    return pl.pallas_call(
        functools.partial(_mm_ln_kernel, alpha),
        grid=(m // bm,),
        in_specs=[pl.BlockSpec((bm, k), lambda i: (i, 0)),
                  pl.BlockSpec((k, d), lambda i: (0, 0)),
                  pl.BlockSpec((bm, d), lambda i: (i, 0)),
                  pl.BlockSpec((1, d), lambda i: (0, 0)),
                  pl.BlockSpec((1, d), lambda i: (0, 0))],
        out_specs=[pl.BlockSpec((bm, d), lambda i: (i, 0)),
                   pl.BlockSpec((bm, d), lambda i: (i, 0)),
                   pl.BlockSpec((bm * sub, LANES), lambda i: (i, 0))],
        out_shape=[jax.ShapeDtypeStruct((m, d), F32), jax.ShapeDtypeStruct((m, d), BF16),
                   jax.ShapeDtypeStruct((m * sub, LANES), U32)],
        compiler_params=_params(),
        name="matmul_residual_ln",
    )(a, w, res, g.reshape(1, d), b.reshape(1, d))


def _hgrn_tables(c):
    t = np.arange(c)
    levels = []
    b = c // 2
    while b >= 1:
        levels.append(b)
        b //= 2
    sel = [(t[None, :] <= t[:, None]).astype(np.float32)]
    lvl = np.where(np.eye(c, dtype=bool), 0, -1).astype(np.int32)
    for l, b in enumerate(levels):
        blk = t // b
        odd = (blk % 2) == 1
        ref_row = np.where(odd, blk * b - 1, blk * b + b - 1)
        sel.append((t[None, :] <= ref_row[:, None]).astype(np.float32))
        same = (t[:, None] // (2 * b)) == (t[None, :] // (2 * b))
        lvl = np.where(same & odd[:, None] & (~odd)[None, :], l + 1, lvl)
    sel = np.concatenate(sel, axis=0)
    return np.concatenate([sel, sel, sel], axis=1), lvl


def _hgrn_kernel(n_heads, c, q_ref, f_ref, i_ref, g_ref, s0_ref, lb_ref, gn_ref, sel_ref, lvl_ref,
                 o_ref, sfin_ref, state_ref, cums_ref, qw_ref, kw_ref):
    j = pl.program_id(1)

    @pl.when(j == 0)
    def _():
        state_ref[...] = s0_ref[0]

    n_lvl = qw_ref.shape[0]
    lb = lb_ref[...]
    f = lb + (1.0 - lb) * _sigmoid(f_ref[...])
    k = 1.0 - f
    q = _silu(q_ref[...])
    lg = jnp.log(f)
    hi = lg.astype(BF16)
    rest = lg - hi.astype(F32)
    mid = rest.astype(BF16)
    lo = (rest - mid.astype(F32)).astype(BF16)
    cums_ref[...] = _dot(sel_ref[...], jnp.concatenate([hi, mid, lo], axis=0))
    cum = cums_ref[0:c]
    qw_ref[0] = q.astype(BF16)
    kw_ref[0] = k.astype(BF16)
    for l in range(1, n_lvl):
        w = jnp.exp(-jnp.abs(cum - cums_ref[l * c:(l + 1) * c]))
        qw_ref[l] = (q * w).astype(BF16)
        kw_ref[l] = (k * w).astype(BF16)
    last = cum[c - 1:c, :]
    q_in = (q * jnp.exp(cum)).astype(BF16)
    k_end = (k * jnp.exp(last - cum)).astype(BF16)
    decay = jnp.exp(last)
    lvl = lvl_ref[...]
    for h in range(n_heads):
        hs = slice(h * HEAD_DIM, (h + 1) * HEAD_DIM)
        st = state_ref[h]
        o = _dot_nt(q_in[:, hs], st.astype(BF16))
        scores = jnp.zeros((c, c), F32)
        for l in range(n_lvl):
            scores = jnp.where(lvl == l, _dot_nt(qw_ref[l, :, hs], kw_ref[l, :, hs]), scores)
        v = i_ref[:, hs].astype(BF16)
        o = o + _dot(scores.astype(BF16), v)
        state_ref[h] = st * decay[:, hs] + _dot_tn(v, k_end[:, hs])
        o = o * lax.rsqrt(jnp.mean(o * o, axis=-1, keepdims=True) + RMS_EPS) * gn_ref[...]
        o_ref[:, hs] = (o * _silu(g_ref[:, hs])).astype(o_ref.dtype)

    @pl.when(j == pl.num_programs(1) - 1)
    def _():
        sfin_ref[0] = state_ref[...]


def hgrn_mixer(proj, row0, batch, seq, s0_t, lb, gnorm):
    width = proj.shape[1] // 5
    n_heads = width // HEAD_DIM
    c = HGRN_BLOCK_T if seq % HGRN_BLOCK_T == 0 else seq
    n = seq // c
    assert row0 % c == 0 and c & (c - 1) == 0
    blk0 = row0 // c
    sel, lvl = _hgrn_tables(c)
    n_lvl = sel.shape[0] // c

    def sec(s):
        return pl.BlockSpec((c, width), lambda b, j: (blk0 + b * n + j, s))

    state_spec = pl.BlockSpec((1, n_heads, HEAD_DIM, HEAD_DIM), lambda b, j: (b, 0, 0, 0))
    return pl.pallas_call(
        functools.partial(_hgrn_kernel, n_heads, c),
        grid=(batch, n),
        in_specs=[sec(0), sec(1), sec(2), sec(3), state_spec,
                  pl.BlockSpec((1, width), lambda b, j: (0, 0)),
                  pl.BlockSpec((1, HEAD_DIM), lambda b, j: (0, 0)),
                  pl.BlockSpec(sel.shape, lambda b, j: (0, 0)),
                  pl.BlockSpec(lvl.shape, lambda b, j: (0, 0))],
        out_specs=[pl.BlockSpec((c, width), lambda b, j: (b * n + j, 0)), state_spec],
        out_shape=[jax.ShapeDtypeStruct((batch * seq, width), BF16),
                   jax.ShapeDtypeStruct((batch, n_heads, HEAD_DIM, HEAD_DIM), F32)],
        scratch_shapes=[pltpu.VMEM((n_heads, HEAD_DIM, HEAD_DIM), F32),
                        pltpu.VMEM((n_lvl * c, width), F32),
                        pltpu.VMEM((n_lvl, c, width), BF16),
                        pltpu.VMEM((n_lvl, c, width), BF16)],
        compiler_params=_params(),
        name="hgrn_mixer",
    )(proj, proj, proj, proj, s0_t, lb.reshape(1, width), gnorm.reshape(1, HEAD_DIM),
      jnp.asarray(sel).astype(BF16), jnp.asarray(lvl))


def _pool_kernel(pos0, bt, group, u_ref, prev_ref, w_ref, scale_ref, o_ref, ext_ref):
    j = pl.program_id(1)

    @pl.when(j == 0)
    def _():
        ext_ref[0:POOL_CARRY] = prev_ref[0]

    u = u_ref[...]
    ext_ref[POOL_CARRY:POOL_CARRY + bt] = u
    pos = pos0 + j * bt + lax.broadcasted_iota(I32, (bt, 1), 0)
    for gi, win in enumerate(POOL_WINDOWS):
        cs = slice(gi * group, (gi + 1) * group)
        s = ext_ref[:, cs]
        step = 1
        while step < win:
            s = s + pltpu.roll(s, step, axis=0)
            step *= 2
        cnt = jnp.minimum(pos + 1, win).astype(F32)
        d = s[POOL_CARRY:] / cnt - u[:, cs]
        y = _dot(d.astype(BF16), w_ref[gi]) * scale_ref[:, cs]
        o_ref[:, cs] = y.astype(o_ref.dtype)
    ext_ref[0:POOL_CARRY] = ext_ref[bt:bt + POOL_CARRY]


def pool_mixer(proj, row0, batch, seq, pos0, prev, pool_w, pool_scale):
    width = proj.shape[1] // 5
    group = width // len(POOL_WINDOWS)
    bt = min(POOL_BLOCK_T, seq)
    n = seq // bt
    assert seq % bt == 0 and row0 % bt == 0 and sum(POOL_WINDOWS) // 2 <= POOL_CARRY
    blk0 = row0 // bt
    return pl.pallas_call(
        functools.partial(_pool_kernel, pos0, bt, group),
        grid=(batch, n),
        in_specs=[pl.BlockSpec((bt, width), lambda b, j: (blk0 + b * n + j, 4)),
                  pl.BlockSpec((1, POOL_CARRY, width), lambda b, j: (b, 0, 0)),
                  pl.BlockSpec(pool_w.shape, lambda b, j: (0, 0, 0)),
                  pl.BlockSpec((1, width), lambda b, j: (0, 0))],
        out_specs=pl.BlockSpec((bt, width), lambda b, j: (b * n + j, 0)),
        out_shape=jax.ShapeDtypeStruct((batch * seq, width), BF16),
        scratch_shapes=[pltpu.VMEM((bt + POOL_CARRY, width), F32)],
        compiler_params=_params(),
        name="pool_mixer",
    )(proj, prev, pool_w, pool_scale.reshape(1, width))


def _rel_bias(table, dist):
    idx = jnp.clip(dist, -(CHUNK - 1), MAX_REL) + (CHUNK - 1)
    return table[:, idx].astype(F32)


def _attn_prompt_kernel(bq, hps, scale, q_ref, k0_ref, k1_ref, k2_ref, v0_ref, v1_ref, v2_ref,
                        vals_ref, o_ref, bias_ref):
    b = pl.program_id(1)
    i = pl.program_id(2)

    @pl.when(jnp.logical_and(b == 0, i == 0))
    def _():
        span = vals_ref.shape[-1]
        qi = lax.broadcasted_iota(I32, (bq, 3 * bq), 0)
        kj = lax.broadcasted_iota(I32, (bq, 3 * bq), 1)
        lo = (qi // CHUNK) * CHUNK
        in_band = jnp.logical_and(kj >= lo, kj < lo + WINDOW + CHUNK)
        for hh in range(hps):
            rows = jnp.broadcast_to(vals_ref[0, hh:hh + 1, :], (bq, span))
            toep = pltpu.roll(rows, span - (bq - 1), 1, stride=1, stride_axis=0)
            bias_ref[hh] = jnp.where(in_band, toep[:, :3 * bq], NEG)

    ks = (k0_ref, k1_ref, k2_ref)
    vs = (v0_ref, v1_ref, v2_ref)
    for hh in range(hps):
        hs = slice(hh * HEAD_DIM, (hh + 1) * HEAD_DIM)
        q = q_ref[:, hs]
        s = []
        for p in range(3):
            sp = _dot_nt(q, ks[p][:, hs]) * scale + bias_ref[hh, :, p * bq:(p + 1) * bq]
            if p < 2:
                sp = jnp.where(i >= 2 - p, sp, NEG)
            s.append(sp)
        m = jnp.maximum(jnp.maximum(jnp.max(s[0], axis=-1, keepdims=True),
                                    jnp.max(s[1], axis=-1, keepdims=True)),
                        jnp.max(s[2], axis=-1, keepdims=True))
        e = [jnp.exp(sp - m) for sp in s]
        inv = 1.0 / (jnp.sum(e[0], axis=-1, keepdims=True) + jnp.sum(e[1], axis=-1, keepdims=True)
                     + jnp.sum(e[2], axis=-1, keepdims=True))
        o = _dot((e[0] * inv).astype(BF16), vs[0][:, hs])
        o = o + _dot((e[1] * inv).astype(BF16), vs[1][:, hs])
        o = o + _dot((e[2] * inv).astype(BF16), vs[2][:, hs])
        o_ref[:, hs] = o.astype(o_ref.dtype)


def band_attention_prompt(qkv, batch, seq, rel_bias):
    d = qkv.shape[1] // 3
    n_heads = d // HEAD_DIM
    bq = ATTN_BLOCK_Q
    hps = min(ATTN_HEADS_PER_STEP, n_heads)
    assert seq % bq == 0 and WINDOW == 2 * bq and bq % CHUNK == 0 and n_heads % hps == 0
    nq = seq // bq
    ng = n_heads // hps
    offs = jnp.arange(4 * bq) - (bq - 1)
    vals = _rel_bias(rel_bias, WINDOW - offs).reshape(ng, hps, 4 * bq)

    def kv(sec, p):
        return pl.BlockSpec((bq, hps * HEAD_DIM),
                            lambda g, b, i: (b * nq + jnp.maximum(i - 2 + p, 0), sec * ng + g))

    return pl.pallas_call(
        functools.partial(_attn_prompt_kernel, bq, hps, HEAD_DIM ** -0.5),
        grid=(ng, batch, nq),
        in_specs=[pl.BlockSpec((bq, hps * HEAD_DIM), lambda g, b, i: (b * nq + i, g)),
                  kv(1, 0), kv(1, 1), kv(1, 2), kv(2, 0), kv(2, 1), kv(2, 2),
                  pl.BlockSpec((1, hps, 4 * bq), lambda g, b, i: (g, 0, 0))],
        out_specs=pl.BlockSpec((bq, hps * HEAD_DIM), lambda g, b, i: (b * nq + i, g)),
        out_shape=jax.ShapeDtypeStruct((batch * seq, d), BF16),
        scratch_shapes=[pltpu.VMEM((hps, bq, 3 * bq), F32)],
        compiler_params=_params(),
        name="band_attention_prompt",
    )(qkv, qkv, qkv, qkv, qkv, qkv, qkv, vals)


def _attn_sample_kernel(n_heads, lc, scale, qkv_ref, ck_ref, cv_ref, bias_ref, o_ref):
    d = n_heads * HEAD_DIM
    for h in range(n_heads):
        hs = slice(h * HEAD_DIM, (h + 1) * HEAD_DIM)
        q = qkv_ref[:, h * HEAD_DIM:(h + 1) * HEAD_DIM].astype(BF16)
        kn = qkv_ref[:, d + h * HEAD_DIM:d + (h + 1) * HEAD_DIM].astype(BF16)
        vn = qkv_ref[:, 2 * d + h * HEAD_DIM:2 * d + (h + 1) * HEAD_DIM].astype(BF16)
        kc = ck_ref[0, 0, :, h, :].astype(BF16)
        vc = cv_ref[0, 0, :, h, :].astype(BF16)
        sc = _dot_nt(q, kc) * scale + bias_ref[h, :, 0:lc]
        sn = _dot_nt(q, kn) * scale + bias_ref[h, :, lc:]
        m = jnp.maximum(jnp.max(sc, axis=-1, keepdims=True), jnp.max(sn, axis=-1, keepdims=True))
        ec = jnp.exp(sc - m)
        en = jnp.exp(sn - m)
        inv = 1.0 / (jnp.sum(ec, axis=-1, keepdims=True) + jnp.sum(en, axis=-1, keepdims=True))
        o = _dot((ec * inv).astype(BF16), vc) + _dot((en * inv).astype(BF16), vn)
        o_ref[:, hs] = o.astype(o_ref.dtype)


def band_attention_sample(qkv, row0, batch, seq, cache_k, cache_v, layer, rel_bias):
    d = qkv.shape[1] // 3
    n_heads = d // HEAD_DIM
    lc = cache_k.shape[2]
    assert row0 % seq == 0 and lc % 128 == 0
    blk0 = row0 // seq
    dist = jnp.arange(seq)[:, None] + lc - jnp.arange(lc + seq)[None, :]
    bias = _rel_bias(rel_bias, dist)
    cache_spec = pl.BlockSpec((1, 1, lc, n_heads, HEAD_DIM), lambda b: (layer, b, 0, 0, 0))
    return pl.pallas_call(
        functools.partial(_attn_sample_kernel, n_heads, lc, HEAD_DIM ** -0.5),
        grid=(batch,),
        in_specs=[pl.BlockSpec((seq, 3 * d), lambda b: (blk0 + b, 0)),
                  cache_spec, cache_spec,
                  pl.BlockSpec(bias.shape, lambda b: (0, 0, 0))],
        out_specs=pl.BlockSpec((seq, d), lambda b: (b, 0)),
        out_shape=jax.ShapeDtypeStruct((batch * seq, d), BF16),
        compiler_params=_params(),
        name="band_attention_sample",
    )(qkv, cache_k, cache_v, bias)


def _first_max(vals, idx, sentinel):
    m = jnp.max(vals, axis=0, keepdims=True)
    first = jnp.min(jnp.where(vals == m, idx, sentinel), axis=0, keepdims=True)
    return m, first


def _stack_rows(rows, lanes):
    ridx = lax.broadcasted_iota(I32, (len(rows), lanes), 0)
    out = jnp.zeros((len(rows), lanes), rows[0].dtype)
    for r, row in enumerate(rows):
        out = jnp.where(ridx == r, row, out)
    return out


def _router_kernel(n_experts, bt, x_ref, wt_ref, bias_ref, e_ref, gate_ref, rank_ref, cnt_ref,
                   carry_ref):
    i = pl.program_id(0)

    @pl.when(i == 0)
    def _():
        carry_ref[...] = jnp.zeros_like(carry_ref)

    per = n_experts // N_GROUPS
    logits = _dot_nt(wt_ref[...], x_ref[...])
    scores = _sigmoid(logits)
    sel = scores + bias_ref[:, 0:1]

    sub = lax.broadcasted_iota(I32, (per, bt), 0)
    grp_rows = []
    for g in range(N_GROUPS):
        blk = sel[g * per:(g + 1) * per]
        m1, i1 = _first_max(blk, sub, per)
        m2 = jnp.max(jnp.where(sub == i1, -jnp.inf, blk), axis=0, keepdims=True)
        grp_rows.append(m1 + m2)
    grp = _stack_rows(grp_rows, bt)
    gidx = lax.broadcasted_iota(I32, (N_GROUPS, bt), 0)
    keep = jnp.zeros((N_GROUPS, bt), F32)
    for _ in range(TOPK_GROUPS):
        _, first = _first_max(grp, gidx, N_GROUPS)
        hit = gidx == first
        keep = jnp.where(hit, 1.0, keep)
        grp = jnp.where(hit, -jnp.inf, grp)
    masked = jnp.concatenate(
        [jnp.where(keep[g:g + 1] > 0.0, sel[g * per:(g + 1) * per], -jnp.inf)
         for g in range(N_GROUPS)], axis=0)

    eidx = lax.broadcasted_iota(I32, (n_experts, bt), 0)
    chosen = jnp.zeros((n_experts, bt), F32)
    picks = []
    gates = []
    for _ in range(TOP_K):
        _, first = _first_max(masked, eidx, n_experts)
        hit = eidx == first
        picks.append(first)
        gates.append(jnp.sum(jnp.where(hit, scores, 0.0), axis=0, keepdims=True))
        chosen = jnp.where(hit, 1.0, chosen)
        masked = jnp.where(hit, -jnp.inf, masked)
    gate = _stack_rows(gates, bt)
    gate = gate / jnp.sum(gate, axis=0, keepdims=True) * ROUTE_SCALE

    earlier = (lax.broadcasted_iota(I32, (bt, bt), 0) < lax.broadcasted_iota(I32, (bt, bt), 1))
    before = _dot(chosen.astype(BF16), jnp.where(earlier, 1.0, 0.0).astype(BF16)) + carry_ref[:, 0:1]
    ranks = [jnp.sum(jnp.where(eidx == p, before, 0.0), axis=0, keepdims=True) for p in picks]

    e_ref[...] = _stack_rows(picks, bt)
    gate_ref[...] = gate
    rank_ref[...] = _stack_rows(ranks, bt).astype(I32)
    carry_ref[...] = carry_ref[...] + jnp.sum(chosen, axis=1, keepdims=True)
    cnt_ref[...] = carry_ref[...]


def moe_router(x, w_router, router_bias, bt=ROUTER_BLOCK_T):
    n, d = x.shape
    n_experts = w_router.shape[1]
    bt = min(bt, n)
    assert n % bt == 0 and n_experts % N_GROUPS == 0
    lanes = 128
    tok = pl.BlockSpec((TOP_K, bt), lambda i: (0, i))
    e, gate, rank, cnt = pl.pallas_call(
        functools.partial(_router_kernel, n_experts, bt),
        grid=(n // bt,),
        in_specs=[pl.BlockSpec((bt, d), lambda i: (i, 0)),
                  pl.BlockSpec((n_experts, d), lambda i: (0, 0)),
                  pl.BlockSpec((n_experts, lanes), lambda i: (0, 0))],
        out_specs=[tok, tok, tok, pl.BlockSpec((n_experts, lanes), lambda i: (0, 0))],
        out_shape=[jax.ShapeDtypeStruct((TOP_K, n), I32), jax.ShapeDtypeStruct((TOP_K, n), F32),
                   jax.ShapeDtypeStruct((TOP_K, n), I32),
                   jax.ShapeDtypeStruct((n_experts, lanes), F32)],
        scratch_shapes=[pltpu.VMEM((n_experts, lanes), F32)],
        compiler_params=_params(),
        name="moe_router",
    )(x, w_router.T.astype(BF16),
      jnp.broadcast_to(router_bias.astype(F32)[:, None], (n_experts, lanes)))
    return e, gate, rank, cnt[:, 0].astype(I32)


def _dispatch_kernel(bt, n_experts, fill_start_ref, fill_len_ref, dest_ref, x_ref, xs_ref,
                     zero_ref, sem, fill_sem):
    i = pl.program_id(0)
    sub = xs_ref.shape[1]

    @pl.when(i == 0)
    def _():
        zero_ref[...] = jnp.zeros_like(zero_ref)

        def fill(e, carry):
            base = fill_start_ref[e]

            def zero_copy(r):
                return pltpu.make_async_copy(zero_ref, xs_ref.at[base + r], fill_sem)

            def start(r, c):
                zero_copy(r).start()
                return c

            def wait(r, c):
                zero_copy(r).wait()
                return c

            lax.fori_loop(0, fill_len_ref[e], start, 0)
            lax.fori_loop(0, fill_len_ref[e], wait, 0)
            return carry

        lax.fori_loop(0, n_experts, fill, 0)

    for t in range(bt):
        for k in range(TOP_K):
            pltpu.make_async_copy(x_ref.at[pl.ds(t * sub, sub)], xs_ref.at[dest_ref[t * TOP_K + k]],
                                  sem).start(priority=k % 2)
    for k in range(TOP_K):
        pltpu.make_async_copy(x_ref, x_ref, sem).wait()


def moe_dispatch(xpk, dest_flat, fill_start, fill_len, rows, bt=DISPATCH_BLOCK_T):
    n = dest_flat.shape[0] // TOP_K
    sub = xpk.shape[0] // n
    bt = min(bt, n)
    assert n % bt == 0
    n_experts = fill_start.shape[0]
    return pl.pallas_call(
        functools.partial(_dispatch_kernel, bt, n_experts),
        grid_spec=pltpu.PrefetchScalarGridSpec(
            num_scalar_prefetch=2,
            grid=(n // bt,),
            in_specs=[pl.BlockSpec((bt * TOP_K,), lambda i, *_: (i,), memory_space=pltpu.SMEM),
                      pl.BlockSpec((bt * sub, LANES), lambda i, *_: (i, 0))],
            out_specs=pl.BlockSpec(memory_space=pl.ANY),
            scratch_shapes=[pltpu.VMEM((sub, LANES), U32),
                            pltpu.SemaphoreType.DMA, pltpu.SemaphoreType.DMA]),
        out_shape=jax.ShapeDtypeStruct((rows, sub, LANES), U32),
        compiler_params=_params(),
        name="moe_dispatch",
    )(fill_start, fill_len, dest_flat, xpk)


def _expert_kernel(blk, tile_e_ref, n_used_ref, xs_ref, wgu_ref, wdn_ref, ys_ref, wgu_b, wdn_b):
    i = pl.program_id(0)
    de = wdn_b.shape[0]
    changed = jnp.logical_or(i == 0, tile_e_ref[i] != tile_e_ref[jnp.maximum(i - 1, 0)])

    @pl.when(changed)
    def _():
        wgu_b[...] = wgu_ref[0, 0].astype(BF16)
        wdn_b[...] = wdn_ref[0, 0].astype(BF16)

    @pl.when(i < n_used_ref[0])
    def _():
        lo, hi = _unpack_pairs(_load_row_tiles(xs_ref, blk))
        x = jnp.concatenate([lo.astype(BF16), hi.astype(BF16)], axis=1)
        gu = _dot(x, wgu_b[...])
        a = _silu(gu[:, :de]) * gu[:, de:]
        _store_row_tiles(ys_ref, _pack_pairs(_dot(a.astype(BF16), wdn_b[...])))

    @pl.when(i >= n_used_ref[0])
    def _():
        ys_ref[...] = jnp.zeros_like(ys_ref)


def moe_experts(xs, tile_e, n_used, w_gu, w_dn, layer, blk=EXPERT_BLOCK):
    sub = w_gu.shape[2] // 2 // LANES
    d = 2 * sub * LANES
    rows = xs.shape[0] // sub
    n_tiles = rows // blk
    de = w_dn.shape[2]
    return pl.pallas_call(
        functools.partial(_expert_kernel, blk),
        grid_spec=pltpu.PrefetchScalarGridSpec(
            num_scalar_prefetch=2,
            grid=(n_tiles,),
            in_specs=[pl.BlockSpec((blk * sub, LANES),
                                   lambda i, te, nu: (jnp.minimum(i, nu[0] - 1), 0)),
                      pl.BlockSpec((1, 1, d, 2 * de), lambda i, te, nu: (layer, te[i], 0, 0)),
                      pl.BlockSpec((1, 1, de, d), lambda i, te, nu: (layer, te[i], 0, 0))],
            out_specs=pl.BlockSpec((blk * sub, LANES), lambda i, te, nu: (i, 0)),
            scratch_shapes=[pltpu.VMEM((d, 2 * de), BF16), pltpu.VMEM((de, d), BF16)]),
        out_shape=jax.ShapeDtypeStruct((rows * sub, LANES), U32),
        compiler_params=_params(),
        name="moe_experts",
    )(tile_e, n_used, xs, w_gu, w_dn)


def _combine_kernel(alpha, bt, dest_ref, dest_next_ref, ys_ref, gate_ref, x_ref, xb_ref, wsgu_ref,
                    wsdn_ref, g_ref, b_ref, xo_ref, xbo_ref, buf_a, buf_b, sem_a, sem_b):
    i = pl.program_id(0)
    de = wsdn_ref.shape[0]
    sub = ys_ref.shape[1]
    per_tile = bt * TOP_K

    def row_copy(idx_ref, base, buf, sem, t, k):
        return pltpu.make_async_copy(ys_ref.at[idx_ref[base + t * TOP_K + k]],
                                     buf.at[k, pl.ds(t * sub, sub)], sem)

    def issue_unrolled(idx_ref, base, buf, sem):
        for t in range(bt):
            for k in range(TOP_K):
                row_copy(idx_ref, base, buf, sem, t, k).start(priority=k % 2)

    def drain(buf, other, sem):
        pltpu.make_async_copy(other, buf, sem).wait()

    def compute(r0, buf):
        rows = pl.ds(r0, bt)
        sgu = _dot(xb_ref[rows, :], wsgu_ref[...])
        shared = _dot((_silu(sgu[:, :de]) * sgu[:, de:]).astype(BF16), wsdn_ref[...])
        routed_lo, routed_hi = None, None
        for k in range(TOP_K):
            gate = gate_ref[rows, k:k + 1]
            lo, hi = _unpack_pairs(_load_row_tiles(buf.at[k], bt))
            routed_lo = gate * lo if k == 0 else routed_lo + gate * lo
            routed_hi = gate * hi if k == 0 else routed_hi + gate * hi
        routed = jnp.concatenate([routed_lo, routed_hi], axis=1)
        out = _layer_norm(alpha * x_ref[rows, :] + (routed + shared), g_ref[...], b_ref[...])
        xo_ref[rows, :] = out
        xbo_ref[rows, :] = out.astype(BF16)

    @pl.when(i == 0)
    def _():
        def body(t, carry):
            for k in range(TOP_K):
                row_copy(dest_ref, 0, buf_a, sem_a, t, k).start()
            return carry
        lax.fori_loop(0, bt, body, 0)

    drain(buf_a, buf_b, sem_a)
    issue_unrolled(dest_ref, per_tile, buf_b, sem_b)
    compute(0, buf_a)
    drain(buf_b, buf_a, sem_b)
    issue_unrolled(dest_next_ref, 0, buf_a, sem_a)
    compute(bt, buf_b)

    @pl.when(i == pl.num_programs(0) - 1)
    def _():
        drain(buf_a, buf_b, sem_a)


def moe_combine_ln(ys, dest_flat, gate_tm, x, xb, ws_gu, ws_dn, g, b, alpha, bt=COMBINE_BLOCK_T):
    n, d = x.shape
    bt = min(bt, n // 2)
    assert n % (2 * bt) == 0
    n_tiles = n // bt
    de = ws_dn.shape[0]
    sub = ys.shape[1]
    row = pl.BlockSpec((2 * bt, d), lambda i: (i, 0))
    vec = pl.BlockSpec((1, d), lambda i: (0, 0))
    return pl.pallas_call(
        functools.partial(_combine_kernel, alpha, bt),
        grid=(n_tiles // 2,),
        in_specs=[pl.BlockSpec((2 * bt * TOP_K,), lambda i: (i,), memory_space=pltpu.SMEM),
                  pl.BlockSpec((bt * TOP_K,), lambda i: (jnp.minimum(2 * i + 2, n_tiles - 1),),
                               memory_space=pltpu.SMEM),
                  pl.BlockSpec(memory_space=pl.ANY),
                  pl.BlockSpec((2 * bt, TOP_K), lambda i: (i, 0)),
                  row, row,
                  pl.BlockSpec((d, 2 * de), lambda i: (0, 0)),
                  pl.BlockSpec((de, d), lambda i: (0, 0)),
                  vec, vec],
        out_specs=[row, row],
        out_shape=[jax.ShapeDtypeStruct((n, d), F32), jax.ShapeDtypeStruct((n, d), BF16)],
        scratch_shapes=[pltpu.VMEM((TOP_K, bt * sub, LANES), U32),
                        pltpu.VMEM((TOP_K, bt * sub, LANES), U32),
                        pltpu.SemaphoreType.DMA, pltpu.SemaphoreType.DMA],
        compiler_params=_params(),
        name="moe_combine_ln",
    )(dest_flat, dest_flat, ys, gate_tm, x, xb, ws_gu, ws_dn, g.reshape(1, d), b.reshape(1, d))


def moe_ffn_ln(x, xb, xpk, w_router, router_bias, w_gu, w_dn, layer, ws_gu, ws_dn, g, b, alpha):
    n, d = x.shape
    n_experts = w_router.shape[1]
    blk = EXPERT_BLOCK
    top_e, gate, rank, counts = moe_router(xb, w_router, router_bias)
    padded = (counts + blk - 1) // blk * blk
    pend = jnp.cumsum(padded)
    pstart = pend - padded
    hot = top_e[:, :, None] == jnp.arange(n_experts, dtype=I32)
    dest = jnp.sum(jnp.where(hot, pstart, 0), axis=-1) + rank
    dest_flat = dest.T.reshape(-1)
    n_tiles = (n * TOP_K + n_experts * (blk - 1) + blk - 1) // blk
    tile_row0 = jnp.arange(n_tiles, dtype=I32) * blk
    tile_e = jnp.minimum(jnp.sum((pend[None, :] <= tile_row0[:, None]).astype(I32), axis=1),
                         n_experts - 1)
    n_used = (pend[-1:] // blk).astype(I32)
    sub = d // 2 // LANES
    rows = n_tiles * blk
    xs = moe_dispatch(xpk, dest_flat, (pstart + counts).astype(I32), (padded - counts).astype(I32),
                      rows)
    ys = moe_experts(xs.reshape(rows * sub, LANES), tile_e, n_used, w_gu, w_dn, layer, blk)
    ys = ys.reshape(rows, sub, LANES)
    return moe_combine_ln(ys, dest_flat, gate.T, x, xb, ws_gu, ws_dn, g, b, alpha)


def kernel(x_prompt, x_sample, state_hgrn, state_pool, cache_k, cache_v, ln_g, ln_b, w_in_even,
           lb_even, gnorm_even, pool_w_even, pool_scale_even, w_out_even, w_qkv_odd, rel_bias_odd,
           w_o_odd, w_router, router_bias, w_gu, w_dn, ws_gu, ws_dn):
    batch, seq, d = x_prompt.shape
    dec_batch, dec_seq, _ = x_sample.shape
    depth = ln_g.shape[0]
    n_p = batch * seq
    n_s = dec_batch * dec_seq
    width = w_in_even.shape[2] // 5
    n_heads_a = width // HEAD_DIM
    n_heads_c = d // HEAD_DIM
    alpha = (2 * depth) ** 0.25
    assert dec_seq >= POOL_PREV and seq >= max(POOL_PREV, WINDOW)

    x = jnp.concatenate([x_prompt.reshape(n_p, d), x_sample.reshape(n_s, d)], axis=0)
    xb = x.astype(BF16)

    p = jax.nn.softmax(lb_even.astype(F32), axis=0)
    lbs = jnp.cumsum(p, axis=0) - p[0]

    hgrn_p, pool_p, k_p, v_p, hgrn_s, pool_s, k_s, v_s = [], [], [], [], [], [], [], []
    for l in range(depth):
        j = l // 2
        if l % 2 == 0:
            proj = matmul(xb, w_in_even[j].astype(BF16))
            zero_state = jnp.zeros((batch, n_heads_a, HEAD_DIM, HEAD_DIM), F32)
            oa_p, sp = hgrn_mixer(proj, 0, batch, seq, zero_state, lbs[j], gnorm_even[j])
            oa_s, ss = hgrn_mixer(proj, n_p, dec_batch, dec_seq,
                                  jnp.swapaxes(state_hgrn[j].astype(F32), -1, -2), lbs[j],
                                  gnorm_even[j])
            pw = pool_w_even[j].astype(BF16)
            ob_p = pool_mixer(proj, 0, batch, seq, 0, jnp.zeros((batch, POOL_CARRY, width), F32),
                              pw, pool_scale_even[j])
            prev_s = jnp.pad(state_pool[j].astype(F32),
                             ((0, 0), (POOL_CARRY - POOL_PREV, 0), (0, 0)))
            ob_s = pool_mixer(proj, n_p, dec_batch, dec_seq, PAST_LEN, prev_s, pw,
                              pool_scale_even[j])
            mixed = jnp.concatenate([jnp.concatenate([oa_p, oa_s], axis=0),
                                     jnp.concatenate([ob_p, ob_s], axis=0)], axis=1)
            w_mix = w_out_even[j]
            hgrn_p.append(jnp.swapaxes(sp, -1, -2))
            hgrn_s.append(jnp.swapaxes(ss, -1, -2))
            pool_p.append(jnp.stack([proj[(b + 1) * seq - POOL_PREV:(b + 1) * seq, 4 * width:]
                                     for b in range(batch)]))
            pool_s.append(proj[n_p:].reshape(dec_batch, dec_seq, 5 * width)
                          [:, dec_seq - POOL_PREV:, 4 * width:])
        else:
            qkv = matmul(xb, w_qkv_odd[j].astype(BF16), out_dtype=BF16)
            o_p = band_attention_prompt(qkv, batch, seq, rel_bias_odd[j])
            o_s = band_attention_sample(qkv, n_p, dec_batch, dec_seq, cache_k, cache_v, j,
                                        rel_bias_odd[j])
            mixed = jnp.concatenate([o_p, o_s], axis=0)
            w_mix = w_o_odd[j]
            keep = min(WINDOW, seq)
            tail_p = jnp.stack([qkv[(b + 1) * seq - keep:(b + 1) * seq]
                                for b in range(batch)]).astype(F32)
            new_s = qkv[n_p:].reshape(dec_batch, dec_seq, 3 * d).astype(F32)
            k_p.append(tail_p[:, :, d:2 * d].reshape(batch, keep, n_heads_c, HEAD_DIM))
            v_p.append(tail_p[:, :, 2 * d:].reshape(batch, keep, n_heads_c, HEAD_DIM))
            k_s.append(new_s[:, :, d:2 * d].reshape(dec_batch, dec_seq, n_heads_c, HEAD_DIM))
            v_s.append(new_s[:, :, 2 * d:].reshape(dec_batch, dec_seq, n_heads_c, HEAD_DIM))
        x, xb, xpk = matmul_residual_ln(mixed, w_mix.astype(BF16), x, ln_g[l, 0], ln_b[l, 0], alpha)
        x, xb = moe_ffn_ln(x, xb, xpk, w_router[l], router_bias[l], w_gu, w_dn, l,
                           ws_gu[l].astype(BF16), ws_dn[l].astype(BF16), ln_g[l, 1], ln_b[l, 1],
                           alpha)

    y_p = x[:n_p].reshape(batch, seq, d)
    y_s = x[n_p:].reshape(dec_batch, dec_seq, d)
    return (y_p, y_s, jnp.stack(hgrn_p), jnp.stack(pool_p), jnp.stack(k_p), jnp.stack(v_p),
            jnp.stack(hgrn_s), jnp.stack(pool_s), jnp.stack(k_s), jnp.stack(v_s))
```

```python
import functools

import numpy as np
import jax
import jax.numpy as jnp
from jax import lax
from jax.experimental import pallas as pl
from jax.experimental.pallas import tpu as pltpu

F32 = jnp.float32
BF16 = jnp.bfloat16
I32 = jnp.int32
U32 = jnp.uint32

CHUNK = 64
HEAD_DIM = 128
LANES = 128
POOL_WINDOWS = (2, 4, 8, 16)
POOL_PREV = max(POOL_WINDOWS) - 1
POOL_CARRY = POOL_PREV + 1
LEFT_CHUNKS = 8
WINDOW = LEFT_CHUNKS * CHUNK
MAX_REL = 128
N_GROUPS = 8
TOPK_GROUPS = 4
TOP_K = 8
ROUTE_SCALE = 2.5
LN_EPS = 1e-5
RMS_EPS = 1e-6
PAST_LEN = 1024
NEG = -1e30

VMEM_LIMIT_BYTES = 56 * 1024 * 1024
MM_BLOCK_M = 1024
MM_BLOCK_N = 1024
LN_BLOCK_M = 512
HGRN_BLOCK_T = 128
HGRN_HEADS_INTERLEAVED = 4
ATTN_BLOCK_Q = 256
ATTN_HEADS_PER_STEP = 4
ATTN_HEADS_INTERLEAVED = 4
POOL_BLOCK_T = 512
ROUTER_BLOCK_T = 512
DISPATCH_BLOCK_T = 128
EXPERT_BLOCK = 512
COMBINE_BLOCK_T = 128


def _params(**kw):
    return pltpu.CompilerParams(vmem_limit_bytes=VMEM_LIMIT_BYTES, **kw)


def _sigmoid(x):
    return 1.0 / (1.0 + jnp.exp(-x))


def _silu(x):
    return x * _sigmoid(x)


def _dot(a, b):
    return jnp.dot(a, b, preferred_element_type=F32)


def _dot_nt(a, b):
    return lax.dot_general(a, b, (((1,), (1,)), ((), ())), preferred_element_type=F32)


def _dot_tn(a, b):
    return lax.dot_general(a, b, (((0,), (0,)), ((), ())), preferred_element_type=F32)


def _layer_norm(y, g, b):
    mu = jnp.mean(y, axis=-1, keepdims=True)
    yc = y - mu
    var = jnp.mean(yc * yc, axis=-1, keepdims=True)
    return yc * lax.rsqrt(var + LN_EPS) * g + b


def _mm_kernel(x_ref, w_ref, o_ref):
    o_ref[...] = _dot(x_ref[...], w_ref[...]).astype(o_ref.dtype)


def matmul(x, w, out_dtype=F32, bm=MM_BLOCK_M, bn=MM_BLOCK_N):
    m, k = x.shape
    n = w.shape[1]
    bm, bn = min(bm, m), min(bn, n)
    assert m % bm == 0 and n % bn == 0
    return pl.pallas_call(
        _mm_kernel,
        grid=(m // bm, n // bn),
        in_specs=[pl.BlockSpec((bm, k), lambda i, j: (i, 0)),
                  pl.BlockSpec((k, bn), lambda i, j: (0, j))],
        out_specs=pl.BlockSpec((bm, bn), lambda i, j: (i, j)),
        out_shape=jax.ShapeDtypeStruct((m, n), out_dtype),
        compiler_params=_params(),
        name="matmul",
    )(x, w)


def _pack_pairs(x):
    n = x.shape[1] // 2
    lo = lax.bitcast_convert_type(x[:, :n].astype(jnp.bfloat16).astype(F32), U32)
    hi = lax.bitcast_convert_type(x[:, n:].astype(jnp.bfloat16).astype(F32), U32)
    return (lo >> 16) | hi


def _unpack_pairs(u):
    lo = lax.bitcast_convert_type(u << 16, F32)
    hi = lax.bitcast_convert_type(u & jnp.uint32(0xFFFF0000), F32)
    return lo, hi


def _store_row_tiles(ref, u):
    rows = u.shape[0]
    sub = ref.shape[0] // rows
    for c in range(sub):
        ref[pl.ds(c, rows, stride=sub), :] = u[:, c * LANES:(c + 1) * LANES]


def _load_row_tiles(ref, rows):
    sub = ref.shape[0] // rows
    return jnp.concatenate([ref[pl.ds(c, rows, stride=sub), :] for c in range(sub)], axis=1)


def _mm_ln_kernel(alpha, a_ref, w_ref, res_ref, g_ref, b_ref, x_ref, xb_ref, xpk_ref):
    y = alpha * res_ref[...] + _dot(a_ref[...], w_ref[...])
    out = _layer_norm(y, g_ref[...], b_ref[...])
    x_ref[...] = out
    xb_ref[...] = out.astype(BF16)
    _store_row_tiles(xpk_ref, _pack_pairs(out))


def matmul_residual_ln(a, w, res, g, b, alpha, bm=LN_BLOCK_M):
    m, k = a.shape
    d = w.shape[1]
    bm = min(bm, m)
    assert m % bm == 0 and d % (2 * LANES) == 0
    sub = d // 2 // LANES
    return pl.pallas_call(
        functools.partial(_mm_ln_kernel, alpha),
        grid=(m // bm,),
        in_specs=[pl.BlockSpec((bm, k), lambda i: (i, 0)),
                  pl.BlockSpec((k, d), lambda i: (0, 0)),
                  pl.BlockSpec((bm, d), lambda i: (i, 0)),
                  pl.BlockSpec((1, d), lambda i: (0, 0)),
                  pl.BlockSpec((1, d), lambda i: (0, 0))],
        out_specs=[pl.BlockSpec((bm, d), lambda i: (i, 0)),
                   pl.BlockSpec((bm, d), lambda i: (i, 0)),
                   pl.BlockSpec((bm * sub, LANES), lambda i: (i, 0))],
        out_shape=[jax.ShapeDtypeStruct((m, d), F32), jax.ShapeDtypeStruct((m, d), BF16),
                   jax.ShapeDtypeStruct((m * sub, LANES), U32)],
        compiler_params=_params(),
        name="matmul_residual_ln",
    )(a, w, res, g.reshape(1, d), b.reshape(1, d))


def _hgrn_tables(c):
    t = np.arange(c)
    levels = []
    b = c // 2
    while b >= 1:
        levels.append(b)
        b //= 2
    sel = [(t[None, :] <= t[:, None]).astype(np.float32)]
    lvl = np.where(np.eye(c, dtype=bool), 0, -1).astype(np.int32)
    for l, b in enumerate(levels):
        blk = t // b
        odd = (blk % 2) == 1
        ref_row = np.where(odd, blk * b - 1, blk * b + b - 1)
        sel.append((t[None, :] <= ref_row[:, None]).astype(np.float32))
        same = (t[:, None] // (2 * b)) == (t[None, :] // (2 * b))
        lvl = np.where(same & odd[:, None] & (~odd)[None, :], l + 1, lvl)
    sel = np.concatenate(sel, axis=0)
    return np.concatenate([sel, sel, sel], axis=1), lvl


def _hgrn_kernel(n_heads, c, q_ref, f_ref, i_ref, g_ref, s0_ref, lb_ref, gn_ref, sel_ref, lvl_ref,
                 o_ref, sfin_ref, state_ref, cums_ref, qw_ref, kw_ref):
    j = pl.program_id(1)

    @pl.when(j == 0)
    def _():
        state_ref[...] = s0_ref[0]

    n_lvl = qw_ref.shape[0]
    lb = lb_ref[...]
    f = lb + (1.0 - lb) * _sigmoid(f_ref[...])
    k = 1.0 - f
    q = _silu(q_ref[...])
    lg = jnp.log(f)
    hi = lg.astype(BF16)
    rest = lg - hi.astype(F32)
    mid = rest.astype(BF16)
    lo = (rest - mid.astype(F32)).astype(BF16)
    cums_ref[...] = _dot(sel_ref[...], jnp.concatenate([hi, mid, lo], axis=0))
    cum = cums_ref[0:c]
    qw_ref[0] = q.astype(BF16)
    kw_ref[0] = k.astype(BF16)
    for l in range(1, n_lvl):
        w = jnp.exp(-jnp.abs(cum - cums_ref[l * c:(l + 1) * c]))
        qw_ref[l] = (q * w).astype(BF16)
        kw_ref[l] = (k * w).astype(BF16)
    last = cum[c - 1:c, :]
    q_in = (q * jnp.exp(cum)).astype(BF16)
    k_end = (k * jnp.exp(last - cum)).astype(BF16)
    decay = jnp.exp(last)
    lvl = lvl_ref[...]
    for h0 in range(0, n_heads, HGRN_HEADS_INTERLEAVED):
        heads = list(range(h0, min(h0 + HGRN_HEADS_INTERLEAVED, n_heads)))
        cols = [slice(h * HEAD_DIM, (h + 1) * HEAD_DIM) for h in heads]
        st = [state_ref[h] for h in heads]
        o = [_dot_nt(q_in[:, hs], s.astype(BF16)) for hs, s in zip(cols, st)]
        scores = [jnp.zeros((c, c), F32) for _ in heads]
        for l in range(n_lvl):
            scores = [jnp.where(lvl == l, _dot_nt(qw_ref[l, :, hs], kw_ref[l, :, hs]), sc)
                      for hs, sc in zip(cols, scores)]
        v = [i_ref[:, hs].astype(BF16) for hs in cols]
        o = [oh + _dot(sc.astype(BF16), vh) for oh, sc, vh in zip(o, scores, v)]
        for h, hs, s, vh in zip(heads, cols, st, v):
            state_ref[h] = s * decay[:, hs] + _dot_tn(vh, k_end[:, hs])
        o = [oh * lax.rsqrt(jnp.mean(oh * oh, axis=-1, keepdims=True) + RMS_EPS) * gn_ref[...]
             for oh in o]
        for hs, oh in zip(cols, o):
            o_ref[:, hs] = (oh * _silu(g_ref[:, hs])).astype(o_ref.dtype)

    @pl.when(j == pl.num_programs(1) - 1)
    def _():
        sfin_ref[0] = state_ref[...]


def _fill_in_place(body, n_inputs, base):
    if base is None:
        return body
    return lambda *refs: body(*refs[:n_inputs], *refs[n_inputs + 1:])


def hgrn_mixer(proj, row0, batch, seq, s0_t, lb, gnorm, base=None):
    width = proj.shape[1] // 5
    n_heads = width // HEAD_DIM
    c = HGRN_BLOCK_T if seq % HGRN_BLOCK_T == 0 else seq
    n = seq // c
    assert row0 % c == 0 and c & (c - 1) == 0
    blk0 = row0 // c
    sel, lvl = _hgrn_tables(c)
    n_lvl = sel.shape[0] // c

    def sec(s):
        return pl.BlockSpec((c, width), lambda b, j: (blk0 + b * n + j, s))

    state_spec = pl.BlockSpec((1, n_heads, HEAD_DIM, HEAD_DIM), lambda b, j: (b, 0, 0, 0))
    inputs = [proj, proj, proj, proj, s0_t, lb.reshape(1, width), gnorm.reshape(1, HEAD_DIM),
              jnp.asarray(sel).astype(BF16), jnp.asarray(lvl)]
    in_specs = [sec(0), sec(1), sec(2), sec(3), state_spec,
                pl.BlockSpec((1, width), lambda b, j: (0, 0)),
                pl.BlockSpec((1, HEAD_DIM), lambda b, j: (0, 0)),
                pl.BlockSpec(sel.shape, lambda b, j: (0, 0)),
                pl.BlockSpec(lvl.shape, lambda b, j: (0, 0))]
    return pl.pallas_call(
        _fill_in_place(functools.partial(_hgrn_kernel, n_heads, c), len(inputs), base),
        grid=(batch, n),
        in_specs=in_specs + ([] if base is None else [pl.BlockSpec(memory_space=pl.ANY)]),
        out_specs=[pl.BlockSpec((c, width), lambda b, j: (blk0 + b * n + j, 0)), state_spec],
        out_shape=[jax.ShapeDtypeStruct((proj.shape[0], 2 * width), BF16),
                   jax.ShapeDtypeStruct((batch, n_heads, HEAD_DIM, HEAD_DIM), F32)],
        input_output_aliases={} if base is None else {len(inputs): 0},
        scratch_shapes=[pltpu.VMEM((n_heads, HEAD_DIM, HEAD_DIM), F32),
                        pltpu.VMEM((n_lvl * c, width), F32),
                        pltpu.VMEM((n_lvl, c, width), BF16),
                        pltpu.VMEM((n_lvl, c, width), BF16)],
        compiler_params=_params(),
        name="hgrn_mixer",
    )(*inputs, *([] if base is None else [base]))


def _pool_kernel(pos0, bt, group, u_ref, prev_ref, w_ref, scale_ref, o_ref, ext_ref):
    j = pl.program_id(1)

    @pl.when(j == 0)
    def _():
        ext_ref[0:POOL_CARRY] = prev_ref[0]

    u = u_ref[...]
    ext_ref[POOL_CARRY:POOL_CARRY + bt] = u
    pos = pos0 + j * bt + lax.broadcasted_iota(I32, (bt, 1), 0)
    for gi, win in enumerate(POOL_WINDOWS):
        cs = slice(gi * group, (gi + 1) * group)
        s = ext_ref[:, cs]
        step = 1
        while step < win:
            s = s + pltpu.roll(s, step, axis=0)
            step *= 2
        cnt = jnp.minimum(pos + 1, win).astype(F32)
        d = s[POOL_CARRY:] / cnt - u[:, cs]
        y = _dot(d.astype(BF16), w_ref[gi]) * scale_ref[:, cs]
        o_ref[:, cs] = y.astype(o_ref.dtype)
    ext_ref[0:POOL_CARRY] = ext_ref[bt:bt + POOL_CARRY]


def pool_mixer(proj, row0, batch, seq, pos0, prev, pool_w, pool_scale, base):
    width = proj.shape[1] // 5
    group = width // len(POOL_WINDOWS)
    bt = min(POOL_BLOCK_T, seq)
    n = seq // bt
    assert seq % bt == 0 and row0 % bt == 0 and sum(POOL_WINDOWS) // 2 <= POOL_CARRY
    blk0 = row0 // bt
    inputs = [proj, prev, pool_w, pool_scale.reshape(1, width)]
    return pl.pallas_call(
        _fill_in_place(functools.partial(_pool_kernel, pos0, bt, group), len(inputs), base),
        grid=(batch, n),
        in_specs=[pl.BlockSpec((bt, width), lambda b, j: (blk0 + b * n + j, 4)),
                  pl.BlockSpec((1, POOL_CARRY, width), lambda b, j: (b, 0, 0)),
                  pl.BlockSpec(pool_w.shape, lambda b, j: (0, 0, 0)),
                  pl.BlockSpec((1, width), lambda b, j: (0, 0)),
                  pl.BlockSpec(memory_space=pl.ANY)],
        out_specs=pl.BlockSpec((bt, width), lambda b, j: (blk0 + b * n + j, 1)),
        out_shape=jax.ShapeDtypeStruct(base.shape, base.dtype),
        input_output_aliases={len(inputs): 0},
        scratch_shapes=[pltpu.VMEM((bt + POOL_CARRY, width), F32)],
        compiler_params=_params(),
        name="pool_mixer",
    )(*inputs, base)


def _rel_bias(table, dist):
    idx = jnp.clip(dist, -(CHUNK - 1), MAX_REL) + (CHUNK - 1)
    return table[:, idx].astype(F32)


def _attn_prompt_kernel(bq, hps, scale, q_ref, k0_ref, k1_ref, k2_ref, v0_ref, v1_ref, v2_ref,
                        vals_ref, o_ref, bias_ref, kcat_ref, vcat_ref):
    b = pl.program_id(1)
    i = pl.program_id(2)
    band = WINDOW + CHUNK

    @pl.when(jnp.logical_and(b == 0, i == 0))
    def _():
        span = vals_ref.shape[-1]
        for hh in range(hps):
            rows = jnp.broadcast_to(vals_ref[0, hh:hh + 1, :], (CHUNK, span))
            toep = pltpu.roll(rows, span - (CHUNK - 1), 1, stride=1, stride_axis=0)
            bias_ref[hh] = toep[:, :band]

    for p, (k_ref, v_ref) in enumerate(((k0_ref, v0_ref), (k1_ref, v1_ref), (k2_ref, v2_ref))):
        kcat_ref[p * bq:(p + 1) * bq] = k_ref[...]
        vcat_ref[p * bq:(p + 1) * bq] = v_ref[...]

    col = lax.broadcasted_iota(I32, (CHUNK, band), 1)
    starts = [c * CHUNK for c in range(bq // CHUNK)]
    for h0 in range(0, hps, ATTN_HEADS_INTERLEAVED):
        heads = range(h0, min(h0 + ATTN_HEADS_INTERLEAVED, hps))
        work = [(hh, slice(hh * HEAD_DIM, (hh + 1) * HEAD_DIM), r0) for hh in heads for r0 in starts]
        s = [_dot_nt(q_ref[r0:r0 + CHUNK, hs], kcat_ref[r0:r0 + band, hs]) * scale + bias_ref[hh]
             for hh, hs, r0 in work]
        s = [jnp.where(col >= WINDOW - r0 - i * bq, sc, NEG) for (_, _, r0), sc in zip(work, s)]
        m = [jnp.max(sc, axis=-1, keepdims=True) for sc in s]
        e = [jnp.exp(sc - mc) for sc, mc in zip(s, m)]
        inv = [1.0 / jnp.sum(ec, axis=-1, keepdims=True) for ec in e]
        o = [_dot((ec * ic).astype(BF16), vcat_ref[r0:r0 + band, hs])
             for (_, hs, r0), ec, ic in zip(work, e, inv)]
        for n, hh in enumerate(heads):
            o_ref[:, hh * HEAD_DIM:(hh + 1) * HEAD_DIM] = jnp.concatenate(
                o[n * len(starts):(n + 1) * len(starts)], axis=0).astype(o_ref.dtype)


def band_attention_prompt(qkv, batch, seq, rel_bias):
    d = qkv.shape[1] // 3
    n_heads = d // HEAD_DIM
    bq = ATTN_BLOCK_Q
    hps = min(ATTN_HEADS_PER_STEP, n_heads)
    assert seq % bq == 0 and WINDOW == 2 * bq and bq % CHUNK == 0 and n_heads % hps == 0
    nq = seq // bq
    ng = n_heads // hps
    band = WINDOW + CHUNK
    span = -(-(band + CHUNK - 1) // LANES) * LANES
    offs = jnp.arange(span) - (CHUNK - 1)
    vals = _rel_bias(rel_bias, WINDOW - offs).reshape(ng, hps, span)

    def kv(sec, p):
        return pl.BlockSpec((bq, hps * HEAD_DIM),
                            lambda g, b, i: (b * nq + jnp.maximum(i - 2 + p, 0), sec * ng + g))

    return pl.pallas_call(
        functools.partial(_attn_prompt_kernel, bq, hps, HEAD_DIM ** -0.5),
        grid=(ng, batch, nq),
        in_specs=[pl.BlockSpec((bq, hps * HEAD_DIM), lambda g, b, i: (b * nq + i, g)),
                  kv(1, 0), kv(1, 1), kv(1, 2), kv(2, 0), kv(2, 1), kv(2, 2),
                  pl.BlockSpec((1, hps, span), lambda g, b, i: (g, 0, 0))],
        out_specs=pl.BlockSpec((bq, hps * HEAD_DIM), lambda g, b, i: (b * nq + i, g)),
        out_shape=jax.ShapeDtypeStruct((qkv.shape[0], d), BF16),
        scratch_shapes=[pltpu.VMEM((hps, CHUNK, band), F32),
                        pltpu.VMEM((3 * bq, hps * HEAD_DIM), BF16),
                        pltpu.VMEM((3 * bq, hps * HEAD_DIM), BF16)],
        compiler_params=_params(),
        name="band_attention_prompt",
    )(qkv, qkv, qkv, qkv, qkv, qkv, qkv, vals)


def _attn_sample_kernel(n_heads, lc, scale, qkv_ref, ck_ref, cv_ref, bias_ref, base_ref, o_ref):
    del base_ref
    d = n_heads * HEAD_DIM

    def cols(sec, h):
        return slice(sec * d + h * HEAD_DIM, sec * d + (h + 1) * HEAD_DIM)

    for h0 in range(0, n_heads, ATTN_HEADS_INTERLEAVED):
        heads = range(h0, min(h0 + ATTN_HEADS_INTERLEAVED, n_heads))
        sc = [_dot_nt(qkv_ref[:, cols(0, h)], ck_ref[0, 0, :, h, :].astype(BF16)) * scale
              + bias_ref[h, :, 0:lc] for h in heads]
        sn = [_dot_nt(qkv_ref[:, cols(0, h)], qkv_ref[:, cols(1, h)]) * scale + bias_ref[h, :, lc:]
              for h in heads]
        m = [jnp.maximum(jnp.max(a, axis=-1, keepdims=True), jnp.max(b, axis=-1, keepdims=True))
             for a, b in zip(sc, sn)]
        ec = [jnp.exp(a - mh) for a, mh in zip(sc, m)]
        en = [jnp.exp(b - mh) for b, mh in zip(sn, m)]
        inv = [1.0 / (jnp.sum(a, axis=-1, keepdims=True) + jnp.sum(b, axis=-1, keepdims=True))
               for a, b in zip(ec, en)]
        for h, a, b, ih in zip(heads, ec, en, inv):
            o = (_dot((a * ih).astype(BF16), cv_ref[0, 0, :, h, :].astype(BF16))
                 + _dot((b * ih).astype(BF16), qkv_ref[:, cols(2, h)]))
            o_ref[:, cols(0, h)] = o.astype(o_ref.dtype)


def band_attention_sample(qkv, row0, batch, seq, cache_k, cache_v, layer, rel_bias, base):
    d = qkv.shape[1] // 3
    n_heads = d // HEAD_DIM
    lc = cache_k.shape[2]
    assert row0 % seq == 0 and lc % 128 == 0
    blk0 = row0 // seq
    dist = jnp.arange(seq)[:, None] + lc - jnp.arange(lc + seq)[None, :]
    bias = _rel_bias(rel_bias, dist)
    cache_spec = pl.BlockSpec((1, 1, lc, n_heads, HEAD_DIM), lambda b: (layer, b, 0, 0, 0))
    return pl.pallas_call(
        functools.partial(_attn_sample_kernel, n_heads, lc, HEAD_DIM ** -0.5),
        grid=(batch,),
        in_specs=[pl.BlockSpec((seq, 3 * d), lambda b: (blk0 + b, 0)),
                  cache_spec, cache_spec,
                  pl.BlockSpec(bias.shape, lambda b: (0, 0, 0)),
                  pl.BlockSpec(memory_space=pl.ANY)],
        out_specs=pl.BlockSpec((seq, d), lambda b: (blk0 + b, 0)),
        out_shape=jax.ShapeDtypeStruct(base.shape, base.dtype),
        input_output_aliases={4: 0},
        compiler_params=_params(),
        name="band_attention_sample",
    )(qkv, cache_k, cache_v, bias, base)


def _first_max(vals, idx, sentinel):
    m = jnp.max(vals, axis=0, keepdims=True)
    first = jnp.min(jnp.where(vals == m, idx, sentinel), axis=0, keepdims=True)
    return m, first


def _stack_rows(rows, lanes):
    ridx = lax.broadcasted_iota(I32, (len(rows), lanes), 0)
    out = jnp.zeros((len(rows), lanes), rows[0].dtype)
    for r, row in enumerate(rows):
        out = jnp.where(ridx == r, row, out)
    return out


def _router_kernel(n_experts, bt, x_ref, wt_ref, bias_ref, e_ref, gate_ref, rank_ref, cnt_ref,
                   carry_ref):
    i = pl.program_id(0)

    @pl.when(i == 0)
    def _():
        carry_ref[...] = jnp.zeros_like(carry_ref)

    per = n_experts // N_GROUPS
    logits = _dot_nt(wt_ref[...], x_ref[...])
    scores = _sigmoid(logits)
    sel = scores + bias_ref[:, 0:1]

    sub = lax.broadcasted_iota(I32, (per, bt), 0)
    grp_rows = []
    for g in range(N_GROUPS):
        blk = sel[g * per:(g + 1) * per]
        m1, i1 = _first_max(blk, sub, per)
        m2 = jnp.max(jnp.where(sub == i1, -jnp.inf, blk), axis=0, keepdims=True)
        grp_rows.append(m1 + m2)
    grp = _stack_rows(grp_rows, bt)
    gidx = lax.broadcasted_iota(I32, (N_GROUPS, bt), 0)
    keep = jnp.zeros((N_GROUPS, bt), F32)
    for _ in range(TOPK_GROUPS):
        _, first = _first_max(grp, gidx, N_GROUPS)
        hit = gidx == first
        keep = jnp.where(hit, 1.0, keep)
        grp = jnp.where(hit, -jnp.inf, grp)
    masked = jnp.concatenate(
        [jnp.where(keep[g:g + 1] > 0.0, sel[g * per:(g + 1) * per], -jnp.inf)
         for g in range(N_GROUPS)], axis=0)

    eidx = lax.broadcasted_iota(I32, (n_experts, bt), 0)
    chosen = jnp.zeros((n_experts, bt), F32)
    picks = []
    gates = []
    for _ in range(TOP_K):
        _, first = _first_max(masked, eidx, n_experts)
        hit = eidx == first
        picks.append(first)
        gates.append(jnp.sum(jnp.where(hit, scores, 0.0), axis=0, keepdims=True))
        chosen = jnp.where(hit, 1.0, chosen)
        masked = jnp.where(hit, -jnp.inf, masked)
    gate = _stack_rows(gates, bt)
    gate = gate / jnp.sum(gate, axis=0, keepdims=True) * ROUTE_SCALE

    earlier = (lax.broadcasted_iota(I32, (bt, bt), 0) < lax.broadcasted_iota(I32, (bt, bt), 1))
    before = _dot(chosen.astype(BF16), jnp.where(earlier, 1.0, 0.0).astype(BF16)) + carry_ref[:, 0:1]
    ranks = [jnp.sum(jnp.where(eidx == p, before, 0.0), axis=0, keepdims=True) for p in picks]

    e_ref[...] = _stack_rows(picks, bt)
    gate_ref[...] = gate
    rank_ref[...] = _stack_rows(ranks, bt).astype(I32)
    carry_ref[...] = carry_ref[...] + jnp.sum(chosen, axis=1, keepdims=True)
    cnt_ref[...] = carry_ref[...]


def moe_router(x, w_router, router_bias, bt=ROUTER_BLOCK_T):
    n, d = x.shape
    n_experts = w_router.shape[1]
    bt = min(bt, n)
    assert n % bt == 0 and n_experts % N_GROUPS == 0
    lanes = 128
    tok = pl.BlockSpec((TOP_K, bt), lambda i: (0, i))
    e, gate, rank, cnt = pl.pallas_call(
        functools.partial(_router_kernel, n_experts, bt),
        grid=(n // bt,),
        in_specs=[pl.BlockSpec((bt, d), lambda i: (i, 0)),
                  pl.BlockSpec((n_experts, d), lambda i: (0, 0)),
                  pl.BlockSpec((n_experts, lanes), lambda i: (0, 0))],
        out_specs=[tok, tok, tok, pl.BlockSpec((n_experts, lanes), lambda i: (0, 0))],
        out_shape=[jax.ShapeDtypeStruct((TOP_K, n), I32), jax.ShapeDtypeStruct((TOP_K, n), F32),
                   jax.ShapeDtypeStruct((TOP_K, n), I32),
                   jax.ShapeDtypeStruct((n_experts, lanes), F32)],
        scratch_shapes=[pltpu.VMEM((n_experts, lanes), F32)],
        compiler_params=_params(),
        name="moe_router",
    )(x, w_router.T.astype(BF16),
      jnp.broadcast_to(router_bias.astype(F32)[:, None], (n_experts, lanes)))
    return e, gate, rank, cnt[:, 0].astype(I32)


def _dispatch_kernel(bt, n_experts, fill_start_ref, fill_len_ref, dest_ref, x_ref, xs_ref,
                     zero_ref, sem, fill_sem):
    i = pl.program_id(0)
    sub = xs_ref.shape[1]

    @pl.when(i == 0)
    def _():
        zero_ref[...] = jnp.zeros_like(zero_ref)

        def fill(e, carry):
            base = fill_start_ref[e]

            def zero_copy(r):
                return pltpu.make_async_copy(zero_ref, xs_ref.at[base + r], fill_sem)

            def start(r, c):
                zero_copy(r).start()
                return c

            def wait(r, c):
                zero_copy(r).wait()
                return c

            lax.fori_loop(0, fill_len_ref[e], start, 0)
            lax.fori_loop(0, fill_len_ref[e], wait, 0)
            return carry

        lax.fori_loop(0, n_experts, fill, 0)

    for t in range(bt):
        for k in range(TOP_K):
            pltpu.make_async_copy(x_ref.at[pl.ds(t * sub, sub)], xs_ref.at[dest_ref[t * TOP_K + k]],
                                  sem).start(priority=k % 2)
    for k in range(TOP_K):
        pltpu.make_async_copy(x_ref, x_ref, sem).wait()


def moe_dispatch(xpk, dest_flat, fill_start, fill_len, rows, bt=DISPATCH_BLOCK_T):
    n = dest_flat.shape[0] // TOP_K
    sub = xpk.shape[0] // n
    bt = min(bt, n)
    assert n % bt == 0
    n_experts = fill_start.shape[0]
    return pl.pallas_call(
        functools.partial(_dispatch_kernel, bt, n_experts),
        grid_spec=pltpu.PrefetchScalarGridSpec(
            num_scalar_prefetch=2,
            grid=(n // bt,),
            in_specs=[pl.BlockSpec((bt * TOP_K,), lambda i, *_: (i,), memory_space=pltpu.SMEM),
                      pl.BlockSpec((bt * sub, LANES), lambda i, *_: (i, 0))],
            out_specs=pl.BlockSpec(memory_space=pl.ANY),
            scratch_shapes=[pltpu.VMEM((sub, LANES), U32),
                            pltpu.SemaphoreType.DMA, pltpu.SemaphoreType.DMA]),
        out_shape=jax.ShapeDtypeStruct((rows, sub, LANES), U32),
        compiler_params=_params(),
        name="moe_dispatch",
    )(fill_start, fill_len, dest_flat, xpk)


def _expert_kernel(blk, tile_e_ref, n_used_ref, xs_ref, wgu_ref, wdn_ref, ys_ref, wgu_b, wdn_b):
    i = pl.program_id(0)
    de = wdn_b.shape[0]
    changed = jnp.logical_or(i == 0, tile_e_ref[i] != tile_e_ref[jnp.maximum(i - 1, 0)])

    @pl.when(changed)
    def _():
        wgu_b[...] = wgu_ref[0, 0].astype(BF16)
        wdn_b[...] = wdn_ref[0, 0].astype(BF16)

    @pl.when(i < n_used_ref[0])
    def _():
        lo, hi = _unpack_pairs(_load_row_tiles(xs_ref, blk))
        x = jnp.concatenate([lo.astype(BF16), hi.astype(BF16)], axis=1)
        gu = _dot(x, wgu_b[...])
        a = _silu(gu[:, :de]) * gu[:, de:]
        _store_row_tiles(ys_ref, _pack_pairs(_dot(a.astype(BF16), wdn_b[...])))

    @pl.when(i >= n_used_ref[0])
    def _():
        ys_ref[...] = jnp.zeros_like(ys_ref)


def moe_experts(xs, tile_e, n_used, w_gu, w_dn, layer, blk=EXPERT_BLOCK):
    sub = w_gu.shape[2] // 2 // LANES
    d = 2 * sub * LANES
    rows = xs.shape[0] // sub
    n_tiles = rows // blk
    de = w_dn.shape[2]
    return pl.pallas_call(
        functools.partial(_expert_kernel, blk),
        grid_spec=pltpu.PrefetchScalarGridSpec(
            num_scalar_prefetch=2,
            grid=(n_tiles,),
            in_specs=[pl.BlockSpec((blk * sub, LANES),
                                   lambda i, te, nu: (jnp.minimum(i, nu[0] - 1), 0)),
                      pl.BlockSpec((1, 1, d, 2 * de), lambda i, te, nu: (layer, te[i], 0, 0)),
                      pl.BlockSpec((1, 1, de, d), lambda i, te, nu: (layer, te[i], 0, 0))],
            out_specs=pl.BlockSpec((blk * sub, LANES), lambda i, te, nu: (i, 0)),
            scratch_shapes=[pltpu.VMEM((d, 2 * de), BF16), pltpu.VMEM((de, d), BF16)]),
        out_shape=jax.ShapeDtypeStruct((rows * sub, LANES), U32),
        compiler_params=_params(),
        name="moe_experts",
    )(tile_e, n_used, xs, w_gu, w_dn)


def _combine_kernel(alpha, bt, dest_ref, dest_next_ref, ys_ref, gate_ref, x_ref, xb_ref, wsgu_ref,
                    wsdn_ref, g_ref, b_ref, xo_ref, xbo_ref, buf_a, buf_b, sem_a, sem_b):
    i = pl.program_id(0)
    de = wsdn_ref.shape[0]
    sub = ys_ref.shape[1]
    per_tile = bt * TOP_K

    def row_copy(idx_ref, base, buf, sem, t, k):
        return pltpu.make_async_copy(ys_ref.at[idx_ref[base + t * TOP_K + k]],
                                     buf.at[k, pl.ds(t * sub, sub)], sem)

    def issue_unrolled(idx_ref, base, buf, sem):
        for t in range(bt):
            for k in range(TOP_K):
                row_copy(idx_ref, base, buf, sem, t, k).start(priority=k % 2)

    def drain(buf, other, sem):
        pltpu.make_async_copy(other, buf, sem).wait()

    def compute(r0, buf):
        rows = pl.ds(r0, bt)
        sgu = _dot(xb_ref[rows, :], wsgu_ref[...])
        shared = _dot((_silu(sgu[:, :de]) * sgu[:, de:]).astype(BF16), wsdn_ref[...])
        routed_lo, routed_hi = None, None
        for k in range(TOP_K):
            gate = gate_ref[rows, k:k + 1]
            lo, hi = _unpack_pairs(_load_row_tiles(buf.at[k], bt))
            routed_lo = gate * lo if k == 0 else routed_lo + gate * lo
            routed_hi = gate * hi if k == 0 else routed_hi + gate * hi
        routed = jnp.concatenate([routed_lo, routed_hi], axis=1)
        out = _layer_norm(alpha * x_ref[rows, :] + (routed + shared), g_ref[...], b_ref[...])
        xo_ref[rows, :] = out
        xbo_ref[rows, :] = out.astype(BF16)

    @pl.when(i == 0)
    def _():
        def body(t, carry):
            for k in range(TOP_K):
                row_copy(dest_ref, 0, buf_a, sem_a, t, k).start()
            return carry
        lax.fori_loop(0, bt, body, 0)

    drain(buf_a, buf_b, sem_a)
    issue_unrolled(dest_ref, per_tile, buf_b, sem_b)
    compute(0, buf_a)
    drain(buf_b, buf_a, sem_b)
    issue_unrolled(dest_next_ref, 0, buf_a, sem_a)
    compute(bt, buf_b)

    @pl.when(i == pl.num_programs(0) - 1)
    def _():
        drain(buf_a, buf_b, sem_a)


def moe_combine_ln(ys, dest_flat, gate_tm, x, xb, ws_gu, ws_dn, g, b, alpha, bt=COMBINE_BLOCK_T):
    n, d = x.shape
    bt = min(bt, n // 2)
    assert n % (2 * bt) == 0
    n_tiles = n // bt
    de = ws_dn.shape[0]
    sub = ys.shape[1]
    row = pl.BlockSpec((2 * bt, d), lambda i: (i, 0))
    vec = pl.BlockSpec((1, d), lambda i: (0, 0))
    return pl.pallas_call(
        functools.partial(_combine_kernel, alpha, bt),
        grid=(n_tiles // 2,),
        in_specs=[pl.BlockSpec((2 * bt * TOP_K,), lambda i: (i,), memory_space=pltpu.SMEM),
                  pl.BlockSpec((bt * TOP_K,), lambda i: (jnp.minimum(2 * i + 2, n_tiles - 1),),
                               memory_space=pltpu.SMEM),
                  pl.BlockSpec(memory_space=pl.ANY),
                  pl.BlockSpec((2 * bt, TOP_K), lambda i: (i, 0)),
                  row, row,
                  pl.BlockSpec((d, 2 * de), lambda i: (0, 0)),
                  pl.BlockSpec((de, d), lambda i: (0, 0)),
                  vec, vec],
        out_specs=[row, row],
        out_shape=[jax.ShapeDtypeStruct((n, d), F32), jax.ShapeDtypeStruct((n, d), BF16)],
        scratch_shapes=[pltpu.VMEM((TOP_K, bt * sub, LANES), U32),
                        pltpu.VMEM((TOP_K, bt * sub, LANES), U32),
                        pltpu.SemaphoreType.DMA, pltpu.SemaphoreType.DMA],
        compiler_params=_params(),
        name="moe_combine_ln",
    )(dest_flat, dest_flat, ys, gate_tm, x, xb, ws_gu, ws_dn, g.reshape(1, d), b.reshape(1, d))


def moe_ffn_ln(x, xb, xpk, w_router, router_bias, w_gu, w_dn, layer, ws_gu, ws_dn, g, b, alpha):
    n, d = x.shape
    n_experts = w_router.shape[1]
    blk = EXPERT_BLOCK
    top_e, gate, rank, counts = moe_router(xb, w_router, router_bias)
    padded = (counts + blk - 1) // blk * blk
    pend = jnp.cumsum(padded)
    pstart = pend - padded
    hot = top_e[:, :, None] == jnp.arange(n_experts, dtype=I32)
    dest = jnp.sum(jnp.where(hot, pstart, 0), axis=-1) + rank
    dest_flat = dest.T.reshape(-1)
    n_tiles = (n * TOP_K + n_experts * (blk - 1) + blk - 1) // blk
    tile_row0 = jnp.arange(n_tiles, dtype=I32) * blk
    tile_e = jnp.minimum(jnp.sum((pend[None, :] <= tile_row0[:, None]).astype(I32), axis=1),
                         n_experts - 1)
    n_used = (pend[-1:] // blk).astype(I32)
    sub = d // 2 // LANES
    rows = n_tiles * blk
    xs = moe_dispatch(xpk, dest_flat, (pstart + counts).astype(I32), (padded - counts).astype(I32),
                      rows)
    ys = moe_experts(xs.reshape(rows * sub, LANES), tile_e, n_used, w_gu, w_dn, layer, blk)
    ys = ys.reshape(rows, sub, LANES)
    return moe_combine_ln(ys, dest_flat, gate.T, x, xb, ws_gu, ws_dn, g, b, alpha)


def kernel(x_prompt, x_sample, state_hgrn, state_pool, cache_k, cache_v, ln_g, ln_b, w_in_even,
           lb_even, gnorm_even, pool_w_even, pool_scale_even, w_out_even, w_qkv_odd, rel_bias_odd,
           w_o_odd, w_router, router_bias, w_gu, w_dn, ws_gu, ws_dn):
    batch, seq, d = x_prompt.shape
    dec_batch, dec_seq, _ = x_sample.shape
    depth = ln_g.shape[0]
    n_p = batch * seq
    n_s = dec_batch * dec_seq
    width = w_in_even.shape[2] // 5
    n_heads_a = width // HEAD_DIM
    n_heads_c = d // HEAD_DIM
    alpha = (2 * depth) ** 0.25
    assert dec_seq >= POOL_PREV and seq >= max(POOL_PREV, WINDOW)

    x = jnp.concatenate([x_prompt.reshape(n_p, d), x_sample.reshape(n_s, d)], axis=0)
    xb = x.astype(BF16)

    p = jax.nn.softmax(lb_even.astype(F32), axis=0)
    lbs = jnp.cumsum(p, axis=0) - p[0]

    hgrn_p, pool_p, k_p, v_p, hgrn_s, pool_s, k_s, v_s = [], [], [], [], [], [], [], []
    for l in range(depth):
        j = l // 2
        if l % 2 == 0:
            proj = matmul(xb, w_in_even[j].astype(BF16))
            zero_state = jnp.zeros((batch, n_heads_a, HEAD_DIM, HEAD_DIM), F32)
            mixed, sp = hgrn_mixer(proj, 0, batch, seq, zero_state, lbs[j], gnorm_even[j])
            mixed, ss = hgrn_mixer(proj, n_p, dec_batch, dec_seq,
                                   jnp.swapaxes(state_hgrn[j].astype(F32), -1, -2), lbs[j],
                                   gnorm_even[j], mixed)
            pw = pool_w_even[j].astype(BF16)
            mixed = pool_mixer(proj, 0, batch, seq, 0, jnp.zeros((batch, POOL_CARRY, width), F32),
                               pw, pool_scale_even[j], mixed)
            prev_s = jnp.pad(state_pool[j].astype(F32),
                             ((0, 0), (POOL_CARRY - POOL_PREV, 0), (0, 0)))
            mixed = pool_mixer(proj, n_p, dec_batch, dec_seq, PAST_LEN, prev_s, pw,
                               pool_scale_even[j], mixed)
            w_mix = w_out_even[j]
            hgrn_p.append(jnp.swapaxes(sp, -1, -2))
            hgrn_s.append(jnp.swapaxes(ss, -1, -2))
            pool_p.append(jnp.stack([proj[(b + 1) * seq - POOL_PREV:(b + 1) * seq, 4 * width:]
                                     for b in range(batch)]))
            pool_s.append(proj[n_p:].reshape(dec_batch, dec_seq, 5 * width)
                          [:, dec_seq - POOL_PREV:, 4 * width:])
        else:
            qkv = matmul(xb, w_qkv_odd[j].astype(BF16), out_dtype=BF16)
            mixed = band_attention_prompt(qkv, batch, seq, rel_bias_odd[j])
            mixed = band_attention_sample(qkv, n_p, dec_batch, dec_seq, cache_k, cache_v, j,
                                          rel_bias_odd[j], mixed)
            w_mix = w_o_odd[j]
            keep = min(WINDOW, seq)
            tail_p = jnp.stack([qkv[(b + 1) * seq - keep:(b + 1) * seq]
                                for b in range(batch)]).astype(F32)
            new_s = qkv[n_p:].reshape(dec_batch, dec_seq, 3 * d).astype(F32)
            k_p.append(tail_p[:, :, d:2 * d].reshape(batch, keep, n_heads_c, HEAD_DIM))
            v_p.append(tail_p[:, :, 2 * d:].reshape(batch, keep, n_heads_c, HEAD_DIM))
            k_s.append(new_s[:, :, d:2 * d].reshape(dec_batch, dec_seq, n_heads_c, HEAD_DIM))
            v_s.append(new_s[:, :, 2 * d:].reshape(dec_batch, dec_seq, n_heads_c, HEAD_DIM))
        x, xb, xpk = matmul_residual_ln(mixed, w_mix.astype(BF16), x, ln_g[l, 0], ln_b[l, 0], alpha)
        x, xb = moe_ffn_ln(x, xb, xpk, w_router[l], router_bias[l], w_gu, w_dn, l,
                           ws_gu[l].astype(BF16), ws_dn[l].astype(BF16), ln_g[l, 1], ln_b[l, 1],
                           alpha)

    y_p = x[:n_p].reshape(batch, seq, d)
    y_s = x[n_p:].reshape(dec_batch, dec_seq, d)
    return (y_p, y_s, jnp.stack(hgrn_p), jnp.stack(pool_p), jnp.stack(k_p), jnp.stack(v_p),
            jnp.stack(hgrn_s), jnp.stack(pool_s), jnp.stack(k_s), jnp.stack(v_s))
```

```python
import functools

import numpy as np
import jax
import jax.numpy as jnp
from jax import lax
from jax.experimental import pallas as pl
from jax.experimental.pallas import tpu as pltpu

F32 = jnp.float32
BF16 = jnp.bfloat16
I32 = jnp.int32
U32 = jnp.uint32

CHUNK = 64
HEAD_DIM = 128
LANES = 128
POOL_WINDOWS = (2, 4, 8, 16)
POOL_PREV = max(POOL_WINDOWS) - 1
POOL_CARRY = POOL_PREV + 1
LEFT_CHUNKS = 8
WINDOW = LEFT_CHUNKS * CHUNK
MAX_REL = 128
N_GROUPS = 8
TOPK_GROUPS = 4
TOP_K = 8
ROUTE_SCALE = 2.5
LN_EPS = 1e-5
RMS_EPS = 1e-6
PAST_LEN = 1024
NEG = -1e30

VMEM_LIMIT_BYTES = 56 * 1024 * 1024
MM_BLOCK_M = 1024
MM_BLOCK_N = 1024
LN_BLOCK_M = 512
HGRN_BLOCK_T = 128
HGRN_HEADS_INTERLEAVED = 8
ATTN_BLOCK_Q = 256
ATTN_HEADS_PER_STEP = 8
ATTN_HEADS_INTERLEAVED = 4
POOL_BLOCK_T = 512
ROUTER_BLOCK_T = 512
DISPATCH_BLOCK_T = 128
EXPERT_BLOCK = 512
COMBINE_BLOCK_T = 128


def _params(**kw):
    return pltpu.CompilerParams(vmem_limit_bytes=VMEM_LIMIT_BYTES, **kw)


def _sigmoid(x):
    return 1.0 / (1.0 + jnp.exp(-x))


def _silu(x):
    return x * _sigmoid(x)


def _dot(a, b):
    return jnp.dot(a, b, preferred_element_type=F32)


def _dot_nt(a, b):
    return lax.dot_general(a, b, (((1,), (1,)), ((), ())), preferred_element_type=F32)


def _dot_tn(a, b):
    return lax.dot_general(a, b, (((0,), (0,)), ((), ())), preferred_element_type=F32)


def _layer_norm(y, g, b):
    mu = jnp.mean(y, axis=-1, keepdims=True)
    yc = y - mu
    var = jnp.mean(yc * yc, axis=-1, keepdims=True)
    return yc * lax.rsqrt(var + LN_EPS) * g + b


def _mm_kernel(x_ref, w_ref, o_ref):
    o_ref[...] = _dot(x_ref[...], w_ref[...]).astype(o_ref.dtype)


def matmul(x, w, out_dtype=F32, bm=MM_BLOCK_M, bn=MM_BLOCK_N):
    m, k = x.shape
    n = w.shape[1]
    bm, bn = min(bm, m), min(bn, n)
    assert m % bm == 0 and n % bn == 0
    return pl.pallas_call(
        _mm_kernel,
        grid=(m // bm, n // bn),
        in_specs=[pl.BlockSpec((bm, k), lambda i, j: (i, 0)),
                  pl.BlockSpec((k, bn), lambda i, j: (0, j))],
        out_specs=pl.BlockSpec((bm, bn), lambda i, j: (i, j)),
        out_shape=jax.ShapeDtypeStruct((m, n), out_dtype),
        compiler_params=_params(),
        name="matmul",
    )(x, w)


def _pack_pairs(x):
    n = x.shape[1] // 2
    lo = lax.bitcast_convert_type(x[:, :n].astype(jnp.bfloat16).astype(F32), U32)
    hi = lax.bitcast_convert_type(x[:, n:].astype(jnp.bfloat16).astype(F32), U32)
    return (lo >> 16) | hi


def _unpack_pairs(u):
    lo = lax.bitcast_convert_type(u << 16, F32)
    hi = lax.bitcast_convert_type(u & jnp.uint32(0xFFFF0000), F32)
    return lo, hi


def _store_row_tiles(ref, u):
    rows = u.shape[0]
    sub = ref.shape[0] // rows
    for c in range(sub):
        ref[pl.ds(c, rows, stride=sub), :] = u[:, c * LANES:(c + 1) * LANES]


def _load_row_tiles(ref, rows):
    sub = ref.shape[0] // rows
    return jnp.concatenate([ref[pl.ds(c, rows, stride=sub), :] for c in range(sub)], axis=1)


def _mm_ln_kernel(alpha, a_ref, w_ref, res_ref, g_ref, b_ref, x_ref, xb_ref, xpk_ref):
    y = alpha * res_ref[...] + _dot(a_ref[...], w_ref[...])
    out = _layer_norm(y, g_ref[...], b_ref[...])
    x_ref[...] = out
    xb_ref[...] = out.astype(BF16)
    _store_row_tiles(xpk_ref, _pack_pairs(out))


def matmul_residual_ln(a, w, res, g, b, alpha, bm=LN_BLOCK_M):
    m, k = a.shape
    d = w.shape[1]
    bm = min(bm, m)
    assert m % bm == 0 and d % (2 * LANES) == 0
    sub = d // 2 // LANES
    return pl.pallas_call(
        functools.partial(_mm_ln_kernel, alpha),
        grid=(m // bm,),
        in_specs=[pl.BlockSpec((bm, k), lambda i: (i, 0)),
                  pl.BlockSpec((k, d), lambda i: (0, 0)),
                  pl.BlockSpec((bm, d), lambda i: (i, 0)),
                  pl.BlockSpec((1, d), lambda i: (0, 0)),
                  pl.BlockSpec((1, d), lambda i: (0, 0))],
        out_specs=[pl.BlockSpec((bm, d), lambda i: (i, 0)),
                   pl.BlockSpec((bm, d), lambda i: (i, 0)),
                   pl.BlockSpec((bm * sub, LANES), lambda i: (i, 0))],
        out_shape=[jax.ShapeDtypeStruct((m, d), F32), jax.ShapeDtypeStruct((m, d), BF16),
                   jax.ShapeDtypeStruct((m * sub, LANES), U32)],
        compiler_params=_params(),
        name="matmul_residual_ln",
    )(a, w, res, g.reshape(1, d), b.reshape(1, d))


def _hgrn_tables(c):
    t = np.arange(c)
    levels = []
    b = c // 2
    while b >= 1:
        levels.append(b)
        b //= 2
    sel = [(t[None, :] <= t[:, None]).astype(np.float32)]
    lvl = np.where(np.eye(c, dtype=bool), 0, -1).astype(np.int32)
    for l, b in enumerate(levels):
        blk = t // b
        odd = (blk % 2) == 1
        ref_row = np.where(odd, blk * b - 1, blk * b + b - 1)
        sel.append((t[None, :] <= ref_row[:, None]).astype(np.float32))
        same = (t[:, None] // (2 * b)) == (t[None, :] // (2 * b))
        lvl = np.where(same & odd[:, None] & (~odd)[None, :], l + 1, lvl)
    sel = np.concatenate(sel, axis=0)
    return np.concatenate([sel, sel, sel], axis=1), lvl


def _hgrn_kernel(n_heads, c, q_ref, f_ref, i_ref, g_ref, s0_ref, lb_ref, gn_ref, sel_ref, lvl_ref,
                 o_ref, sfin_ref, state_ref, cums_ref, qw_ref, kw_ref):
    j = pl.program_id(1)

    @pl.when(j == 0)
    def _():
        state_ref[...] = s0_ref[0]

    n_lvl = qw_ref.shape[0]
    lb = lb_ref[...]
    f = lb + (1.0 - lb) * _sigmoid(f_ref[...])
    k = 1.0 - f
    q = _silu(q_ref[...])
    lg = jnp.log(f)
    hi = lg.astype(BF16)
    rest = lg - hi.astype(F32)
    mid = rest.astype(BF16)
    lo = (rest - mid.astype(F32)).astype(BF16)
    cums_ref[...] = _dot(sel_ref[...], jnp.concatenate([hi, mid, lo], axis=0))
    cum = cums_ref[0:c]
    qw_ref[0] = q.astype(BF16)
    kw_ref[0] = k.astype(BF16)
    for l in range(1, n_lvl):
        w = jnp.exp(-jnp.abs(cum - cums_ref[l * c:(l + 1) * c]))
        qw_ref[l] = (q * w).astype(BF16)
        kw_ref[l] = (k * w).astype(BF16)
    last = cum[c - 1:c, :]
    q_in = (q * jnp.exp(cum)).astype(BF16)
    k_end = (k * jnp.exp(last - cum)).astype(BF16)
    decay = jnp.exp(last)
    lvl = lvl_ref[...]
    for h0 in range(0, n_heads, HGRN_HEADS_INTERLEAVED):
        heads = list(range(h0, min(h0 + HGRN_HEADS_INTERLEAVED, n_heads)))
        cols = [slice(h * HEAD_DIM, (h + 1) * HEAD_DIM) for h in heads]
        st = [state_ref[h] for h in heads]
        o = [_dot_nt(q_in[:, hs], s.astype(BF16)) for hs, s in zip(cols, st)]
        scores = [jnp.zeros((c, c), F32) for _ in heads]
        for l in range(n_lvl):
            scores = [jnp.where(lvl == l, _dot_nt(qw_ref[l, :, hs], kw_ref[l, :, hs]), sc)
                      for hs, sc in zip(cols, scores)]
        v = [i_ref[:, hs].astype(BF16) for hs in cols]
        o = [oh + _dot(sc.astype(BF16), vh) for oh, sc, vh in zip(o, scores, v)]
        for h, hs, s, vh in zip(heads, cols, st, v):
            state_ref[h] = s * decay[:, hs] + _dot_tn(vh, k_end[:, hs])
        o = [oh * lax.rsqrt(jnp.mean(oh * oh, axis=-1, keepdims=True) + RMS_EPS) * gn_ref[...]
             for oh in o]
        for hs, oh in zip(cols, o):
            o_ref[:, hs] = (oh * _silu(g_ref[:, hs])).astype(o_ref.dtype)

    @pl.when(j == pl.num_programs(1) - 1)
    def _():
        sfin_ref[0] = state_ref[...]


def _fill_in_place(body, n_inputs, base):
    if base is None:
        return body
    return lambda *refs: body(*refs[:n_inputs], *refs[n_inputs + 1:])


def hgrn_mixer(proj, row0, batch, seq, s0_t, lb, gnorm, base=None):
    width = proj.shape[1] // 5
    n_heads = width // HEAD_DIM
    c = HGRN_BLOCK_T if seq % HGRN_BLOCK_T == 0 else seq
    n = seq // c
    assert row0 % c == 0 and c & (c - 1) == 0
    blk0 = row0 // c
    sel, lvl = _hgrn_tables(c)
    n_lvl = sel.shape[0] // c

    def sec(s):
        return pl.BlockSpec((c, width), lambda b, j: (blk0 + b * n + j, s))

    state_spec = pl.BlockSpec((1, n_heads, HEAD_DIM, HEAD_DIM), lambda b, j: (b, 0, 0, 0))
    inputs = [proj, proj, proj, proj, s0_t, lb.reshape(1, width), gnorm.reshape(1, HEAD_DIM),
              jnp.asarray(sel).astype(BF16), jnp.asarray(lvl)]
    in_specs = [sec(0), sec(1), sec(2), sec(3), state_spec,
                pl.BlockSpec((1, width), lambda b, j: (0, 0)),
                pl.BlockSpec((1, HEAD_DIM), lambda b, j: (0, 0)),
                pl.BlockSpec(sel.shape, lambda b, j: (0, 0)),
                pl.BlockSpec(lvl.shape, lambda b, j: (0, 0))]
    return pl.pallas_call(
        _fill_in_place(functools.partial(_hgrn_kernel, n_heads, c), len(inputs), base),
        grid=(batch, n),
        in_specs=in_specs + ([] if base is None else [pl.BlockSpec(memory_space=pl.ANY)]),
        out_specs=[pl.BlockSpec((c, width), lambda b, j: (blk0 + b * n + j, 0)), state_spec],
        out_shape=[jax.ShapeDtypeStruct((proj.shape[0], 2 * width), BF16),
                   jax.ShapeDtypeStruct((batch, n_heads, HEAD_DIM, HEAD_DIM), F32)],
        input_output_aliases={} if base is None else {len(inputs): 0},
        scratch_shapes=[pltpu.VMEM((n_heads, HEAD_DIM, HEAD_DIM), F32),
                        pltpu.VMEM((n_lvl * c, width), F32),
                        pltpu.VMEM((n_lvl, c, width), BF16),
                        pltpu.VMEM((n_lvl, c, width), BF16)],
        compiler_params=_params(),
        name="hgrn_mixer",
    )(*inputs, *([] if base is None else [base]))


def _pool_kernel(pos0, bt, group, u_ref, prev_ref, w_ref, scale_ref, o_ref, ext_ref):
    j = pl.program_id(1)

    @pl.when(j == 0)
    def _():
        ext_ref[0:POOL_CARRY] = prev_ref[0]

    u = u_ref[...]
    ext_ref[POOL_CARRY:POOL_CARRY + bt] = u
    pos = pos0 + j * bt + lax.broadcasted_iota(I32, (bt, 1), 0)
    for gi, win in enumerate(POOL_WINDOWS):
        cs = slice(gi * group, (gi + 1) * group)
        s = ext_ref[:, cs]
        step = 1
        while step < win:
            s = s + pltpu.roll(s, step, axis=0)
            step *= 2
        cnt = jnp.minimum(pos + 1, win).astype(F32)
        d = s[POOL_CARRY:] / cnt - u[:, cs]
        y = _dot(d.astype(BF16), w_ref[gi]) * scale_ref[:, cs]
        o_ref[:, cs] = y.astype(o_ref.dtype)
    ext_ref[0:POOL_CARRY] = ext_ref[bt:bt + POOL_CARRY]


def pool_mixer(proj, row0, batch, seq, pos0, prev, pool_w, pool_scale, base):
    width = proj.shape[1] // 5
    group = width // len(POOL_WINDOWS)
    bt = min(POOL_BLOCK_T, seq)
    n = seq // bt
    assert seq % bt == 0 and row0 % bt == 0 and sum(POOL_WINDOWS) // 2 <= POOL_CARRY
    blk0 = row0 // bt
    inputs = [proj, prev, pool_w, pool_scale.reshape(1, width)]
    return pl.pallas_call(
        _fill_in_place(functools.partial(_pool_kernel, pos0, bt, group), len(inputs), base),
        grid=(batch, n),
        in_specs=[pl.BlockSpec((bt, width), lambda b, j: (blk0 + b * n + j, 4)),
                  pl.BlockSpec((1, POOL_CARRY, width), lambda b, j: (b, 0, 0)),
                  pl.BlockSpec(pool_w.shape, lambda b, j: (0, 0, 0)),
                  pl.BlockSpec((1, width), lambda b, j: (0, 0)),
                  pl.BlockSpec(memory_space=pl.ANY)],
        out_specs=pl.BlockSpec((bt, width), lambda b, j: (blk0 + b * n + j, 1)),
        out_shape=jax.ShapeDtypeStruct(base.shape, base.dtype),
        input_output_aliases={len(inputs): 0},
        scratch_shapes=[pltpu.VMEM((bt + POOL_CARRY, width), F32)],
        compiler_params=_params(),
        name="pool_mixer",
    )(*inputs, base)


def _rel_bias(table, dist):
    idx = jnp.clip(dist, -(CHUNK - 1), MAX_REL) + (CHUNK - 1)
    return table[:, idx].astype(F32)


def _attn_prompt_kernel(bq, hps, scale, q_ref, k0_ref, k1_ref, k2_ref, v0_ref, v1_ref, v2_ref,
                        vals_ref, o_ref, bias_ref, kcat_ref, vcat_ref):
    b = pl.program_id(1)
    i = pl.program_id(2)
    band = WINDOW + CHUNK

    @pl.when(jnp.logical_and(b == 0, i == 0))
    def _():
        span = vals_ref.shape[-1]
        for hh in range(hps):
            rows = jnp.broadcast_to(vals_ref[0, hh:hh + 1, :], (CHUNK, span))
            toep = pltpu.roll(rows, span - (CHUNK - 1), 1, stride=1, stride_axis=0)
            bias_ref[hh] = toep[:, :band]

    for p, (k_ref, v_ref) in enumerate(((k0_ref, v0_ref), (k1_ref, v1_ref), (k2_ref, v2_ref))):
        kcat_ref[p * bq:(p + 1) * bq] = k_ref[...]
        vcat_ref[p * bq:(p + 1) * bq] = v_ref[...]

    col = lax.broadcasted_iota(I32, (CHUNK, band), 1)
    starts = [c * CHUNK for c in range(bq // CHUNK)]
    for h0 in range(0, hps, ATTN_HEADS_INTERLEAVED):
        heads = range(h0, min(h0 + ATTN_HEADS_INTERLEAVED, hps))
        work = [(hh, slice(hh * HEAD_DIM, (hh + 1) * HEAD_DIM), r0) for hh in heads for r0 in starts]
        s = [_dot_nt(q_ref[r0:r0 + CHUNK, hs], kcat_ref[r0:r0 + band, hs]) * scale + bias_ref[hh]
             for hh, hs, r0 in work]
        s = [jnp.where(col >= WINDOW - r0 - i * bq, sc, NEG) for (_, _, r0), sc in zip(work, s)]
        m = [jnp.max(sc, axis=-1, keepdims=True) for sc in s]
        e = [jnp.exp(sc - mc) for sc, mc in zip(s, m)]
        inv = [1.0 / jnp.sum(ec, axis=-1, keepdims=True) for ec in e]
        o = [_dot((ec * ic).astype(BF16), vcat_ref[r0:r0 + band, hs])
             for (_, hs, r0), ec, ic in zip(work, e, inv)]
        for n, hh in enumerate(heads):
            o_ref[:, hh * HEAD_DIM:(hh + 1) * HEAD_DIM] = jnp.concatenate(
                o[n * len(starts):(n + 1) * len(starts)], axis=0).astype(o_ref.dtype)


def band_attention_prompt(qkv, batch, seq, rel_bias):
    d = qkv.shape[1] // 3
    n_heads = d // HEAD_DIM
    bq = ATTN_BLOCK_Q
    hps = min(ATTN_HEADS_PER_STEP, n_heads)
    assert seq % bq == 0 and WINDOW == 2 * bq and bq % CHUNK == 0 and n_heads % hps == 0
    nq = seq // bq
    ng = n_heads // hps
    band = WINDOW + CHUNK
    span = -(-(band + CHUNK - 1) // LANES) * LANES
    offs = jnp.arange(span) - (CHUNK - 1)
    vals = _rel_bias(rel_bias, WINDOW - offs).reshape(ng, hps, span)

    def kv(sec, p):
        return pl.BlockSpec((bq, hps * HEAD_DIM),
                            lambda g, b, i: (b * nq + jnp.maximum(i - 2 + p, 0), sec * ng + g))

    return pl.pallas_call(
        functools.partial(_attn_prompt_kernel, bq, hps, HEAD_DIM ** -0.5),
        grid=(ng, batch, nq),
        in_specs=[pl.BlockSpec((bq, hps * HEAD_DIM), lambda g, b, i: (b * nq + i, g)),
                  kv(1, 0), kv(1, 1), kv(1, 2), kv(2, 0), kv(2, 1), kv(2, 2),
                  pl.BlockSpec((1, hps, span), lambda g, b, i: (g, 0, 0))],
        out_specs=pl.BlockSpec((bq, hps * HEAD_DIM), lambda g, b, i: (b * nq + i, g)),
        out_shape=jax.ShapeDtypeStruct((qkv.shape[0], d), BF16),
        scratch_shapes=[pltpu.VMEM((hps, CHUNK, band), F32),
                        pltpu.VMEM((3 * bq, hps * HEAD_DIM), BF16),
                        pltpu.VMEM((3 * bq, hps * HEAD_DIM), BF16)],
        compiler_params=_params(),
        name="band_attention_prompt",
    )(qkv, qkv, qkv, qkv, qkv, qkv, qkv, vals)


def _attn_sample_kernel(n_heads, lc, scale, qkv_ref, ck_ref, cv_ref, bias_ref, base_ref, o_ref):
    del base_ref
    d = n_heads * HEAD_DIM

    def cols(sec, h):
        return slice(sec * d + h * HEAD_DIM, sec * d + (h + 1) * HEAD_DIM)

    for h0 in range(0, n_heads, ATTN_HEADS_INTERLEAVED):
        heads = range(h0, min(h0 + ATTN_HEADS_INTERLEAVED, n_heads))
        sc = [_dot_nt(qkv_ref[:, cols(0, h)], ck_ref[0, 0, :, h, :].astype(BF16)) * scale
              + bias_ref[h, :, 0:lc] for h in heads]
        sn = [_dot_nt(qkv_ref[:, cols(0, h)], qkv_ref[:, cols(1, h)]) * scale + bias_ref[h, :, lc:]
              for h in heads]
        m = [jnp.maximum(jnp.max(a, axis=-1, keepdims=True), jnp.max(b, axis=-1, keepdims=True))
             for a, b in zip(sc, sn)]
        ec = [jnp.exp(a - mh) for a, mh in zip(sc, m)]
        en = [jnp.exp(b - mh) for b, mh in zip(sn, m)]
        inv = [1.0 / (jnp.sum(a, axis=-1, keepdims=True) + jnp.sum(b, axis=-1, keepdims=True))
               for a, b in zip(ec, en)]
        for h, a, b, ih in zip(heads, ec, en, inv):
            o = (_dot((a * ih).astype(BF16), cv_ref[0, 0, :, h, :].astype(BF16))
                 + _dot((b * ih).astype(BF16), qkv_ref[:, cols(2, h)]))
            o_ref[:, cols(0, h)] = o.astype(o_ref.dtype)


def band_attention_sample(qkv, row0, batch, seq, cache_k, cache_v, layer, rel_bias, base):
    d = qkv.shape[1] // 3
    n_heads = d // HEAD_DIM
    lc = cache_k.shape[2]
    assert row0 % seq == 0 and lc % 128 == 0
    blk0 = row0 // seq
    dist = jnp.arange(seq)[:, None] + lc - jnp.arange(lc + seq)[None, :]
    bias = _rel_bias(rel_bias, dist)
    cache_spec = pl.BlockSpec((1, 1, lc, n_heads, HEAD_DIM), lambda b: (layer, b, 0, 0, 0))
    return pl.pallas_call(
        functools.partial(_attn_sample_kernel, n_heads, lc, HEAD_DIM ** -0.5),
        grid=(batch,),
        in_specs=[pl.BlockSpec((seq, 3 * d), lambda b: (blk0 + b, 0)),
                  cache_spec, cache_spec,
                  pl.BlockSpec(bias.shape, lambda b: (0, 0, 0)),
                  pl.BlockSpec(memory_space=pl.ANY)],
        out_specs=pl.BlockSpec((seq, d), lambda b: (blk0 + b, 0)),
        out_shape=jax.ShapeDtypeStruct(base.shape, base.dtype),
        input_output_aliases={4: 0},
        compiler_params=_params(),
        name="band_attention_sample",
    )(qkv, cache_k, cache_v, bias, base)


def _first_max(vals, idx, sentinel):
    m = jnp.max(vals, axis=0, keepdims=True)
    first = jnp.min(jnp.where(vals == m, idx, sentinel), axis=0, keepdims=True)
    return m, first


def _stack_rows(rows, lanes):
    ridx = lax.broadcasted_iota(I32, (len(rows), lanes), 0)
    out = jnp.zeros((len(rows), lanes), rows[0].dtype)
    for r, row in enumerate(rows):
        out = jnp.where(ridx == r, row, out)
    return out


def _router_kernel(n_experts, bt, x_ref, wt_ref, bias_ref, e_ref, gate_ref, rank_ref, cnt_ref,
                   tile_cnt_ref, carry_ref):
    i = pl.program_id(0)

    @pl.when(i == 0)
    def _():
        carry_ref[...] = jnp.zeros_like(carry_ref)

    per = n_experts // N_GROUPS
    logits = _dot_nt(wt_ref[...], x_ref[...])
    scores = _sigmoid(logits)
    sel = scores + bias_ref[:, 0:1]

    sub = lax.broadcasted_iota(I32, (per, bt), 0)
    grp_rows = []
    for g in range(N_GROUPS):
        blk = sel[g * per:(g + 1) * per]
        m1, i1 = _first_max(blk, sub, per)
        m2 = jnp.max(jnp.where(sub == i1, -jnp.inf, blk), axis=0, keepdims=True)
        grp_rows.append(m1 + m2)
    grp = _stack_rows(grp_rows, bt)
    gidx = lax.broadcasted_iota(I32, (N_GROUPS, bt), 0)
    keep = jnp.zeros((N_GROUPS, bt), F32)
    for _ in range(TOPK_GROUPS):
        _, first = _first_max(grp, gidx, N_GROUPS)
        hit = gidx == first
        keep = jnp.where(hit, 1.0, keep)
        grp = jnp.where(hit, -jnp.inf, grp)
    masked = jnp.concatenate(
        [jnp.where(keep[g:g + 1] > 0.0, sel[g * per:(g + 1) * per], -jnp.inf)
         for g in range(N_GROUPS)], axis=0)

    eidx = lax.broadcasted_iota(I32, (n_experts, bt), 0)
    chosen = jnp.zeros((n_experts, bt), F32)
    picks = []
    gates = []
    for _ in range(TOP_K):
        _, first = _first_max(masked, eidx, n_experts)
        hit = eidx == first
        picks.append(first)
        gates.append(jnp.sum(jnp.where(hit, scores, 0.0), axis=0, keepdims=True))
        chosen = jnp.where(hit, 1.0, chosen)
        masked = jnp.where(hit, -jnp.inf, masked)
    gate = _stack_rows(gates, bt)
    gate = gate / jnp.sum(gate, axis=0, keepdims=True) * ROUTE_SCALE

    earlier = (lax.broadcasted_iota(I32, (bt, bt), 0) < lax.broadcasted_iota(I32, (bt, bt), 1))
    before = _dot(chosen.astype(BF16), jnp.where(earlier, 1.0, 0.0).astype(BF16)) + carry_ref[:, 0:1]
    ranks = [jnp.sum(jnp.where(eidx == p, before, 0.0), axis=0, keepdims=True) for p in picks]

    e_ref[...] = _stack_rows(picks, bt)
    gate_ref[...] = gate
    rank_ref[...] = _stack_rows(ranks, bt).astype(I32)
    in_tile = jnp.sum(chosen, axis=1, keepdims=True)
    tile_cnt_ref[0] = jnp.broadcast_to(in_tile, tile_cnt_ref.shape[1:])
    carry_ref[...] = carry_ref[...] + in_tile
    cnt_ref[...] = carry_ref[...]


def moe_router(x, w_router, router_bias, bt=ROUTER_BLOCK_T):
    n, d = x.shape
    n_experts = w_router.shape[1]
    bt = min(bt, n)
    assert n % bt == 0 and n_experts % N_GROUPS == 0
    lanes = 128
    tok = pl.BlockSpec((TOP_K, bt), lambda i: (0, i))
    e, gate, rank, cnt, tile_cnt = pl.pallas_call(
        functools.partial(_router_kernel, n_experts, bt),
        grid=(n // bt,),
        in_specs=[pl.BlockSpec((bt, d), lambda i: (i, 0)),
                  pl.BlockSpec((n_experts, d), lambda i: (0, 0)),
                  pl.BlockSpec((n_experts, lanes), lambda i: (0, 0))],
        out_specs=[tok, tok, tok, pl.BlockSpec((n_experts, lanes), lambda i: (0, 0)),
                   pl.BlockSpec((1, n_experts, lanes), lambda i: (i, 0, 0))],
        out_shape=[jax.ShapeDtypeStruct((TOP_K, n), I32), jax.ShapeDtypeStruct((TOP_K, n), F32),
                   jax.ShapeDtypeStruct((TOP_K, n), I32),
                   jax.ShapeDtypeStruct((n_experts, lanes), F32),
                   jax.ShapeDtypeStruct((n // bt, n_experts, lanes), F32)],
        scratch_shapes=[pltpu.VMEM((n_experts, lanes), F32)],
        compiler_params=_params(),
        name="moe_router",
    )(x, w_router.T.astype(BF16),
      jnp.broadcast_to(router_bias.astype(F32)[:, None], (n_experts, lanes)))
    return e, gate, rank, cnt[:, 0].astype(I32), tile_cnt[:, :, 0].astype(I32)


def _dispatch_kernel(bt, n_experts, sub, fill_start_ref, fill_len_ref, cnt_ref, local_ref, row_ref,
                     lp_ref, x_ref, xs_ref, sort_ref, zero_ref, sems, fill_sem):
    i = pl.program_id(0)
    last = pl.num_programs(0) - 1
    slot = i % 2
    piece_sizes = [1 << b for b in range(bt.bit_length() - 1, -1, -1)]

    def wait_slot(s):
        pltpu.make_async_copy(sort_ref.at[s], sort_ref.at[s], sems.at[s]).wait()

    @pl.when(i == 0)
    def _():
        zero_ref[...] = jnp.zeros_like(zero_ref)

        def fill(e, carry):
            base = fill_start_ref[e]

            def zero_copy(r):
                return pltpu.make_async_copy(
                    zero_ref, xs_ref.at[pl.ds(pl.multiple_of((base + r) * sub, sub), sub)], fill_sem)

            def start(r, c):
                zero_copy(r).start()
                return c

            def wait(r, c):
                zero_copy(r).wait()
                return c

            lax.fori_loop(0, fill_len_ref[e], start, 0)
            lax.fori_loop(0, fill_len_ref[e], wait, 0)
            return carry

        lax.fori_loop(0, n_experts, fill, 0)

    @pl.when(i >= 2)
    def _():
        wait_slot(slot)

    def place(t, carry):
        tile = x_ref[pl.ds(pl.multiple_of(t * sub, sub), sub), :]
        for k in range(TOP_K):
            at = pl.multiple_of(lp_ref[t * TOP_K + k] * sub, sub)
            sort_ref[slot, pl.ds(at, sub), :] = tile
        return carry

    lax.fori_loop(0, bt, place, 0, unroll=8)

    def send(e, carry):
        n_rows = cnt_ref[i * n_experts + e]
        src0 = local_ref[i * n_experts + e]
        dst0 = row_ref[i * n_experts + e]
        done = 0
        for p in piece_sizes:
            take = (n_rows & p) != 0

            @pl.when(take)
            def _(done=done, p=p):
                pltpu.make_async_copy(
                    sort_ref.at[slot, pl.ds(pl.multiple_of((src0 + done) * sub, sub), p * sub)],
                    xs_ref.at[pl.ds(pl.multiple_of((dst0 + done) * sub, sub), p * sub)],
                    sems.at[slot]).start()

            done = done + jnp.where(take, p, 0)
        return carry

    lax.fori_loop(0, n_experts, send, 0)

    @pl.when(i == last)
    def _():
        @pl.when(i >= 1)
        def _():
            wait_slot(1 - slot)
        wait_slot(slot)


def moe_dispatch(xpk, lp_flat, tile_cnt, tile_local, tile_row, fill_start, fill_len, rows,
                 bt=ROUTER_BLOCK_T):
    n = lp_flat.shape[0] // TOP_K
    sub = xpk.shape[0] // n
    bt = min(bt, n)
    assert n % bt == 0 and bt & (bt - 1) == 0
    n_experts = fill_start.shape[0]
    return pl.pallas_call(
        functools.partial(_dispatch_kernel, bt, n_experts, sub),
        grid_spec=pltpu.PrefetchScalarGridSpec(
            num_scalar_prefetch=5,
            grid=(n // bt,),
            in_specs=[pl.BlockSpec((bt * TOP_K,), lambda i, *_: (i,), memory_space=pltpu.SMEM),
                      pl.BlockSpec((bt * sub, LANES), lambda i, *_: (i, 0))],
            out_specs=pl.BlockSpec(memory_space=pl.ANY),
            scratch_shapes=[pltpu.VMEM((2, bt * TOP_K * sub, LANES), U32),
                            pltpu.VMEM((sub, LANES), U32),
                            pltpu.SemaphoreType.DMA((2,)), pltpu.SemaphoreType.DMA]),
        out_shape=jax.ShapeDtypeStruct((rows * sub, LANES), U32),
        compiler_params=_params(),
        name="moe_dispatch",
    )(fill_start, fill_len, tile_cnt.reshape(-1), tile_local.reshape(-1), tile_row.reshape(-1),
      lp_flat, xpk)


def _expert_kernel(blk, tile_e_ref, n_used_ref, xs_ref, wgu_ref, wdn_ref, ys_ref, wgu_b, wdn_b):
    i = pl.program_id(0)
    de = wdn_b.shape[0]
    changed = jnp.logical_or(i == 0, tile_e_ref[i] != tile_e_ref[jnp.maximum(i - 1, 0)])

    @pl.when(changed)
    def _():
        wgu_b[...] = wgu_ref[0, 0].astype(BF16)
        wdn_b[...] = wdn_ref[0, 0].astype(BF16)

    @pl.when(i < n_used_ref[0])
    def _():
        lo, hi = _unpack_pairs(_load_row_tiles(xs_ref, blk))
        x = jnp.concatenate([lo.astype(BF16), hi.astype(BF16)], axis=1)
        gu = _dot(x, wgu_b[...])
        a = _silu(gu[:, :de]) * gu[:, de:]
        _store_row_tiles(ys_ref, _pack_pairs(_dot(a.astype(BF16), wdn_b[...])))

    @pl.when(i >= n_used_ref[0])
    def _():
        ys_ref[...] = jnp.zeros_like(ys_ref)


def moe_experts(xs, tile_e, n_used, w_gu, w_dn, layer, blk=EXPERT_BLOCK):
    sub = w_gu.shape[2] // 2 // LANES
    d = 2 * sub * LANES
    rows = xs.shape[0] // sub
    n_tiles = rows // blk
    de = w_dn.shape[2]
    return pl.pallas_call(
        functools.partial(_expert_kernel, blk),
        grid_spec=pltpu.PrefetchScalarGridSpec(
            num_scalar_prefetch=2,
            grid=(n_tiles,),
            in_specs=[pl.BlockSpec((blk * sub, LANES),
                                   lambda i, te, nu: (jnp.minimum(i, nu[0] - 1), 0)),
                      pl.BlockSpec((1, 1, d, 2 * de), lambda i, te, nu: (layer, te[i], 0, 0)),
                      pl.BlockSpec((1, 1, de, d), lambda i, te, nu: (layer, te[i], 0, 0))],
            out_specs=pl.BlockSpec((blk * sub, LANES), lambda i, te, nu: (i, 0)),
            scratch_shapes=[pltpu.VMEM((d, 2 * de), BF16), pltpu.VMEM((de, d), BF16)]),
        out_shape=jax.ShapeDtypeStruct((rows * sub, LANES), U32),
        compiler_params=_params(),
        name="moe_experts",
    )(tile_e, n_used, xs, w_gu, w_dn)


def _combine_kernel(alpha, bt, dest_ref, dest_next_ref, ys_ref, gate_ref, x_ref, xb_ref, wsgu_ref,
                    wsdn_ref, g_ref, b_ref, xo_ref, xbo_ref, buf_a, buf_b, sem_a, sem_b):
    i = pl.program_id(0)
    de = wsdn_ref.shape[0]
    sub = ys_ref.shape[1]
    per_tile = bt * TOP_K

    def row_copy(idx_ref, base, buf, sem, t, k):
        return pltpu.make_async_copy(ys_ref.at[idx_ref[base + t * TOP_K + k]],
                                     buf.at[k, pl.ds(t * sub, sub)], sem)

    def issue_unrolled(idx_ref, base, buf, sem):
        for t in range(bt):
            for k in range(TOP_K):
                row_copy(idx_ref, base, buf, sem, t, k).start(priority=k % 2)

    def drain(buf, other, sem):
        pltpu.make_async_copy(other, buf, sem).wait()

    def compute(r0, buf):
        rows = pl.ds(r0, bt)
        sgu = _dot(xb_ref[rows, :], wsgu_ref[...])
        shared = _dot((_silu(sgu[:, :de]) * sgu[:, de:]).astype(BF16), wsdn_ref[...])
        routed_lo, routed_hi = None, None
        for k in range(TOP_K):
            gate = gate_ref[rows, k:k + 1]
            lo, hi = _unpack_pairs(_load_row_tiles(buf.at[k], bt))
            routed_lo = gate * lo if k == 0 else routed_lo + gate * lo
            routed_hi = gate * hi if k == 0 else routed_hi + gate * hi
        routed = jnp.concatenate([routed_lo, routed_hi], axis=1)
        out = _layer_norm(alpha * x_ref[rows, :] + (routed + shared), g_ref[...], b_ref[...])
        xo_ref[rows, :] = out
        xbo_ref[rows, :] = out.astype(BF16)

    @pl.when(i == 0)
    def _():
        def body(t, carry):
            for k in range(TOP_K):
                row_copy(dest_ref, 0, buf_a, sem_a, t, k).start()
            return carry
        lax.fori_loop(0, bt, body, 0)

    drain(buf_a, buf_b, sem_a)
    issue_unrolled(dest_ref, per_tile, buf_b, sem_b)
    compute(0, buf_a)
    drain(buf_b, buf_a, sem_b)
    issue_unrolled(dest_next_ref, 0, buf_a, sem_a)
    compute(bt, buf_b)

    @pl.when(i == pl.num_programs(0) - 1)
    def _():
        drain(buf_a, buf_b, sem_a)


def moe_combine_ln(ys, dest_flat, gate_tm, x, xb, ws_gu, ws_dn, g, b, alpha, bt=COMBINE_BLOCK_T):
    n, d = x.shape
    bt = min(bt, n // 2)
    assert n % (2 * bt) == 0
    n_tiles = n // bt
    de = ws_dn.shape[0]
    sub = ys.shape[1]
    row = pl.BlockSpec((2 * bt, d), lambda i: (i, 0))
    vec = pl.BlockSpec((1, d), lambda i: (0, 0))
    return pl.pallas_call(
        functools.partial(_combine_kernel, alpha, bt),
        grid=(n_tiles // 2,),
        in_specs=[pl.BlockSpec((2 * bt * TOP_K,), lambda i: (i,), memory_space=pltpu.SMEM),
                  pl.BlockSpec((bt * TOP_K,), lambda i: (jnp.minimum(2 * i + 2, n_tiles - 1),),
                               memory_space=pltpu.SMEM),
                  pl.BlockSpec(memory_space=pl.ANY),
                  pl.BlockSpec((2 * bt, TOP_K), lambda i: (i, 0)),
                  row, row,
                  pl.BlockSpec((d, 2 * de), lambda i: (0, 0)),
                  pl.BlockSpec((de, d), lambda i: (0, 0)),
                  vec, vec],
        out_specs=[row, row],
        out_shape=[jax.ShapeDtypeStruct((n, d), F32), jax.ShapeDtypeStruct((n, d), BF16)],
        scratch_shapes=[pltpu.VMEM((TOP_K, bt * sub, LANES), U32),
                        pltpu.VMEM((TOP_K, bt * sub, LANES), U32),
                        pltpu.SemaphoreType.DMA, pltpu.SemaphoreType.DMA],
        compiler_params=_params(),
        name="moe_combine_ln",
    )(dest_flat, dest_flat, ys, gate_tm, x, xb, ws_gu, ws_dn, g.reshape(1, d), b.reshape(1, d))


def moe_ffn_ln(x, xb, xpk, w_router, router_bias, w_gu, w_dn, layer, ws_gu, ws_dn, g, b, alpha):
    n, d = x.shape
    n_experts = w_router.shape[1]
    blk = EXPERT_BLOCK
    top_e, gate, rank, counts, tile_cnt = moe_router(xb, w_router, router_bias)
    padded = (counts + blk - 1) // blk * blk
    pend = jnp.cumsum(padded)
    pstart = pend - padded
    hot = top_e[:, :, None] == jnp.arange(n_experts, dtype=I32)
    dest = jnp.sum(jnp.where(hot, pstart, 0), axis=-1) + rank
    dest_flat = dest.T.reshape(-1)
    earlier_tiles = jnp.cumsum(tile_cnt, axis=0) - tile_cnt
    tile_local = jnp.cumsum(tile_cnt, axis=1) - tile_cnt
    tile_row = pstart[None, :] + earlier_tiles
    per_token = jnp.repeat(tile_local - earlier_tiles, n // tile_cnt.shape[0], axis=0)
    lp = jnp.sum(jnp.where(hot, per_token[None], 0), axis=-1) + rank
    lp_flat = lp.T.reshape(-1)
    n_tiles = (n * TOP_K + n_experts * (blk - 1) + blk - 1) // blk
    tile_row0 = jnp.arange(n_tiles, dtype=I32) * blk
    tile_e = jnp.minimum(jnp.sum((pend[None, :] <= tile_row0[:, None]).astype(I32), axis=1),
                         n_experts - 1)
    n_used = (pend[-1:] // blk).astype(I32)
    sub = d // 2 // LANES
    rows = n_tiles * blk
    xs = moe_dispatch(xpk, lp_flat, tile_cnt, tile_local, tile_row, (pstart + counts).astype(I32),
                      (padded - counts).astype(I32), rows)
    ys = moe_experts(xs, tile_e, n_used, w_gu, w_dn, layer, blk)
    ys = ys.reshape(rows, sub, LANES)
    return moe_combine_ln(ys, dest_flat, gate.T, x, xb, ws_gu, ws_dn, g, b, alpha)


def kernel(x_prompt, x_sample, state_hgrn, state_pool, cache_k, cache_v, ln_g, ln_b, w_in_even,
           lb_even, gnorm_even, pool_w_even, pool_scale_even, w_out_even, w_qkv_odd, rel_bias_odd,
           w_o_odd, w_router, router_bias, w_gu, w_dn, ws_gu, ws_dn):
    batch, seq, d = x_prompt.shape
    dec_batch, dec_seq, _ = x_sample.shape
    depth = ln_g.shape[0]
    n_p = batch * seq
    n_s = dec_batch * dec_seq
    width = w_in_even.shape[2] // 5
    n_heads_a = width // HEAD_DIM
    n_heads_c = d // HEAD_DIM
    alpha = (2 * depth) ** 0.25
    assert dec_seq >= POOL_PREV and seq >= max(POOL_PREV, WINDOW)

    x = jnp.concatenate([x_prompt.reshape(n_p, d), x_sample.reshape(n_s, d)], axis=0)
    xb = x.astype(BF16)

    p = jax.nn.softmax(lb_even.astype(F32), axis=0)
    lbs = jnp.cumsum(p, axis=0) - p[0]

    hgrn_p, pool_p, k_p, v_p, hgrn_s, pool_s, k_s, v_s = [], [], [], [], [], [], [], []
    for l in range(depth):
        j = l // 2
        if l % 2 == 0:
            proj = matmul(xb, w_in_even[j].astype(BF16))
            zero_state = jnp.zeros((batch, n_heads_a, HEAD_DIM, HEAD_DIM), F32)
            mixed, sp = hgrn_mixer(proj, 0, batch, seq, zero_state, lbs[j], gnorm_even[j])
            mixed, ss = hgrn_mixer(proj, n_p, dec_batch, dec_seq,
                                   jnp.swapaxes(state_hgrn[j].astype(F32), -1, -2), lbs[j],
                                   gnorm_even[j], mixed)
            pw = pool_w_even[j].astype(BF16)
            mixed = pool_mixer(proj, 0, batch, seq, 0, jnp.zeros((batch, POOL_CARRY, width), F32),
                               pw, pool_scale_even[j], mixed)
            prev_s = jnp.pad(state_pool[j].astype(F32),
                             ((0, 0), (POOL_CARRY - POOL_PREV, 0), (0, 0)))
            mixed = pool_mixer(proj, n_p, dec_batch, dec_seq, PAST_LEN, prev_s, pw,
                               pool_scale_even[j], mixed)
            w_mix = w_out_even[j]
            hgrn_p.append(jnp.swapaxes(sp, -1, -2))
            hgrn_s.append(jnp.swapaxes(ss, -1, -2))
            pool_p.append(jnp.stack([proj[(b + 1) * seq - POOL_PREV:(b + 1) * seq, 4 * width:]
                                     for b in range(batch)]))
            pool_s.append(proj[n_p:].reshape(dec_batch, dec_seq, 5 * width)
                          [:, dec_seq - POOL_PREV:, 4 * width:])
        else:
            qkv = matmul(xb, w_qkv_odd[j].astype(BF16), out_dtype=BF16)
            mixed = band_attention_prompt(qkv, batch, seq, rel_bias_odd[j])
            mixed = band_attention_sample(qkv, n_p, dec_batch, dec_seq, cache_k, cache_v, j,
                                          rel_bias_odd[j], mixed)
            w_mix = w_o_odd[j]
            keep = min(WINDOW, seq)
            tail_p = jnp.stack([qkv[(b + 1) * seq - keep:(b + 1) * seq]
                                for b in range(batch)]).astype(F32)
            new_s = qkv[n_p:].reshape(dec_batch, dec_seq, 3 * d).astype(F32)
            k_p.append(tail_p[:, :, d:2 * d].reshape(batch, keep, n_heads_c, HEAD_DIM))
            v_p.append(tail_p[:, :, 2 * d:].reshape(batch, keep, n_heads_c, HEAD_DIM))
            k_s.append(new_s[:, :, d:2 * d].reshape(dec_batch, dec_seq, n_heads_c, HEAD_DIM))
            v_s.append(new_s[:, :, 2 * d:].reshape(dec_batch, dec_seq, n_heads_c, HEAD_DIM))
        x, xb, xpk = matmul_residual_ln(mixed, w_mix.astype(BF16), x, ln_g[l, 0], ln_b[l, 0], alpha)
        x, xb = moe_ffn_ln(x, xb, xpk, w_router[l], router_bias[l], w_gu, w_dn, l,
                           ws_gu[l].astype(BF16), ws_dn[l].astype(BF16), ln_g[l, 1], ln_b[l, 1],
                           alpha)

    y_p = x[:n_p].reshape(batch, seq, d)
    y_s = x[n_p:].reshape(dec_batch, dec_seq, d)
    return (y_p, y_s, jnp.stack(hgrn_p), jnp.stack(pool_p), jnp.stack(k_p), jnp.stack(v_p),
            jnp.stack(hgrn_s), jnp.stack(pool_s), jnp.stack(k_s), jnp.stack(v_s))
```

```python
import functools

import numpy as np
import jax
import jax.numpy as jnp
from jax import lax
from jax.experimental import pallas as pl
from jax.experimental.pallas import tpu as pltpu

F32 = jnp.float32
BF16 = jnp.bfloat16
I32 = jnp.int32
U32 = jnp.uint32

CHUNK = 64
HEAD_DIM = 128
LANES = 128
POOL_WINDOWS = (2, 4, 8, 16)
POOL_PREV = max(POOL_WINDOWS) - 1
POOL_CARRY = POOL_PREV + 1
LEFT_CHUNKS = 8
WINDOW = LEFT_CHUNKS * CHUNK
MAX_REL = 128
N_GROUPS = 8
TOPK_GROUPS = 4
TOP_K = 8
ROUTE_SCALE = 2.5
LN_EPS = 1e-5
RMS_EPS = 1e-6
PAST_LEN = 1024
NEG = -1e30

VMEM_LIMIT_BYTES = 56 * 1024 * 1024
MM_BLOCK_M = 1024
MM_BLOCK_N = 1024
LN_BLOCK_M = 512
HGRN_BLOCK_T = 128
HGRN_HEADS_INTERLEAVED = 8
ATTN_BLOCK_Q = 256
ATTN_HEADS_PER_STEP = 4
ATTN_HEADS_INTERLEAVED = 4
POOL_BLOCK_T = 512
ROUTER_BLOCK_T = 512
DISPATCH_BLOCK_T = 128
EXPERT_BLOCK = 512
COMBINE_BLOCK_T = 128


def _params(**kw):
    return pltpu.CompilerParams(vmem_limit_bytes=VMEM_LIMIT_BYTES, **kw)


def _sigmoid(x):
    return 1.0 / (1.0 + jnp.exp(-x))


def _silu(x):
    return x * _sigmoid(x)


def _dot(a, b):
    return jnp.dot(a, b, preferred_element_type=F32)


def _dot_nt(a, b):
    return lax.dot_general(a, b, (((1,), (1,)), ((), ())), preferred_element_type=F32)


def _dot_tn(a, b):
    return lax.dot_general(a, b, (((0,), (0,)), ((), ())), preferred_element_type=F32)


def _layer_norm(y, g, b):
    mu = jnp.mean(y, axis=-1, keepdims=True)
    yc = y - mu
    var = jnp.mean(yc * yc, axis=-1, keepdims=True)
    return yc * lax.rsqrt(var + LN_EPS) * g + b


def _mm_kernel(x_ref, w_ref, o_ref):
    o_ref[...] = _dot(x_ref[...], w_ref[...]).astype(o_ref.dtype)


def matmul(x, w, out_dtype=F32, bm=MM_BLOCK_M, bn=MM_BLOCK_N):
    m, k = x.shape
    n = w.shape[1]
    bm, bn = min(bm, m), min(bn, n)
    assert m % bm == 0 and n % bn == 0
    return pl.pallas_call(
        _mm_kernel,
        grid=(m // bm, n // bn),
        in_specs=[pl.BlockSpec((bm, k), lambda i, j: (i, 0)),
                  pl.BlockSpec((k, bn), lambda i, j: (0, j))],
        out_specs=pl.BlockSpec((bm, bn), lambda i, j: (i, j)),
        out_shape=jax.ShapeDtypeStruct((m, n), out_dtype),
        compiler_params=_params(),
        name="matmul",
    )(x, w)


def _pack_pairs(x):
    n = x.shape[1] // 2
    lo = lax.bitcast_convert_type(x[:, :n].astype(jnp.bfloat16).astype(F32), U32)
    hi = lax.bitcast_convert_type(x[:, n:].astype(jnp.bfloat16).astype(F32), U32)
    return (lo >> 16) | hi


def _unpack_pairs(u):
    lo = lax.bitcast_convert_type(u << 16, F32)
    hi = lax.bitcast_convert_type(u & jnp.uint32(0xFFFF0000), F32)
    return lo, hi


def _store_row_tiles(ref, u):
    rows = u.shape[0]
    sub = ref.shape[0] // rows
    for c in range(sub):
        ref[pl.ds(c, rows, stride=sub), :] = u[:, c * LANES:(c + 1) * LANES]


def _load_row_tiles(ref, rows):
    sub = ref.shape[0] // rows
    return jnp.concatenate([ref[pl.ds(c, rows, stride=sub), :] for c in range(sub)], axis=1)


def _mm_ln_kernel(alpha, a_ref, w_ref, res_ref, g_ref, b_ref, x_ref, xb_ref, xpk_ref):
    y = alpha * res_ref[...] + _dot(a_ref[...], w_ref[...])
    out = _layer_norm(y, g_ref[...], b_ref[...])
    x_ref[...] = out
    xb_ref[...] = out.astype(BF16)
    _store_row_tiles(xpk_ref, _pack_pairs(out))


def _mm_ln_split_kernel(alpha, n_first, a_ref, w_ref, res_a_ref, res_b_ref, g_ref, b_ref, x_ref,
                        xb_ref, xpk_ref):
    first = pl.program_id(0) < n_first
    res = jnp.where(first, res_a_ref[...], res_b_ref[...])
    y = alpha * res + _dot(a_ref[...], w_ref[...])
    out = _layer_norm(y, g_ref[...], b_ref[...])
    x_ref[...] = out
    xb_ref[...] = out.astype(BF16)
    _store_row_tiles(xpk_ref, _pack_pairs(out))


def matmul_residual_ln(a, w, res, g, b, alpha, bm=LN_BLOCK_M):
    m, k = a.shape
    d = w.shape[1]
    bm = min(bm, m)
    assert m % bm == 0 and d % (2 * LANES) == 0
    sub = d // 2 // LANES
    if isinstance(res, tuple):
        res_a, res_b = res
        n_first = res_a.shape[0] // bm
        assert res_a.shape[0] % bm == 0 and res_b.shape[0] % bm == 0
        body = functools.partial(_mm_ln_split_kernel, alpha, n_first)
        res_args = [res_a, res_b]
        res_specs = [pl.BlockSpec((bm, d), lambda i: (jnp.minimum(i, n_first - 1), 0)),
                     pl.BlockSpec((bm, d), lambda i: (jnp.maximum(i - n_first, 0), 0))]
    else:
        body = functools.partial(_mm_ln_kernel, alpha)
        res_args = [res]
        res_specs = [pl.BlockSpec((bm, d), lambda i: (i, 0))]
    return pl.pallas_call(
        body,
        grid=(m // bm,),
        in_specs=[pl.BlockSpec((bm, k), lambda i: (i, 0)),
                  pl.BlockSpec((k, d), lambda i: (0, 0)),
                  *res_specs,
                  pl.BlockSpec((1, d), lambda i: (0, 0)),
                  pl.BlockSpec((1, d), lambda i: (0, 0))],
        out_specs=[pl.BlockSpec((bm, d), lambda i: (i, 0)),
                   pl.BlockSpec((bm, d), lambda i: (i, 0)),
                   pl.BlockSpec((bm * sub, LANES), lambda i: (i, 0))],
        out_shape=[jax.ShapeDtypeStruct((m, d), F32), jax.ShapeDtypeStruct((m, d), BF16),
                   jax.ShapeDtypeStruct((m * sub, LANES), U32)],
        compiler_params=_params(),
        name="matmul_residual_ln",
    )(a, w, *res_args, g.reshape(1, d), b.reshape(1, d))


def _hgrn_tables(c):
    t = np.arange(c)
    levels = []
    b = c // 2
    while b >= 1:
        levels.append(b)
        b //= 2
    sel = [(t[None, :] <= t[:, None]).astype(np.float32)]
    lvl = np.where(np.eye(c, dtype=bool), 0, -1).astype(np.int32)
    for l, b in enumerate(levels):
        blk = t // b
        odd = (blk % 2) == 1
        ref_row = np.where(odd, blk * b - 1, blk * b + b - 1)
        sel.append((t[None, :] <= ref_row[:, None]).astype(np.float32))
        same = (t[:, None] // (2 * b)) == (t[None, :] // (2 * b))
        lvl = np.where(same & odd[:, None] & (~odd)[None, :], l + 1, lvl)
    sel = np.concatenate(sel, axis=0)
    return np.concatenate([sel, sel, sel], axis=1), lvl


def _hgrn_kernel(n_heads, c, q_ref, f_ref, i_ref, g_ref, s0_ref, lb_ref, gn_ref, sel_ref, lvl_ref,
                 o_ref, sfin_ref, state_ref, cums_ref, qw_ref, kw_ref):
    j = pl.program_id(1)

    @pl.when(j == 0)
    def _():
        state_ref[...] = s0_ref[0]

    n_lvl = qw_ref.shape[0]
    lb = lb_ref[...]
    f = lb + (1.0 - lb) * _sigmoid(f_ref[...])
    k = 1.0 - f
    q = _silu(q_ref[...])
    lg = jnp.log(f)
    hi = lg.astype(BF16)
    rest = lg - hi.astype(F32)
    mid = rest.astype(BF16)
    lo = (rest - mid.astype(F32)).astype(BF16)
    cums_ref[...] = _dot(sel_ref[...], jnp.concatenate([hi, mid, lo], axis=0))
    cum = cums_ref[0:c]
    qw_ref[0] = q.astype(BF16)
    kw_ref[0] = k.astype(BF16)
    for l in range(1, n_lvl):
        w = jnp.exp(-jnp.abs(cum - cums_ref[l * c:(l + 1) * c]))
        qw_ref[l] = (q * w).astype(BF16)
        kw_ref[l] = (k * w).astype(BF16)
    last = cum[c - 1:c, :]
    q_in = (q * jnp.exp(cum)).astype(BF16)
    k_end = (k * jnp.exp(last - cum)).astype(BF16)
    decay = jnp.exp(last)
    lvl = lvl_ref[...]
    for h0 in range(0, n_heads, HGRN_HEADS_INTERLEAVED):
        heads = list(range(h0, min(h0 + HGRN_HEADS_INTERLEAVED, n_heads)))
        cols = [slice(h * HEAD_DIM, (h + 1) * HEAD_DIM) for h in heads]
        st = [state_ref[h] for h in heads]
        o = [_dot_nt(q_in[:, hs], s.astype(BF16)) for hs, s in zip(cols, st)]
        scores = [jnp.zeros((c, c), F32) for _ in heads]
        for l in range(n_lvl):
            scores = [jnp.where(lvl == l, _dot_nt(qw_ref[l, :, hs], kw_ref[l, :, hs]), sc)
                      for hs, sc in zip(cols, scores)]
        v = [i_ref[:, hs].astype(BF16) for hs in cols]
        o = [oh + _dot(sc.astype(BF16), vh) for oh, sc, vh in zip(o, scores, v)]
        for h, hs, s, vh in zip(heads, cols, st, v):
            state_ref[h] = s * decay[:, hs] + _dot_tn(vh, k_end[:, hs])
        o = [oh * lax.rsqrt(jnp.mean(oh * oh, axis=-1, keepdims=True) + RMS_EPS) * gn_ref[...]
             for oh in o]
        for hs, oh in zip(cols, o):
            o_ref[:, hs] = (oh * _silu(g_ref[:, hs])).astype(o_ref.dtype)

    @pl.when(j == pl.num_programs(1) - 1)
    def _():
        sfin_ref[0] = state_ref[...]


def _fill_in_place(body, n_inputs, base):
    if base is None:
        return body
    return lambda *refs: body(*refs[:n_inputs], *refs[n_inputs + 1:])


def hgrn_mixer(proj, row0, batch, seq, s0_t, lb, gnorm, base=None):
    width = proj.shape[1] // 5
    n_heads = width // HEAD_DIM
    c = HGRN_BLOCK_T if seq % HGRN_BLOCK_T == 0 else seq
    n = seq // c
    assert row0 % c == 0 and c & (c - 1) == 0
    blk0 = row0 // c
    sel, lvl = _hgrn_tables(c)
    n_lvl = sel.shape[0] // c

    def sec(s):
        return pl.BlockSpec((c, width), lambda b, j: (blk0 + b * n + j, s))

    state_spec = pl.BlockSpec((1, n_heads, HEAD_DIM, HEAD_DIM), lambda b, j: (b, 0, 0, 0))
    inputs = [proj, proj, proj, proj, s0_t, lb.reshape(1, width), gnorm.reshape(1, HEAD_DIM),
              jnp.asarray(sel).astype(BF16), jnp.asarray(lvl)]
    in_specs = [sec(0), sec(1), sec(2), sec(3), state_spec,
                pl.BlockSpec((1, width), lambda b, j: (0, 0)),
                pl.BlockSpec((1, HEAD_DIM), lambda b, j: (0, 0)),
                pl.BlockSpec(sel.shape, lambda b, j: (0, 0)),
                pl.BlockSpec(lvl.shape, lambda b, j: (0, 0))]
    return pl.pallas_call(
        _fill_in_place(functools.partial(_hgrn_kernel, n_heads, c), len(inputs), base),
        grid=(batch, n),
        in_specs=in_specs + ([] if base is None else [pl.BlockSpec(memory_space=pl.ANY)]),
        out_specs=[pl.BlockSpec((c, width), lambda b, j: (blk0 + b * n + j, 0)), state_spec],
        out_shape=[jax.ShapeDtypeStruct((proj.shape[0], 2 * width), BF16),
                   jax.ShapeDtypeStruct((batch, n_heads, HEAD_DIM, HEAD_DIM), F32)],
        input_output_aliases={} if base is None else {len(inputs): 0},
        scratch_shapes=[pltpu.VMEM((n_heads, HEAD_DIM, HEAD_DIM), F32),
                        pltpu.VMEM((n_lvl * c, width), F32),
                        pltpu.VMEM((n_lvl, c, width), BF16),
                        pltpu.VMEM((n_lvl, c, width), BF16)],
        compiler_params=_params(),
        name="hgrn_mixer",
    )(*inputs, *([] if base is None else [base]))


def _pool_kernel(pos0, bt, group, u_ref, prev_ref, w_ref, scale_ref, o_ref, ext_ref):
    j = pl.program_id(1)

    @pl.when(j == 0)
    def _():
        ext_ref[0:POOL_CARRY] = prev_ref[0]

    u = u_ref[...]
    ext_ref[POOL_CARRY:POOL_CARRY + bt] = u
    pos = pos0 + j * bt + lax.broadcasted_iota(I32, (bt, 1), 0)
    for gi, win in enumerate(POOL_WINDOWS):
        cs = slice(gi * group, (gi + 1) * group)
        s = ext_ref[:, cs]
        step = 1
        while step < win:
            s = s + pltpu.roll(s, step, axis=0)
            step *= 2
        cnt = jnp.minimum(pos + 1, win).astype(F32)
        d = s[POOL_CARRY:] / cnt - u[:, cs]
        y = _dot(d.astype(BF16), w_ref[gi]) * scale_ref[:, cs]
        o_ref[:, cs] = y.astype(o_ref.dtype)
    ext_ref[0:POOL_CARRY] = ext_ref[bt:bt + POOL_CARRY]


def pool_mixer(proj, row0, batch, seq, pos0, prev, pool_w, pool_scale, base):
    width = proj.shape[1] // 5
    group = width // len(POOL_WINDOWS)
    bt = min(POOL_BLOCK_T, seq)
    n = seq // bt
    assert seq % bt == 0 and row0 % bt == 0 and sum(POOL_WINDOWS) // 2 <= POOL_CARRY
    blk0 = row0 // bt
    inputs = [proj, prev, pool_w, pool_scale.reshape(1, width)]
    return pl.pallas_call(
        _fill_in_place(functools.partial(_pool_kernel, pos0, bt, group), len(inputs), base),
        grid=(batch, n),
        in_specs=[pl.BlockSpec((bt, width), lambda b, j: (blk0 + b * n + j, 4)),
                  pl.BlockSpec((1, POOL_CARRY, width), lambda b, j: (b, 0, 0)),
                  pl.BlockSpec(pool_w.shape, lambda b, j: (0, 0, 0)),
                  pl.BlockSpec((1, width), lambda b, j: (0, 0)),
                  pl.BlockSpec(memory_space=pl.ANY)],
        out_specs=pl.BlockSpec((bt, width), lambda b, j: (blk0 + b * n + j, 1)),
        out_shape=jax.ShapeDtypeStruct(base.shape, base.dtype),
        input_output_aliases={len(inputs): 0},
        scratch_shapes=[pltpu.VMEM((bt + POOL_CARRY, width), F32)],
        compiler_params=_params(),
        name="pool_mixer",
    )(*inputs, base)


def _rel_bias(table, dist):
    idx = jnp.clip(dist, -(CHUNK - 1), MAX_REL) + (CHUNK - 1)
    return table[:, idx].astype(F32)


def _attn_prompt_kernel(bq, hps, scale, q_ref, k0_ref, k1_ref, k2_ref, v0_ref, v1_ref, v2_ref,
                        vals_ref, o_ref, bias_ref, kcat_ref, vcat_ref):
    b = pl.program_id(1)
    i = pl.program_id(2)
    band = WINDOW + CHUNK

    @pl.when(jnp.logical_and(b == 0, i == 0))
    def _():
        span = vals_ref.shape[-1]
        for hh in range(hps):
            rows = jnp.broadcast_to(vals_ref[0, hh:hh + 1, :], (CHUNK, span))
            toep = pltpu.roll(rows, span - (CHUNK - 1), 1, stride=1, stride_axis=0)
            bias_ref[hh] = toep[:, :band]

    for p, (k_ref, v_ref) in enumerate(((k0_ref, v0_ref), (k1_ref, v1_ref), (k2_ref, v2_ref))):
        kcat_ref[p * bq:(p + 1) * bq] = k_ref[...]
        vcat_ref[p * bq:(p + 1) * bq] = v_ref[...]

    col = lax.broadcasted_iota(I32, (CHUNK, band), 1)
    starts = [c * CHUNK for c in range(bq // CHUNK)]
    for h0 in range(0, hps, ATTN_HEADS_INTERLEAVED):
        heads = range(h0, min(h0 + ATTN_HEADS_INTERLEAVED, hps))
        work = [(hh, slice(hh * HEAD_DIM, (hh + 1) * HEAD_DIM), r0) for hh in heads for r0 in starts]
        s = [_dot_nt(q_ref[r0:r0 + CHUNK, hs], kcat_ref[r0:r0 + band, hs]) * scale + bias_ref[hh]
             for hh, hs, r0 in work]
        s = [jnp.where(col >= WINDOW - r0 - i * bq, sc, NEG) for (_, _, r0), sc in zip(work, s)]
        m = [jnp.max(sc, axis=-1, keepdims=True) for sc in s]
        e = [jnp.exp(sc - mc) for sc, mc in zip(s, m)]
        inv = [1.0 / jnp.sum(ec, axis=-1, keepdims=True) for ec in e]
        o = [_dot((ec * ic).astype(BF16), vcat_ref[r0:r0 + band, hs])
             for (_, hs, r0), ec, ic in zip(work, e, inv)]
        for n, hh in enumerate(heads):
            o_ref[:, hh * HEAD_DIM:(hh + 1) * HEAD_DIM] = jnp.concatenate(
                o[n * len(starts):(n + 1) * len(starts)], axis=0).astype(o_ref.dtype)


def band_attention_prompt(qkv, batch, seq, rel_bias):
    d = qkv.shape[1] // 3
    n_heads = d // HEAD_DIM
    bq = ATTN_BLOCK_Q
    hps = min(ATTN_HEADS_PER_STEP, n_heads)
    assert seq % bq == 0 and WINDOW == 2 * bq and bq % CHUNK == 0 and n_heads % hps == 0
    nq = seq // bq
    ng = n_heads // hps
    band = WINDOW + CHUNK
    span = -(-(band + CHUNK - 1) // LANES) * LANES
    offs = jnp.arange(span) - (CHUNK - 1)
    vals = _rel_bias(rel_bias, WINDOW - offs).reshape(ng, hps, span)

    def kv(sec, p):
        return pl.BlockSpec((bq, hps * HEAD_DIM),
                            lambda g, b, i: (b * nq + jnp.maximum(i - 2 + p, 0), sec * ng + g))

    return pl.pallas_call(
        functools.partial(_attn_prompt_kernel, bq, hps, HEAD_DIM ** -0.5),
        grid=(ng, batch, nq),
        in_specs=[pl.BlockSpec((bq, hps * HEAD_DIM), lambda g, b, i: (b * nq + i, g)),
                  kv(1, 0), kv(1, 1), kv(1, 2), kv(2, 0), kv(2, 1), kv(2, 2),
                  pl.BlockSpec((1, hps, span), lambda g, b, i: (g, 0, 0))],
        out_specs=pl.BlockSpec((bq, hps * HEAD_DIM), lambda g, b, i: (b * nq + i, g)),
        out_shape=jax.ShapeDtypeStruct((qkv.shape[0], d), BF16),
        scratch_shapes=[pltpu.VMEM((hps, CHUNK, band), F32),
                        pltpu.VMEM((3 * bq, hps * HEAD_DIM), BF16),
                        pltpu.VMEM((3 * bq, hps * HEAD_DIM), BF16)],
        compiler_params=_params(),
        name="band_attention_prompt",
    )(qkv, qkv, qkv, qkv, qkv, qkv, qkv, vals)


def _attn_sample_kernel(n_heads, lc, scale, qkv_ref, ck_ref, cv_ref, bias_ref, base_ref, o_ref):
    del base_ref
    d = n_heads * HEAD_DIM

    def cols(sec, h):
        return slice(sec * d + h * HEAD_DIM, sec * d + (h + 1) * HEAD_DIM)

    for h0 in range(0, n_heads, ATTN_HEADS_INTERLEAVED):
        heads = range(h0, min(h0 + ATTN_HEADS_INTERLEAVED, n_heads))
        sc = [_dot_nt(qkv_ref[:, cols(0, h)], ck_ref[0, 0, :, h, :].astype(BF16)) * scale
              + bias_ref[h, :, 0:lc] for h in heads]
        sn = [_dot_nt(qkv_ref[:, cols(0, h)], qkv_ref[:, cols(1, h)]) * scale + bias_ref[h, :, lc:]
              for h in heads]
        m = [jnp.maximum(jnp.max(a, axis=-1, keepdims=True), jnp.max(b, axis=-1, keepdims=True))
             for a, b in zip(sc, sn)]
        ec = [jnp.exp(a - mh) for a, mh in zip(sc, m)]
        en = [jnp.exp(b - mh) for b, mh in zip(sn, m)]
        inv = [1.0 / (jnp.sum(a, axis=-1, keepdims=True) + jnp.sum(b, axis=-1, keepdims=True))
               for a, b in zip(ec, en)]
        for h, a, b, ih in zip(heads, ec, en, inv):
            o = (_dot((a * ih).astype(BF16), cv_ref[0, 0, :, h, :].astype(BF16))
                 + _dot((b * ih).astype(BF16), qkv_ref[:, cols(2, h)]))
            o_ref[:, cols(0, h)] = o.astype(o_ref.dtype)


def band_attention_sample(qkv, row0, batch, seq, cache_k, cache_v, layer, rel_bias, base):
    d = qkv.shape[1] // 3
    n_heads = d // HEAD_DIM
    lc = cache_k.shape[2]
    assert row0 % seq == 0 and lc % 128 == 0
    blk0 = row0 // seq
    dist = jnp.arange(seq)[:, None] + lc - jnp.arange(lc + seq)[None, :]
    bias = _rel_bias(rel_bias, dist)
    cache_spec = pl.BlockSpec((1, 1, lc, n_heads, HEAD_DIM), lambda b: (layer, b, 0, 0, 0))
    return pl.pallas_call(
        functools.partial(_attn_sample_kernel, n_heads, lc, HEAD_DIM ** -0.5),
        grid=(batch,),
        in_specs=[pl.BlockSpec((seq, 3 * d), lambda b: (blk0 + b, 0)),
                  cache_spec, cache_spec,
                  pl.BlockSpec(bias.shape, lambda b: (0, 0, 0)),
                  pl.BlockSpec(memory_space=pl.ANY)],
        out_specs=pl.BlockSpec((seq, d), lambda b: (blk0 + b, 0)),
        out_shape=jax.ShapeDtypeStruct(base.shape, base.dtype),
        input_output_aliases={4: 0},
        compiler_params=_params(),
        name="band_attention_sample",
    )(qkv, cache_k, cache_v, bias, base)


def _first_max(vals, idx, sentinel):
    m = jnp.max(vals, axis=0, keepdims=True)
    first = jnp.min(jnp.where(vals == m, idx, sentinel), axis=0, keepdims=True)
    return m, first


def _stack_rows(rows, lanes):
    ridx = lax.broadcasted_iota(I32, (len(rows), lanes), 0)
    out = jnp.zeros((len(rows), lanes), rows[0].dtype)
    for r, row in enumerate(rows):
        out = jnp.where(ridx == r, row, out)
    return out


def _router_kernel(n_experts, bt, x_ref, wt_ref, bias_ref, e_ref, gate_ref, rank_ref, cnt_ref,
                   tile_cnt_ref, carry_ref):
    i = pl.program_id(0)

    @pl.when(i == 0)
    def _():
        carry_ref[...] = jnp.zeros_like(carry_ref)

    per = n_experts // N_GROUPS
    logits = _dot_nt(wt_ref[...], x_ref[...])
    scores = _sigmoid(logits)
    sel = scores + bias_ref[:, 0:1]

    sub = lax.broadcasted_iota(I32, (per, bt), 0)
    grp_rows = []
    for g in range(N_GROUPS):
        blk = sel[g * per:(g + 1) * per]
        m1, i1 = _first_max(blk, sub, per)
        m2 = jnp.max(jnp.where(sub == i1, -jnp.inf, blk), axis=0, keepdims=True)
        grp_rows.append(m1 + m2)
    grp = _stack_rows(grp_rows, bt)
    gidx = lax.broadcasted_iota(I32, (N_GROUPS, bt), 0)
    keep = jnp.zeros((N_GROUPS, bt), F32)
    for _ in range(TOPK_GROUPS):
        _, first = _first_max(grp, gidx, N_GROUPS)
        hit = gidx == first
        keep = jnp.where(hit, 1.0, keep)
        grp = jnp.where(hit, -jnp.inf, grp)
    masked = jnp.concatenate(
        [jnp.where(keep[g:g + 1] > 0.0, sel[g * per:(g + 1) * per], -jnp.inf)
         for g in range(N_GROUPS)], axis=0)

    eidx = lax.broadcasted_iota(I32, (n_experts, bt), 0)
    chosen = jnp.zeros((n_experts, bt), F32)
    picks = []
    gates = []
    for _ in range(TOP_K):
        _, first = _first_max(masked, eidx, n_experts)
        hit = eidx == first
        picks.append(first)
        gates.append(jnp.sum(jnp.where(hit, scores, 0.0), axis=0, keepdims=True))
        chosen = jnp.where(hit, 1.0, chosen)
        masked = jnp.where(hit, -jnp.inf, masked)
    gate = _stack_rows(gates, bt)
    gate = gate / jnp.sum(gate, axis=0, keepdims=True) * ROUTE_SCALE

    earlier = (lax.broadcasted_iota(I32, (bt, bt), 0) < lax.broadcasted_iota(I32, (bt, bt), 1))
    before = _dot(chosen.astype(BF16), jnp.where(earlier, 1.0, 0.0).astype(BF16)) + carry_ref[:, 0:1]
    ranks = [jnp.sum(jnp.where(eidx == p, before, 0.0), axis=0, keepdims=True) for p in picks]

    e_ref[...] = _stack_rows(picks, bt)
    gate_ref[...] = gate
    rank_ref[...] = _stack_rows(ranks, bt).astype(I32)
    in_tile = jnp.sum(chosen, axis=1, keepdims=True)
    tile_cnt_ref[0] = jnp.broadcast_to(in_tile, tile_cnt_ref.shape[1:])
    carry_ref[...] = carry_ref[...] + in_tile
    cnt_ref[...] = carry_ref[...]


def moe_router(x, w_router, router_bias, bt=ROUTER_BLOCK_T):
    n, d = x.shape
    n_experts = w_router.shape[1]
    bt = min(bt, n)
    assert n % bt == 0 and n_experts % N_GROUPS == 0
    lanes = 128
    tok = pl.BlockSpec((TOP_K, bt), lambda i: (0, i))
    e, gate, rank, cnt, tile_cnt = pl.pallas_call(
        functools.partial(_router_kernel, n_experts, bt),
        grid=(n // bt,),
        in_specs=[pl.BlockSpec((bt, d), lambda i: (i, 0)),
                  pl.BlockSpec((n_experts, d), lambda i: (0, 0)),
                  pl.BlockSpec((n_experts, lanes), lambda i: (0, 0))],
        out_specs=[tok, tok, tok, pl.BlockSpec((n_experts, lanes), lambda i: (0, 0)),
                   pl.BlockSpec((1, n_experts, lanes), lambda i: (i, 0, 0))],
        out_shape=[jax.ShapeDtypeStruct((TOP_K, n), I32), jax.ShapeDtypeStruct((TOP_K, n), F32),
                   jax.ShapeDtypeStruct((TOP_K, n), I32),
                   jax.ShapeDtypeStruct((n_experts, lanes), F32),
                   jax.ShapeDtypeStruct((n // bt, n_experts, lanes), F32)],
        scratch_shapes=[pltpu.VMEM((n_experts, lanes), F32)],
        compiler_params=_params(),
        name="moe_router",
    )(x, w_router.T.astype(BF16),
      jnp.broadcast_to(router_bias.astype(F32)[:, None], (n_experts, lanes)))
    return e, gate, rank, cnt[:, 0].astype(I32), tile_cnt[:, :, 0].astype(I32)


def _dispatch_kernel(bt, n_experts, sub, fill_start_ref, fill_len_ref, cnt_ref, local_ref, row_ref,
                     lp_ref, x_ref, xs_ref, sort_ref, zero_ref, sems, fill_sem):
    i = pl.program_id(0)
    last = pl.num_programs(0) - 1
    slot = i % 2
    piece_sizes = [1 << b for b in range(bt.bit_length() - 1, -1, -1)]

    def wait_slot(s):
        pltpu.make_async_copy(sort_ref.at[s], sort_ref.at[s], sems.at[s]).wait()

    @pl.when(i == 0)
    def _():
        zero_ref[...] = jnp.zeros_like(zero_ref)

        def fill(e, carry):
            base = fill_start_ref[e]

            def zero_copy(r):
                return pltpu.make_async_copy(
                    zero_ref, xs_ref.at[pl.ds(pl.multiple_of((base + r) * sub, sub), sub)], fill_sem)

            def start(r, c):
                zero_copy(r).start()
                return c

            def wait(r, c):
                zero_copy(r).wait()
                return c

            lax.fori_loop(0, fill_len_ref[e], start, 0)
            lax.fori_loop(0, fill_len_ref[e], wait, 0)
            return carry

        lax.fori_loop(0, n_experts, fill, 0)

    @pl.when(i >= 2)
    def _():
        wait_slot(slot)

    for t in range(bt):
        tile = x_ref[pl.ds(t * sub, sub), :]
        for k in range(TOP_K):
            at = pl.multiple_of(lp_ref[t * TOP_K + k] * sub, sub)
            sort_ref[slot, pl.ds(at, sub), :] = tile

    def send(e, carry):
        n_rows = cnt_ref[i * n_experts + e]
        src0 = local_ref[i * n_experts + e]
        dst0 = row_ref[i * n_experts + e]
        done = 0
        for p in piece_sizes:
            take = (n_rows & p) != 0

            @pl.when(take)
            def _(done=done, p=p):
                pltpu.make_async_copy(
                    sort_ref.at[slot, pl.ds(pl.multiple_of((src0 + done) * sub, sub), p * sub)],
                    xs_ref.at[pl.ds(pl.multiple_of((dst0 + done) * sub, sub), p * sub)],
                    sems.at[slot]).start()

            done = done + jnp.where(take, p, 0)
        return carry

    lax.fori_loop(0, n_experts, send, 0)

    @pl.when(i == last)
    def _():
        @pl.when(i >= 1)
        def _():
            wait_slot(1 - slot)
        wait_slot(slot)


def moe_dispatch(xpk, lp_flat, tile_cnt, tile_local, tile_row, fill_start, fill_len, rows,
                 bt=ROUTER_BLOCK_T):
    n = lp_flat.shape[0] // TOP_K
    sub = xpk.shape[0] // n
    bt = min(bt, n)
    assert n % bt == 0 and bt & (bt - 1) == 0
    n_experts = fill_start.shape[0]
    return pl.pallas_call(
        functools.partial(_dispatch_kernel, bt, n_experts, sub),
        grid_spec=pltpu.PrefetchScalarGridSpec(
            num_scalar_prefetch=5,
            grid=(n // bt,),
            in_specs=[pl.BlockSpec((bt * TOP_K,), lambda i, *_: (i,), memory_space=pltpu.SMEM),
                      pl.BlockSpec((bt * sub, LANES), lambda i, *_: (i, 0))],
            out_specs=pl.BlockSpec(memory_space=pl.ANY),
            scratch_shapes=[pltpu.VMEM((2, bt * TOP_K * sub, LANES), U32),
                            pltpu.VMEM((sub, LANES), U32),
                            pltpu.SemaphoreType.DMA((2,)), pltpu.SemaphoreType.DMA]),
        out_shape=jax.ShapeDtypeStruct((rows * sub, LANES), U32),
        compiler_params=_params(),
        name="moe_dispatch",
    )(fill_start, fill_len, tile_cnt.reshape(-1), tile_local.reshape(-1), tile_row.reshape(-1),
      lp_flat, xpk)


def _expert_kernel(blk, tile_e_ref, n_used_ref, xs_ref, wgu_ref, wdn_ref, ys_ref, wgu_b, wdn_b):
    i = pl.program_id(0)
    de = wdn_b.shape[0]
    changed = jnp.logical_or(i == 0, tile_e_ref[i] != tile_e_ref[jnp.maximum(i - 1, 0)])

    @pl.when(changed)
    def _():
        wgu_b[...] = wgu_ref[0, 0].astype(BF16)
        wdn_b[...] = wdn_ref[0, 0].astype(BF16)

    @pl.when(i < n_used_ref[0])
    def _():
        lo, hi = _unpack_pairs(_load_row_tiles(xs_ref, blk))
        x = jnp.concatenate([lo.astype(BF16), hi.astype(BF16)], axis=1)
        gu = _dot(x, wgu_b[...])
        a = _silu(gu[:, :de]) * gu[:, de:]
        _store_row_tiles(ys_ref, _pack_pairs(_dot(a.astype(BF16), wdn_b[...])))

    @pl.when(i >= n_used_ref[0])
    def _():
        ys_ref[...] = jnp.zeros_like(ys_ref)


def moe_experts(xs, tile_e, n_used, w_gu, w_dn, layer, blk=EXPERT_BLOCK):
    sub = w_gu.shape[2] // 2 // LANES
    d = 2 * sub * LANES
    rows = xs.shape[0] // sub
    n_tiles = rows // blk
    de = w_dn.shape[2]
    return pl.pallas_call(
        functools.partial(_expert_kernel, blk),
        grid_spec=pltpu.PrefetchScalarGridSpec(
            num_scalar_prefetch=2,
            grid=(n_tiles,),
            in_specs=[pl.BlockSpec((blk * sub, LANES),
                                   lambda i, te, nu: (jnp.minimum(i, nu[0] - 1), 0)),
                      pl.BlockSpec((1, 1, d, 2 * de), lambda i, te, nu: (layer, te[i], 0, 0)),
                      pl.BlockSpec((1, 1, de, d), lambda i, te, nu: (layer, te[i], 0, 0))],
            out_specs=pl.BlockSpec((blk * sub, LANES), lambda i, te, nu: (i, 0)),
            scratch_shapes=[pltpu.VMEM((d, 2 * de), BF16), pltpu.VMEM((de, d), BF16)]),
        out_shape=jax.ShapeDtypeStruct((rows * sub, LANES), U32),
        compiler_params=_params(),
        name="moe_experts",
    )(tile_e, n_used, xs, w_gu, w_dn)


def _combine_kernel(alpha, bt, dest_ref, dest_next_ref, ys_ref, gate_ref, x_ref, xb_ref, wsgu_ref,
                    wsdn_ref, g_ref, b_ref, xo_ref, xbo_ref, buf_a, buf_b, sem_a, sem_b):
    i = pl.program_id(0)
    de = wsdn_ref.shape[0]
    sub = ys_ref.shape[1]
    per_tile = bt * TOP_K

    def row_copy(idx_ref, base, buf, sem, t, k):
        return pltpu.make_async_copy(ys_ref.at[idx_ref[base + t * TOP_K + k]],
                                     buf.at[k, pl.ds(t * sub, sub)], sem)

    def issue_unrolled(idx_ref, base, buf, sem):
        for t in range(bt):
            for k in range(TOP_K):
                row_copy(idx_ref, base, buf, sem, t, k).start(priority=k % 2)

    def drain(buf, other, sem):
        pltpu.make_async_copy(other, buf, sem).wait()

    def compute(r0, buf):
        rows = pl.ds(r0, bt)
        sgu = _dot(xb_ref[rows, :], wsgu_ref[...])
        shared = _dot((_silu(sgu[:, :de]) * sgu[:, de:]).astype(BF16), wsdn_ref[...])
        routed_lo, routed_hi = None, None
        for k in range(TOP_K):
            gate = gate_ref[rows, k:k + 1]
            lo, hi = _unpack_pairs(_load_row_tiles(buf.at[k], bt))
            routed_lo = gate * lo if k == 0 else routed_lo + gate * lo
            routed_hi = gate * hi if k == 0 else routed_hi + gate * hi
        routed = jnp.concatenate([routed_lo, routed_hi], axis=1)
        out = _layer_norm(alpha * x_ref[rows, :] + (routed + shared), g_ref[...], b_ref[...])
        xo_ref[rows, :] = out
        xbo_ref[rows, :] = out.astype(BF16)

    @pl.when(i == 0)
    def _():
        def body(t, carry):
            for k in range(TOP_K):
                row_copy(dest_ref, 0, buf_a, sem_a, t, k).start()
            return carry
        lax.fori_loop(0, bt, body, 0)

    drain(buf_a, buf_b, sem_a)
    issue_unrolled(dest_ref, per_tile, buf_b, sem_b)
    compute(0, buf_a)
    drain(buf_b, buf_a, sem_b)
    issue_unrolled(dest_next_ref, 0, buf_a, sem_a)
    compute(bt, buf_b)

    @pl.when(i == pl.num_programs(0) - 1)
    def _():
        drain(buf_a, buf_b, sem_a)


def moe_combine_ln(ys, dest_flat, gate_tm, x, xb, ws_gu, ws_dn, g, b, alpha, bt=COMBINE_BLOCK_T):
    n, d = x.shape
    bt = min(bt, n // 2)
    assert n % (2 * bt) == 0
    n_tiles = n // bt
    de = ws_dn.shape[0]
    sub = ys.shape[1]
    row = pl.BlockSpec((2 * bt, d), lambda i: (i, 0))
    vec = pl.BlockSpec((1, d), lambda i: (0, 0))
    return pl.pallas_call(
        functools.partial(_combine_kernel, alpha, bt),
        grid=(n_tiles // 2,),
        in_specs=[pl.BlockSpec((2 * bt * TOP_K,), lambda i: (i,), memory_space=pltpu.SMEM),
                  pl.BlockSpec((bt * TOP_K,), lambda i: (jnp.minimum(2 * i + 2, n_tiles - 1),),
                               memory_space=pltpu.SMEM),
                  pl.BlockSpec(memory_space=pl.ANY),
                  pl.BlockSpec((2 * bt, TOP_K), lambda i: (i, 0)),
                  row, row,
                  pl.BlockSpec((d, 2 * de), lambda i: (0, 0)),
                  pl.BlockSpec((de, d), lambda i: (0, 0)),
                  vec, vec],
        out_specs=[row, row],
        out_shape=[jax.ShapeDtypeStruct((n, d), F32), jax.ShapeDtypeStruct((n, d), BF16)],
        scratch_shapes=[pltpu.VMEM((TOP_K, bt * sub, LANES), U32),
                        pltpu.VMEM((TOP_K, bt * sub, LANES), U32),
                        pltpu.SemaphoreType.DMA, pltpu.SemaphoreType.DMA],
        compiler_params=_params(),
        name="moe_combine_ln",
    )(dest_flat, dest_flat, ys, gate_tm, x, xb, ws_gu, ws_dn, g.reshape(1, d), b.reshape(1, d))


def moe_ffn_ln(x, xb, xpk, w_router, router_bias, w_gu, w_dn, layer, ws_gu, ws_dn, g, b, alpha):
    n, d = x.shape
    n_experts = w_router.shape[1]
    blk = EXPERT_BLOCK
    top_e, gate, rank, counts, tile_cnt = moe_router(xb, w_router, router_bias)
    padded = (counts + blk - 1) // blk * blk
    pend = jnp.cumsum(padded)
    pstart = pend - padded
    hot = top_e[:, :, None] == jnp.arange(n_experts, dtype=I32)
    dest = jnp.sum(jnp.where(hot, pstart, 0), axis=-1) + rank
    dest_flat = dest.T.reshape(-1)
    earlier_tiles = jnp.cumsum(tile_cnt, axis=0) - tile_cnt
    tile_local = jnp.cumsum(tile_cnt, axis=1) - tile_cnt
    tile_row = pstart[None, :] + earlier_tiles
    per_token = jnp.repeat(tile_local - earlier_tiles, n // tile_cnt.shape[0], axis=0)
    lp = jnp.sum(jnp.where(hot, per_token[None], 0), axis=-1) + rank
    lp_flat = lp.T.reshape(-1)
    n_tiles = (n * TOP_K + n_experts * (blk - 1) + blk - 1) // blk
    tile_row0 = jnp.arange(n_tiles, dtype=I32) * blk
    tile_e = jnp.minimum(jnp.sum((pend[None, :] <= tile_row0[:, None]).astype(I32), axis=1),
                         n_experts - 1)
    n_used = (pend[-1:] // blk).astype(I32)
    sub = d // 2 // LANES
    rows = n_tiles * blk
    xs = moe_dispatch(xpk, lp_flat, tile_cnt, tile_local, tile_row, (pstart + counts).astype(I32),
                      (padded - counts).astype(I32), rows)
    ys = moe_experts(xs, tile_e, n_used, w_gu, w_dn, layer, blk)
    ys = ys.reshape(rows, sub, LANES)
    return moe_combine_ln(ys, dest_flat, gate.T, x, xb, ws_gu, ws_dn, g, b, alpha)


def kernel(x_prompt, x_sample, state_hgrn, state_pool, cache_k, cache_v, ln_g, ln_b, w_in_even,
           lb_even, gnorm_even, pool_w_even, pool_scale_even, w_out_even, w_qkv_odd, rel_bias_odd,
           w_o_odd, w_router, router_bias, w_gu, w_dn, ws_gu, ws_dn):
    batch, seq, d = x_prompt.shape
    dec_batch, dec_seq, _ = x_sample.shape
    depth = ln_g.shape[0]
    n_p = batch * seq
    n_s = dec_batch * dec_seq
    width = w_in_even.shape[2] // 5
    n_heads_a = width // HEAD_DIM
    n_heads_c = d // HEAD_DIM
    alpha = (2 * depth) ** 0.25
    assert dec_seq >= POOL_PREV and seq >= max(POOL_PREV, WINDOW)

    x = (x_prompt.reshape(n_p, d), x_sample.reshape(n_s, d))
    xb = jnp.concatenate([part.astype(BF16) for part in x], axis=0)

    p = jax.nn.softmax(lb_even.astype(F32), axis=0)
    lbs = jnp.cumsum(p, axis=0) - p[0]

    hgrn_p, pool_p, k_p, v_p, hgrn_s, pool_s, k_s, v_s = [], [], [], [], [], [], [], []
    for l in range(depth):
        j = l // 2
        if l % 2 == 0:
            proj = matmul(xb, w_in_even[j].astype(BF16))
            zero_state = jnp.zeros((batch, n_heads_a, HEAD_DIM, HEAD_DIM), F32)
            mixed, sp = hgrn_mixer(proj, 0, batch, seq, zero_state, lbs[j], gnorm_even[j])
            mixed, ss = hgrn_mixer(proj, n_p, dec_batch, dec_seq,
                                   jnp.swapaxes(state_hgrn[j].astype(F32), -1, -2), lbs[j],
                                   gnorm_even[j], mixed)
            pw = pool_w_even[j].astype(BF16)
            mixed = pool_mixer(proj, 0, batch, seq, 0, jnp.zeros((batch, POOL_CARRY, width), F32),
                               pw, pool_scale_even[j], mixed)
            prev_s = jnp.pad(state_pool[j].astype(F32),
                             ((0, 0), (POOL_CARRY - POOL_PREV, 0), (0, 0)))
            mixed = pool_mixer(proj, n_p, dec_batch, dec_seq, PAST_LEN, prev_s, pw,
                               pool_scale_even[j], mixed)
            w_mix = w_out_even[j]
            hgrn_p.append(jnp.swapaxes(sp, -1, -2))
            hgrn_s.append(jnp.swapaxes(ss, -1, -2))
            pool_p.append(jnp.stack([proj[(b + 1) * seq - POOL_PREV:(b + 1) * seq, 4 * width:]
                                     for b in range(batch)]))
            pool_s.append(proj[n_p:].reshape(dec_batch, dec_seq, 5 * width)
                          [:, dec_seq - POOL_PREV:, 4 * width:])
        else:
            qkv = matmul(xb, w_qkv_odd[j].astype(BF16), out_dtype=BF16)
            mixed = band_attention_prompt(qkv, batch, seq, rel_bias_odd[j])
            mixed = band_attention_sample(qkv, n_p, dec_batch, dec_seq, cache_k, cache_v, j,
                                          rel_bias_odd[j], mixed)
            w_mix = w_o_odd[j]
            keep = min(WINDOW, seq)
            tail_p = jnp.stack([qkv[(b + 1) * seq - keep:(b + 1) * seq]
                                for b in range(batch)]).astype(F32)
            new_s = qkv[n_p:].reshape(dec_batch, dec_seq, 3 * d).astype(F32)
            k_p.append(tail_p[:, :, d:2 * d].reshape(batch, keep, n_heads_c, HEAD_DIM))
            v_p.append(tail_p[:, :, 2 * d:].reshape(batch, keep, n_heads_c, HEAD_DIM))
            k_s.append(new_s[:, :, d:2 * d].reshape(dec_batch, dec_seq, n_heads_c, HEAD_DIM))
            v_s.append(new_s[:, :, 2 * d:].reshape(dec_batch, dec_seq, n_heads_c, HEAD_DIM))
        x, xb, xpk = matmul_residual_ln(mixed, w_mix.astype(BF16), x, ln_g[l, 0], ln_b[l, 0], alpha)
        x, xb = moe_ffn_ln(x, xb, xpk, w_router[l], router_bias[l], w_gu, w_dn, l,
                           ws_gu[l].astype(BF16), ws_dn[l].astype(BF16), ln_g[l, 1], ln_b[l, 1],
                           alpha)

    y_p = x[:n_p].reshape(batch, seq, d)
    y_s = x[n_p:].reshape(dec_batch, dec_seq, d)
    return (y_p, y_s, jnp.stack(hgrn_p), jnp.stack(pool_p), jnp.stack(k_p), jnp.stack(v_p),
            jnp.stack(hgrn_s), jnp.stack(pool_s), jnp.stack(k_s), jnp.stack(v_s))
```

```python
import functools

import numpy as np
import jax
import jax.numpy as jnp
from jax import lax
from jax.experimental import pallas as pl
from jax.experimental.pallas import tpu as pltpu

F32 = jnp.float32
BF16 = jnp.bfloat16
I32 = jnp.int32
U32 = jnp.uint32

CHUNK = 64
HEAD_DIM = 128
LANES = 128
POOL_WINDOWS = (2, 4, 8, 16)
POOL_PREV = max(POOL_WINDOWS) - 1
POOL_CARRY = POOL_PREV + 1
LEFT_CHUNKS = 8
WINDOW = LEFT_CHUNKS * CHUNK
MAX_REL = 128
N_GROUPS = 8
TOPK_GROUPS = 4
TOP_K = 8
ROUTE_SCALE = 2.5
LN_EPS = 1e-5
RMS_EPS = 1e-6
PAST_LEN = 1024
NEG = -1e30

VMEM_LIMIT_BYTES = 56 * 1024 * 1024
MM_BLOCK_M = 1024
MM_BLOCK_N = 1024
LN_BLOCK_M = 512
HGRN_BLOCK_T = 128
HGRN_HEADS_INTERLEAVED = 8
ATTN_BLOCK_Q = 256
ATTN_HEADS_PER_STEP = 4
ATTN_HEADS_INTERLEAVED = 4
POOL_BLOCK_T = 512
ROUTER_BLOCK_T = 512
EXPERT_BLOCK = 512
COMBINE_BLOCK_T = 128


def _params(**kw):
    return pltpu.CompilerParams(vmem_limit_bytes=VMEM_LIMIT_BYTES, **kw)


def _sigmoid(x):
    return 1.0 / (1.0 + jnp.exp(-x))


def _silu(x):
    return x * _sigmoid(x)


def _dot(a, b):
    return jnp.dot(a, b, preferred_element_type=F32)


def _dot_nt(a, b):
    return lax.dot_general(a, b, (((1,), (1,)), ((), ())), preferred_element_type=F32)


def _dot_tn(a, b):
    return lax.dot_general(a, b, (((0,), (0,)), ((), ())), preferred_element_type=F32)


def _layer_norm(y, g, b):
    mu = jnp.mean(y, axis=-1, keepdims=True)
    yc = y - mu
    var = jnp.mean(yc * yc, axis=-1, keepdims=True)
    return yc * lax.rsqrt(var + LN_EPS) * g + b


def _mm_kernel(x_ref, w_ref, o_ref):
    o_ref[...] = _dot(x_ref[...], w_ref[...]).astype(o_ref.dtype)


def matmul(x, w, out_dtype=F32, bm=MM_BLOCK_M, bn=MM_BLOCK_N):
    m, k = x.shape
    n = w.shape[1]
    bm, bn = min(bm, m), min(bn, n)
    assert m % bm == 0 and n % bn == 0
    return pl.pallas_call(
        _mm_kernel,
        grid=(m // bm, n // bn),
        in_specs=[pl.BlockSpec((bm, k), lambda i, j: (i, 0)),
                  pl.BlockSpec((k, bn), lambda i, j: (0, j))],
        out_specs=pl.BlockSpec((bm, bn), lambda i, j: (i, j)),
        out_shape=jax.ShapeDtypeStruct((m, n), out_dtype),
        compiler_params=_params(),
        name="matmul",
    )(x, w)


def _pack_pairs(x):
    n = x.shape[1] // 2
    lo = lax.bitcast_convert_type(x[:, :n].astype(jnp.bfloat16).astype(F32), U32)
    hi = lax.bitcast_convert_type(x[:, n:].astype(jnp.bfloat16).astype(F32), U32)
    return (lo >> 16) | hi


def _unpack_pairs(u):
    lo = lax.bitcast_convert_type(u << 16, F32)
    hi = lax.bitcast_convert_type(u & jnp.uint32(0xFFFF0000), F32)
    return lo, hi


def _store_row_tiles(ref, u):
    rows = u.shape[0]
    sub = ref.shape[0] // rows
    for c in range(sub):
        ref[pl.ds(c, rows, stride=sub), :] = u[:, c * LANES:(c + 1) * LANES]


def _load_row_tiles(ref, rows):
    sub = ref.shape[0] // rows
    return jnp.concatenate([ref[pl.ds(c, rows, stride=sub), :] for c in range(sub)], axis=1)


def _mm_ln_kernel(alpha, a_ref, w_ref, res_ref, g_ref, b_ref, x_ref, xb_ref, xpk_ref):
    y = alpha * res_ref[...] + _dot(a_ref[...], w_ref[...])
    out = _layer_norm(y, g_ref[...], b_ref[...])
    x_ref[...] = out
    xb_ref[...] = out.astype(BF16)
    _store_row_tiles(xpk_ref, _pack_pairs(out))


def _mm_ln_split_kernel(alpha, n_first, a_ref, w_ref, res_a_ref, res_b_ref, g_ref, b_ref, x_ref,
                        xb_ref, xpk_ref):
    first = pl.program_id(0) < n_first
    res = jnp.where(first, res_a_ref[...], res_b_ref[...])
    y = alpha * res + _dot(a_ref[...], w_ref[...])
    out = _layer_norm(y, g_ref[...], b_ref[...])
    x_ref[...] = out
    xb_ref[...] = out.astype(BF16)
    _store_row_tiles(xpk_ref, _pack_pairs(out))


def matmul_residual_ln(a, w, res, g, b, alpha, bm=LN_BLOCK_M):
    m, k = a.shape
    d = w.shape[1]
    bm = min(bm, m)
    assert m % bm == 0 and d % (2 * LANES) == 0
    sub = d // 2 // LANES
    if isinstance(res, tuple):
        res_a, res_b = res
        n_first = res_a.shape[0] // bm
        assert res_a.shape[0] % bm == 0 and res_b.shape[0] % bm == 0
        body = functools.partial(_mm_ln_split_kernel, alpha, n_first)
        res_args = [res_a, res_b]
        res_specs = [pl.BlockSpec((bm, d), lambda i: (jnp.minimum(i, n_first - 1), 0)),
                     pl.BlockSpec((bm, d), lambda i: (jnp.maximum(i - n_first, 0), 0))]
    else:
        body = functools.partial(_mm_ln_kernel, alpha)
        res_args = [res]
        res_specs = [pl.BlockSpec((bm, d), lambda i: (i, 0))]
    return pl.pallas_call(
        body,
        grid=(m // bm,),
        in_specs=[pl.BlockSpec((bm, k), lambda i: (i, 0)),
                  pl.BlockSpec((k, d), lambda i: (0, 0)),
                  *res_specs,
                  pl.BlockSpec((1, d), lambda i: (0, 0)),
                  pl.BlockSpec((1, d), lambda i: (0, 0))],
        out_specs=[pl.BlockSpec((bm, d), lambda i: (i, 0)),
                   pl.BlockSpec((bm, d), lambda i: (i, 0)),
                   pl.BlockSpec((bm * sub, LANES), lambda i: (i, 0))],
        out_shape=[jax.ShapeDtypeStruct((m, d), F32), jax.ShapeDtypeStruct((m, d), BF16),
                   jax.ShapeDtypeStruct((m * sub, LANES), U32)],
        compiler_params=_params(),
        name="matmul_residual_ln",
    )(a, w, *res_args, g.reshape(1, d), b.reshape(1, d))


def _hgrn_tables(c):
    t = np.arange(c)
    levels = []
    b = c // 2
    while b >= 1:
        levels.append(b)
        b //= 2
    sel = [(t[None, :] <= t[:, None]).astype(np.float32)]
    lvl = np.where(np.eye(c, dtype=bool), 0, -1).astype(np.int32)
    for l, b in enumerate(levels):
        blk = t // b
        odd = (blk % 2) == 1
        ref_row = np.where(odd, blk * b - 1, blk * b + b - 1)
        sel.append((t[None, :] <= ref_row[:, None]).astype(np.float32))
        same = (t[:, None] // (2 * b)) == (t[None, :] // (2 * b))
        lvl = np.where(same & odd[:, None] & (~odd)[None, :], l + 1, lvl)
    sel = np.concatenate(sel, axis=0)
    return np.concatenate([sel, sel, sel], axis=1), lvl


def _hgrn_kernel(n_heads, c, q_ref, f_ref, i_ref, g_ref, s0_ref, lb_ref, gn_ref, sel_ref, lvl_ref,
                 o_ref, sfin_ref, state_ref, cums_ref, qw_ref, kw_ref):
    j = pl.program_id(1)

    @pl.when(j == 0)
    def _():
        state_ref[...] = s0_ref[0]

    n_lvl = qw_ref.shape[0]
    lb = lb_ref[...]
    f = lb + (1.0 - lb) * _sigmoid(f_ref[...])
    k = 1.0 - f
    q = _silu(q_ref[...])
    lg = jnp.log(f)
    hi = lg.astype(BF16)
    rest = lg - hi.astype(F32)
    mid = rest.astype(BF16)
    lo = (rest - mid.astype(F32)).astype(BF16)
    cums_ref[...] = _dot(sel_ref[...], jnp.concatenate([hi, mid, lo], axis=0))
    cum = cums_ref[0:c]
    qw_ref[0] = q.astype(BF16)
    kw_ref[0] = k.astype(BF16)
    for l in range(1, n_lvl):
        w = jnp.exp(-jnp.abs(cum - cums_ref[l * c:(l + 1) * c]))
        qw_ref[l] = (q * w).astype(BF16)
        kw_ref[l] = (k * w).astype(BF16)
    last = cum[c - 1:c, :]
    q_in = (q * jnp.exp(cum)).astype(BF16)
    k_end = (k * jnp.exp(last - cum)).astype(BF16)
    decay = jnp.exp(last)
    lvl = lvl_ref[...]
    for h0 in range(0, n_heads, HGRN_HEADS_INTERLEAVED):
        heads = list(range(h0, min(h0 + HGRN_HEADS_INTERLEAVED, n_heads)))
        cols = [slice(h * HEAD_DIM, (h + 1) * HEAD_DIM) for h in heads]
        st = [state_ref[h] for h in heads]
        o = [_dot_nt(q_in[:, hs], s.astype(BF16)) for hs, s in zip(cols, st)]
        scores = [jnp.zeros((c, c), F32) for _ in heads]
        for l in range(n_lvl):
            scores = [jnp.where(lvl == l, _dot_nt(qw_ref[l, :, hs], kw_ref[l, :, hs]), sc)
                      for hs, sc in zip(cols, scores)]
        v = [i_ref[:, hs].astype(BF16) for hs in cols]
        o = [oh + _dot(sc.astype(BF16), vh) for oh, sc, vh in zip(o, scores, v)]
        for h, hs, s, vh in zip(heads, cols, st, v):
            state_ref[h] = s * decay[:, hs] + _dot_tn(vh, k_end[:, hs])
        o = [oh * lax.rsqrt(jnp.mean(oh * oh, axis=-1, keepdims=True) + RMS_EPS) * gn_ref[...]
             for oh in o]
        for hs, oh in zip(cols, o):
            o_ref[:, hs] = (oh * _silu(g_ref[:, hs])).astype(o_ref.dtype)

    @pl.when(j == pl.num_programs(1) - 1)
    def _():
        sfin_ref[0] = state_ref[...]


def _fill_in_place(body, n_inputs, base):
    if base is None:
        return body
    return lambda *refs: body(*refs[:n_inputs], *refs[n_inputs + 1:])


def hgrn_mixer(proj, row0, batch, seq, s0_t, lb, gnorm, base=None):
    width = proj.shape[1] // 5
    n_heads = width // HEAD_DIM
    c = HGRN_BLOCK_T if seq % HGRN_BLOCK_T == 0 else seq
    n = seq // c
    assert row0 % c == 0 and c & (c - 1) == 0
    blk0 = row0 // c
    sel, lvl = _hgrn_tables(c)
    n_lvl = sel.shape[0] // c

    def sec(s):
        return pl.BlockSpec((c, width), lambda b, j: (blk0 + b * n + j, s))

    state_spec = pl.BlockSpec((1, n_heads, HEAD_DIM, HEAD_DIM), lambda b, j: (b, 0, 0, 0))
    inputs = [proj, proj, proj, proj, s0_t, lb.reshape(1, width), gnorm.reshape(1, HEAD_DIM),
              jnp.asarray(sel).astype(BF16), jnp.asarray(lvl)]
    in_specs = [sec(0), sec(1), sec(2), sec(3), state_spec,
                pl.BlockSpec((1, width), lambda b, j: (0, 0)),
                pl.BlockSpec((1, HEAD_DIM), lambda b, j: (0, 0)),
                pl.BlockSpec(sel.shape, lambda b, j: (0, 0)),
                pl.BlockSpec(lvl.shape, lambda b, j: (0, 0))]
    return pl.pallas_call(
        _fill_in_place(functools.partial(_hgrn_kernel, n_heads, c), len(inputs), base),
        grid=(batch, n),
        in_specs=in_specs + ([] if base is None else [pl.BlockSpec(memory_space=pl.ANY)]),
        out_specs=[pl.BlockSpec((c, width), lambda b, j: (blk0 + b * n + j, 0)), state_spec],
        out_shape=[jax.ShapeDtypeStruct((proj.shape[0], 2 * width), BF16),
                   jax.ShapeDtypeStruct((batch, n_heads, HEAD_DIM, HEAD_DIM), F32)],
        input_output_aliases={} if base is None else {len(inputs): 0},
        scratch_shapes=[pltpu.VMEM((n_heads, HEAD_DIM, HEAD_DIM), F32),
                        pltpu.VMEM((n_lvl * c, width), F32),
                        pltpu.VMEM((n_lvl, c, width), BF16),
                        pltpu.VMEM((n_lvl, c, width), BF16)],
        compiler_params=_params(),
        name="hgrn_mixer",
    )(*inputs, *([] if base is None else [base]))


def _pool_kernel(pos0, bt, group, u_ref, prev_ref, w_ref, scale_ref, o_ref, ext_ref):
    j = pl.program_id(1)

    @pl.when(j == 0)
    def _():
        ext_ref[0:POOL_CARRY] = prev_ref[0]

    u = u_ref[...]
    ext_ref[POOL_CARRY:POOL_CARRY + bt] = u
    pos = pos0 + j * bt + lax.broadcasted_iota(I32, (bt, 1), 0)
    for gi, win in enumerate(POOL_WINDOWS):
        cs = slice(gi * group, (gi + 1) * group)
        s = ext_ref[:, cs]
        step = 1
        while step < win:
            s = s + pltpu.roll(s, step, axis=0)
            step *= 2
        cnt = jnp.minimum(pos + 1, win).astype(F32)
        d = s[POOL_CARRY:] / cnt - u[:, cs]
        y = _dot(d.astype(BF16), w_ref[gi]) * scale_ref[:, cs]
        o_ref[:, cs] = y.astype(o_ref.dtype)
    ext_ref[0:POOL_CARRY] = ext_ref[bt:bt + POOL_CARRY]


def pool_mixer(proj, row0, batch, seq, pos0, prev, pool_w, pool_scale, base):
    width = proj.shape[1] // 5
    group = width // len(POOL_WINDOWS)
    bt = min(POOL_BLOCK_T, seq)
    n = seq // bt
    assert seq % bt == 0 and row0 % bt == 0 and sum(POOL_WINDOWS) // 2 <= POOL_CARRY
    blk0 = row0 // bt
    inputs = [proj, prev, pool_w, pool_scale.reshape(1, width)]
    return pl.pallas_call(
        _fill_in_place(functools.partial(_pool_kernel, pos0, bt, group), len(inputs), base),
        grid=(batch, n),
        in_specs=[pl.BlockSpec((bt, width), lambda b, j: (blk0 + b * n + j, 4)),
                  pl.BlockSpec((1, POOL_CARRY, width), lambda b, j: (b, 0, 0)),
                  pl.BlockSpec(pool_w.shape, lambda b, j: (0, 0, 0)),
                  pl.BlockSpec((1, width), lambda b, j: (0, 0)),
                  pl.BlockSpec(memory_space=pl.ANY)],
        out_specs=pl.BlockSpec((bt, width), lambda b, j: (blk0 + b * n + j, 1)),
        out_shape=jax.ShapeDtypeStruct(base.shape, base.dtype),
        input_output_aliases={len(inputs): 0},
        scratch_shapes=[pltpu.VMEM((bt + POOL_CARRY, width), F32)],
        compiler_params=_params(),
        name="pool_mixer",
    )(*inputs, base)


def _rel_bias(table, dist):
    idx = jnp.clip(dist, -(CHUNK - 1), MAX_REL) + (CHUNK - 1)
    return table[:, idx].astype(F32)


def _attn_prompt_kernel(bq, hps, scale, q_ref, k0_ref, k1_ref, k2_ref, v0_ref, v1_ref, v2_ref,
                        vals_ref, o_ref, bias_ref, kcat_ref, vcat_ref):
    b = pl.program_id(1)
    i = pl.program_id(2)
    band = WINDOW + CHUNK

    @pl.when(jnp.logical_and(b == 0, i == 0))
    def _():
        span = vals_ref.shape[-1]
        for hh in range(hps):
            rows = jnp.broadcast_to(vals_ref[0, hh:hh + 1, :], (CHUNK, span))
            toep = pltpu.roll(rows, span - (CHUNK - 1), 1, stride=1, stride_axis=0)
            bias_ref[hh] = toep[:, :band]

    for p, (k_ref, v_ref) in enumerate(((k0_ref, v0_ref), (k1_ref, v1_ref), (k2_ref, v2_ref))):
        kcat_ref[p * bq:(p + 1) * bq] = k_ref[...]
        vcat_ref[p * bq:(p + 1) * bq] = v_ref[...]

    col = lax.broadcasted_iota(I32, (CHUNK, band), 1)
    starts = [c * CHUNK for c in range(bq // CHUNK)]
    for h0 in range(0, hps, ATTN_HEADS_INTERLEAVED):
        heads = range(h0, min(h0 + ATTN_HEADS_INTERLEAVED, hps))
        work = [(hh, slice(hh * HEAD_DIM, (hh + 1) * HEAD_DIM), r0)
                for hh in heads for r0 in starts]
        s = [_dot_nt(q_ref[r0:r0 + CHUNK, hs], kcat_ref[r0:r0 + band, hs]) * scale + bias_ref[hh]
             for hh, hs, r0 in work]
        s = [jnp.where(col >= WINDOW - r0 - i * bq, sc, NEG) for (_, _, r0), sc in zip(work, s)]
        m = [jnp.max(sc, axis=-1, keepdims=True) for sc in s]
        e = [jnp.exp(sc - mc) for sc, mc in zip(s, m)]
        inv = [1.0 / jnp.sum(ec, axis=-1, keepdims=True) for ec in e]
        o = [_dot((ec * ic).astype(BF16), vcat_ref[r0:r0 + band, hs])
             for (_, hs, r0), ec, ic in zip(work, e, inv)]
        for n, hh in enumerate(heads):
            o_ref[:, hh * HEAD_DIM:(hh + 1) * HEAD_DIM] = jnp.concatenate(
                o[n * len(starts):(n + 1) * len(starts)], axis=0).astype(o_ref.dtype)


def band_attention_prompt(qkv, batch, seq, rel_bias):
    d = qkv.shape[1] // 3
    n_heads = d // HEAD_DIM
    bq = ATTN_BLOCK_Q
    hps = min(ATTN_HEADS_PER_STEP, n_heads)
    assert seq % bq == 0 and WINDOW == 2 * bq and bq % CHUNK == 0 and n_heads % hps == 0
    nq = seq // bq
    ng = n_heads // hps
    band = WINDOW + CHUNK
    span = -(-(band + CHUNK - 1) // LANES) * LANES
    offs = jnp.arange(span) - (CHUNK - 1)
    vals = _rel_bias(rel_bias, WINDOW - offs).reshape(ng, hps, span)

    def kv(sec, p):
        return pl.BlockSpec((bq, hps * HEAD_DIM),
                            lambda g, b, i: (b * nq + jnp.maximum(i - 2 + p, 0), sec * ng + g))

    return pl.pallas_call(
        functools.partial(_attn_prompt_kernel, bq, hps, HEAD_DIM ** -0.5),
        grid=(ng, batch, nq),
        in_specs=[pl.BlockSpec((bq, hps * HEAD_DIM), lambda g, b, i: (b * nq + i, g)),
                  kv(1, 0), kv(1, 1), kv(1, 2), kv(2, 0), kv(2, 1), kv(2, 2),
                  pl.BlockSpec((1, hps, span), lambda g, b, i: (g, 0, 0))],
        out_specs=pl.BlockSpec((bq, hps * HEAD_DIM), lambda g, b, i: (b * nq + i, g)),
        out_shape=jax.ShapeDtypeStruct((qkv.shape[0], d), BF16),
        scratch_shapes=[pltpu.VMEM((hps, CHUNK, band), F32),
                        pltpu.VMEM((3 * bq, hps * HEAD_DIM), BF16),
                        pltpu.VMEM((3 * bq, hps * HEAD_DIM), BF16)],
        compiler_params=_params(),
        name="band_attention_prompt",
    )(qkv, qkv, qkv, qkv, qkv, qkv, qkv, vals)


def _attn_sample_kernel(n_heads, lc, scale, qkv_ref, ck_ref, cv_ref, bias_ref, base_ref, o_ref):
    del base_ref
    d = n_heads * HEAD_DIM

    def cols(sec, h):
        return slice(sec * d + h * HEAD_DIM, sec * d + (h + 1) * HEAD_DIM)

    for h0 in range(0, n_heads, ATTN_HEADS_INTERLEAVED):
        heads = range(h0, min(h0 + ATTN_HEADS_INTERLEAVED, n_heads))
        sc = [_dot_nt(qkv_ref[:, cols(0, h)], ck_ref[0, 0, :, h, :].astype(BF16)) * scale
              + bias_ref[h, :, 0:lc] for h in heads]
        sn = [_dot_nt(qkv_ref[:, cols(0, h)], qkv_ref[:, cols(1, h)]) * scale + bias_ref[h, :, lc:]
              for h in heads]
        m = [jnp.maximum(jnp.max(a, axis=-1, keepdims=True), jnp.max(b, axis=-1, keepdims=True))
             for a, b in zip(sc, sn)]
        ec = [jnp.exp(a - mh) for a, mh in zip(sc, m)]
        en = [jnp.exp(b - mh) for b, mh in zip(sn, m)]
        inv = [1.0 / (jnp.sum(a, axis=-1, keepdims=True) + jnp.sum(b, axis=-1, keepdims=True))
               for a, b in zip(ec, en)]
        for h, a, b, ih in zip(heads, ec, en, inv):
            o = (_dot((a * ih).astype(BF16), cv_ref[0, 0, :, h, :].astype(BF16))
                 + _dot((b * ih).astype(BF16), qkv_ref[:, cols(2, h)]))
            o_ref[:, cols(0, h)] = o.astype(o_ref.dtype)


def band_attention_sample(qkv, row0, batch, seq, cache_k, cache_v, layer, rel_bias, base):
    d = qkv.shape[1] // 3
    n_heads = d // HEAD_DIM
    lc = cache_k.shape[2]
    assert row0 % seq == 0 and lc % 128 == 0
    blk0 = row0 // seq
    dist = jnp.arange(seq)[:, None] + lc - jnp.arange(lc + seq)[None, :]
    bias = _rel_bias(rel_bias, dist)
    cache_spec = pl.BlockSpec((1, 1, lc, n_heads, HEAD_DIM), lambda b: (layer, b, 0, 0, 0))
    return pl.pallas_call(
        functools.partial(_attn_sample_kernel, n_heads, lc, HEAD_DIM ** -0.5),
        grid=(batch,),
        in_specs=[pl.BlockSpec((seq, 3 * d), lambda b: (blk0 + b, 0)),
                  cache_spec, cache_spec,
                  pl.BlockSpec(bias.shape, lambda b: (0, 0, 0)),
                  pl.BlockSpec(memory_space=pl.ANY)],
        out_specs=pl.BlockSpec((seq, d), lambda b: (blk0 + b, 0)),
        out_shape=jax.ShapeDtypeStruct(base.shape, base.dtype),
        input_output_aliases={4: 0},
        compiler_params=_params(),
        name="band_attention_sample",
    )(qkv, cache_k, cache_v, bias, base)


def _first_max(vals, idx, sentinel):
    m = jnp.max(vals, axis=0, keepdims=True)
    first = jnp.min(jnp.where(vals == m, idx, sentinel), axis=0, keepdims=True)
    return m, first


def _stack_rows(rows, lanes):
    ridx = lax.broadcasted_iota(I32, (len(rows), lanes), 0)
    out = jnp.zeros((len(rows), lanes), rows[0].dtype)
    for r, row in enumerate(rows):
        out = jnp.where(ridx == r, row, out)
    return out


def _router_kernel(n_experts, bt, x_ref, wt_ref, bias_ref, e_ref, gate_ref, rank_ref, cnt_ref,
                   tile_cnt_ref, carry_ref):
    i = pl.program_id(0)

    @pl.when(i == 0)
    def _():
        carry_ref[...] = jnp.zeros_like(carry_ref)

    per = n_experts // N_GROUPS
    logits = _dot_nt(wt_ref[...], x_ref[...])
    scores = _sigmoid(logits)
    sel = scores + bias_ref[:, 0:1]

    sub = lax.broadcasted_iota(I32, (per, bt), 0)
    grp_rows = []
    for g in range(N_GROUPS):
        blk = sel[g * per:(g + 1) * per]
        m1, i1 = _first_max(blk, sub, per)
        m2 = jnp.max(jnp.where(sub == i1, -jnp.inf, blk), axis=0, keepdims=True)
        grp_rows.append(m1 + m2)
    grp = _stack_rows(grp_rows, bt)
    gidx = lax.broadcasted_iota(I32, (N_GROUPS, bt), 0)
    keep = jnp.zeros((N_GROUPS, bt), F32)
    for _ in range(TOPK_GROUPS):
        _, first = _first_max(grp, gidx, N_GROUPS)
        hit = gidx == first
        keep = jnp.where(hit, 1.0, keep)
        grp = jnp.where(hit, -jnp.inf, grp)
    masked = jnp.concatenate(
        [jnp.where(keep[g:g + 1] > 0.0, sel[g * per:(g + 1) * per], -jnp.inf)
         for g in range(N_GROUPS)], axis=0)

    eidx = lax.broadcasted_iota(I32, (n_experts, bt), 0)
    chosen = jnp.zeros((n_experts, bt), F32)
    picks = []
    gates = []
    for _ in range(TOP_K):
        _, first = _first_max(masked, eidx, n_experts)
        hit = eidx == first
        picks.append(first)
        gates.append(jnp.sum(jnp.where(hit, scores, 0.0), axis=0, keepdims=True))
        chosen = jnp.where(hit, 1.0, chosen)
        masked = jnp.where(hit, -jnp.inf, masked)
    gate = _stack_rows(gates, bt)
    gate = gate / jnp.sum(gate, axis=0, keepdims=True) * ROUTE_SCALE

    earlier = (lax.broadcasted_iota(I32, (bt, bt), 0) < lax.broadcasted_iota(I32, (bt, bt), 1))
    before = (_dot(chosen.astype(BF16), jnp.where(earlier, 1.0, 0.0).astype(BF16))
              + carry_ref[:, 0:1])
    ranks = [jnp.sum(jnp.where(eidx == p, before, 0.0), axis=0, keepdims=True) for p in picks]

    e_ref[...] = _stack_rows(picks, bt)
    gate_ref[...] = gate
    rank_ref[...] = _stack_rows(ranks, bt).astype(I32)
    in_tile = jnp.sum(chosen, axis=1, keepdims=True)
    tile_cnt_ref[0] = jnp.broadcast_to(in_tile, tile_cnt_ref.shape[1:])
    carry_ref[...] = carry_ref[...] + in_tile
    cnt_ref[...] = carry_ref[...]


def moe_router(x, w_router, router_bias, bt=ROUTER_BLOCK_T):
    n, d = x.shape
    n_experts = w_router.shape[1]
    bt = min(bt, n)
    assert n % bt == 0 and n_experts % N_GROUPS == 0
    lanes = LANES
    tok = pl.BlockSpec((TOP_K, bt), lambda i: (0, i))
    e, gate, rank, cnt, tile_cnt = pl.pallas_call(
        functools.partial(_router_kernel, n_experts, bt),
        grid=(n // bt,),
        in_specs=[pl.BlockSpec((bt, d), lambda i: (i, 0)),
                  pl.BlockSpec((n_experts, d), lambda i: (0, 0)),
                  pl.BlockSpec((n_experts, lanes), lambda i: (0, 0))],
        out_specs=[tok, tok, tok, pl.BlockSpec((n_experts, lanes), lambda i: (0, 0)),
                   pl.BlockSpec((1, n_experts, lanes), lambda i: (i, 0, 0))],
        out_shape=[jax.ShapeDtypeStruct((TOP_K, n), I32), jax.ShapeDtypeStruct((TOP_K, n), F32),
                   jax.ShapeDtypeStruct((TOP_K, n), I32),
                   jax.ShapeDtypeStruct((n_experts, lanes), F32),
                   jax.ShapeDtypeStruct((n // bt, n_experts, lanes), F32)],
        scratch_shapes=[pltpu.VMEM((n_experts, lanes), F32)],
        compiler_params=_params(),
        name="moe_router",
    )(x, w_router.T.astype(BF16),
      jnp.broadcast_to(router_bias.astype(F32)[:, None], (n_experts, lanes)))
    return e, gate, rank, cnt[:, 0].astype(I32), tile_cnt[:, :, 0].astype(I32)


def _dispatch_kernel(bt, n_experts, sub, fill_start_ref, fill_len_ref, cnt_ref, local_ref, row_ref,
                     lp_ref, x_ref, xs_ref, sort_ref, zero_ref, sems, fill_sem):
    i = pl.program_id(0)
    last = pl.num_programs(0) - 1
    slot = i % 2
    piece_sizes = [1 << b for b in range(bt.bit_length() - 1, -1, -1)]

    def wait_slot(s):
        pltpu.make_async_copy(sort_ref.at[s], sort_ref.at[s], sems.at[s]).wait()

    @pl.when(i == 0)
    def _():
        zero_ref[...] = jnp.zeros_like(zero_ref)

        def fill(e, carry):
            base = fill_start_ref[e]

            def zero_copy(r):
                row0 = pl.multiple_of((base + r) * sub, sub)
                return pltpu.make_async_copy(zero_ref, xs_ref.at[pl.ds(row0, sub)], fill_sem)

            def start(r, c):
                zero_copy(r).start()
                return c

            def wait(r, c):
                zero_copy(r).wait()
                return c

            lax.fori_loop(0, fill_len_ref[e], start, 0)
            lax.fori_loop(0, fill_len_ref[e], wait, 0)
            return carry

        lax.fori_loop(0, n_experts, fill, 0)

    @pl.when(i >= 2)
    def _():
        wait_slot(slot)

    for t in range(bt):
        tile = x_ref[pl.ds(t * sub, sub), :]
        for k in range(TOP_K):
            at = pl.multiple_of(lp_ref[t * TOP_K + k] * sub, sub)
            sort_ref[slot, pl.ds(at, sub), :] = tile

    def send(e, carry):
        n_rows = cnt_ref[i * n_experts + e]
        src0 = local_ref[i * n_experts + e]
        dst0 = row_ref[i * n_experts + e]
        done = 0
        for p in piece_sizes:
            take = (n_rows & p) != 0

            @pl.when(take)
            def _(done=done, p=p):
                pltpu.make_async_copy(
                    sort_ref.at[slot, pl.ds(pl.multiple_of((src0 + done) * sub, sub), p * sub)],
                    xs_ref.at[pl.ds(pl.multiple_of((dst0 + done) * sub, sub), p * sub)],
                    sems.at[slot]).start()

            done = done + jnp.where(take, p, 0)
        return carry

    lax.fori_loop(0, n_experts, send, 0)

    @pl.when(i == last)
    def _():
        @pl.when(i >= 1)
        def _():
            wait_slot(1 - slot)
        wait_slot(slot)


def moe_dispatch(xpk, lp_flat, tile_cnt, tile_local, tile_row, fill_start, fill_len, rows,
                 bt=ROUTER_BLOCK_T):
    n = lp_flat.shape[0] // TOP_K
    sub = xpk.shape[0] // n
    bt = min(bt, n)
    assert n % bt == 0 and bt & (bt - 1) == 0
    n_experts = fill_start.shape[0]
    return pl.pallas_call(
        functools.partial(_dispatch_kernel, bt, n_experts, sub),
        grid_spec=pltpu.PrefetchScalarGridSpec(
            num_scalar_prefetch=5,
            grid=(n // bt,),
            in_specs=[pl.BlockSpec((bt * TOP_K,), lambda i, *_: (i,), memory_space=pltpu.SMEM),
                      pl.BlockSpec((bt * sub, LANES), lambda i, *_: (i, 0))],
            out_specs=pl.BlockSpec(memory_space=pl.ANY),
            scratch_shapes=[pltpu.VMEM((2, bt * TOP_K * sub, LANES), U32),
                            pltpu.VMEM((sub, LANES), U32),
                            pltpu.SemaphoreType.DMA((2,)), pltpu.SemaphoreType.DMA]),
        out_shape=jax.ShapeDtypeStruct((rows * sub, LANES), U32),
        compiler_params=_params(),
        name="moe_dispatch",
    )(fill_start, fill_len, tile_cnt.reshape(-1), tile_local.reshape(-1), tile_row.reshape(-1),
      lp_flat, xpk)


def _expert_kernel(blk, tile_e_ref, n_used_ref, xs_ref, wgu_ref, wdn_ref, ys_ref, wgu_b, wdn_b):
    i = pl.program_id(0)
    de = wdn_b.shape[0]
    changed = jnp.logical_or(i == 0, tile_e_ref[i] != tile_e_ref[jnp.maximum(i - 1, 0)])

    @pl.when(changed)
    def _():
        wgu_b[...] = wgu_ref[0, 0].astype(BF16)
        wdn_b[...] = wdn_ref[0, 0].astype(BF16)

    @pl.when(i < n_used_ref[0])
    def _():
        lo, hi = _unpack_pairs(_load_row_tiles(xs_ref, blk))
        x = jnp.concatenate([lo.astype(BF16), hi.astype(BF16)], axis=1)
        gu = _dot(x, wgu_b[...])
        a = _silu(gu[:, :de]) * gu[:, de:]
        _store_row_tiles(ys_ref, _pack_pairs(_dot(a.astype(BF16), wdn_b[...])))

    @pl.when(i >= n_used_ref[0])
    def _():
        ys_ref[...] = jnp.zeros_like(ys_ref)


def moe_experts(xs, tile_e, n_used, w_gu, w_dn, layer, blk=EXPERT_BLOCK):
    sub = w_gu.shape[2] // 2 // LANES
    d = 2 * sub * LANES
    rows = xs.shape[0] // sub
    n_tiles = rows // blk
    de = w_dn.shape[2]
    return pl.pallas_call(
        functools.partial(_expert_kernel, blk),
        grid_spec=pltpu.PrefetchScalarGridSpec(
            num_scalar_prefetch=2,
            grid=(n_tiles,),
            in_specs=[pl.BlockSpec((blk * sub, LANES),
                                   lambda i, te, nu: (jnp.minimum(i, nu[0] - 1), 0)),
                      pl.BlockSpec((1, 1, d, 2 * de), lambda i, te, nu: (layer, te[i], 0, 0)),
                      pl.BlockSpec((1, 1, de, d), lambda i, te, nu: (layer, te[i], 0, 0))],
            out_specs=pl.BlockSpec((blk * sub, LANES), lambda i, te, nu: (i, 0)),
            scratch_shapes=[pltpu.VMEM((d, 2 * de), BF16), pltpu.VMEM((de, d), BF16)]),
        out_shape=jax.ShapeDtypeStruct((rows * sub, LANES), U32),
        compiler_params=_params(),
        name="moe_experts",
    )(tile_e, n_used, xs, w_gu, w_dn)


def _combine_kernel(alpha, bt, dest_ref, dest_next_ref, ys_ref, gate_ref, x_ref, xb_ref, wsgu_ref,
                    wsdn_ref, g_ref, b_ref, xo_ref, xbo_ref, buf_a, buf_b, sem_a, sem_b):
    i = pl.program_id(0)
    de = wsdn_ref.shape[0]
    sub = ys_ref.shape[1]
    per_tile = bt * TOP_K

    def row_copy(idx_ref, base, buf, sem, t, k):
        return pltpu.make_async_copy(ys_ref.at[idx_ref[base + t * TOP_K + k]],
                                     buf.at[k, pl.ds(t * sub, sub)], sem)

    def issue_unrolled(idx_ref, base, buf, sem):
        for t in range(bt):
            for k in range(TOP_K):
                row_copy(idx_ref, base, buf, sem, t, k).start(priority=k % 2)

    def drain(buf, other, sem):
        pltpu.make_async_copy(other, buf, sem).wait()

    def compute(r0, buf):
        rows = pl.ds(r0, bt)
        sgu = _dot(xb_ref[rows, :], wsgu_ref[...])
        shared = _dot((_silu(sgu[:, :de]) * sgu[:, de:]).astype(BF16), wsdn_ref[...])
        routed_lo, routed_hi = None, None
        for k in range(TOP_K):
            gate = gate_ref[rows, k:k + 1]
            lo, hi = _unpack_pairs(_load_row_tiles(buf.at[k], bt))
            routed_lo = gate * lo if k == 0 else routed_lo + gate * lo
            routed_hi = gate * hi if k == 0 else routed_hi + gate * hi
        routed = jnp.concatenate([routed_lo, routed_hi], axis=1)
        out = _layer_norm(alpha * x_ref[rows, :] + (routed + shared), g_ref[...], b_ref[...])
        xo_ref[rows, :] = out
        xbo_ref[rows, :] = out.astype(BF16)

    @pl.when(i == 0)
    def _():
        def body(t, carry):
            for k in range(TOP_K):
                row_copy(dest_ref, 0, buf_a, sem_a, t, k).start()
            return carry
        lax.fori_loop(0, bt, body, 0)

    drain(buf_a, buf_b, sem_a)
    issue_unrolled(dest_ref, per_tile, buf_b, sem_b)
    compute(0, buf_a)
    drain(buf_b, buf_a, sem_b)
    issue_unrolled(dest_next_ref, 0, buf_a, sem_a)
    compute(bt, buf_b)

    @pl.when(i == pl.num_programs(0) - 1)
    def _():
        drain(buf_a, buf_b, sem_a)


def moe_combine_ln(ys, dest_flat, gate_tm, x, xb, ws_gu, ws_dn, g, b, alpha, bt=COMBINE_BLOCK_T):
    n, d = x.shape
    bt = min(bt, n // 2)
    assert n % (2 * bt) == 0
    n_tiles = n // bt
    de = ws_dn.shape[0]
    sub = ys.shape[1]
    row = pl.BlockSpec((2 * bt, d), lambda i: (i, 0))
    vec = pl.BlockSpec((1, d), lambda i: (0, 0))
    return pl.pallas_call(
        functools.partial(_combine_kernel, alpha, bt),
        grid=(n_tiles // 2,),
        in_specs=[pl.BlockSpec((2 * bt * TOP_K,), lambda i: (i,), memory_space=pltpu.SMEM),
                  pl.BlockSpec((bt * TOP_K,), lambda i: (jnp.minimum(2 * i + 2, n_tiles - 1),),
                               memory_space=pltpu.SMEM),
                  pl.BlockSpec(memory_space=pl.ANY),
                  pl.BlockSpec((2 * bt, TOP_K), lambda i: (i, 0)),
                  row, row,
                  pl.BlockSpec((d, 2 * de), lambda i: (0, 0)),
                  pl.BlockSpec((de, d), lambda i: (0, 0)),
                  vec, vec],
        out_specs=[row, row],
        out_shape=[jax.ShapeDtypeStruct((n, d), F32), jax.ShapeDtypeStruct((n, d), BF16)],
        scratch_shapes=[pltpu.VMEM((TOP_K, bt * sub, LANES), U32),
                        pltpu.VMEM((TOP_K, bt * sub, LANES), U32),
                        pltpu.SemaphoreType.DMA, pltpu.SemaphoreType.DMA],
        compiler_params=_params(),
        name="moe_combine_ln",
    )(dest_flat, dest_flat, ys, gate_tm, x, xb, ws_gu, ws_dn, g.reshape(1, d), b.reshape(1, d))


def moe_ffn_ln(x, xb, xpk, w_router, router_bias, w_gu, w_dn, layer, ws_gu, ws_dn, g, b, alpha):
    n, d = x.shape
    n_experts = w_router.shape[1]
    blk = EXPERT_BLOCK
    top_e, gate, rank, counts, tile_cnt = moe_router(xb, w_router, router_bias)
    padded = (counts + blk - 1) // blk * blk
    pend = jnp.cumsum(padded)
    pstart = pend - padded
    hot = top_e[:, :, None] == jnp.arange(n_experts, dtype=I32)
    dest = jnp.sum(jnp.where(hot, pstart, 0), axis=-1) + rank
    dest_flat = dest.T.reshape(-1)
    earlier_tiles = jnp.cumsum(tile_cnt, axis=0) - tile_cnt
    tile_local = jnp.cumsum(tile_cnt, axis=1) - tile_cnt
    tile_row = pstart[None, :] + earlier_tiles
    per_token = jnp.repeat(tile_local - earlier_tiles, n // tile_cnt.shape[0], axis=0)
    lp = jnp.sum(jnp.where(hot, per_token[None], 0), axis=-1) + rank
    lp_flat = lp.T.reshape(-1)
    n_tiles = (n * TOP_K + n_experts * (blk - 1) + blk - 1) // blk
    tile_row0 = jnp.arange(n_tiles, dtype=I32) * blk
    tile_e = jnp.minimum(jnp.sum((pend[None, :] <= tile_row0[:, None]).astype(I32), axis=1),
                         n_experts - 1)
    n_used = (pend[-1:] // blk).astype(I32)
    sub = d // 2 // LANES
    rows = n_tiles * blk
    xs = moe_dispatch(xpk, lp_flat, tile_cnt, tile_local, tile_row, (pstart + counts).astype(I32),
                      (padded - counts).astype(I32), rows)
    ys = moe_experts(xs, tile_e, n_used, w_gu, w_dn, layer, blk)
    ys = ys.reshape(rows, sub, LANES)
    return moe_combine_ln(ys, dest_flat, gate.T, x, xb, ws_gu, ws_dn, g, b, alpha)


def kernel(x_prompt, x_sample, state_hgrn, state_pool, cache_k, cache_v, ln_g, ln_b, w_in_even,
           lb_even, gnorm_even, pool_w_even, pool_scale_even, w_out_even, w_qkv_odd, rel_bias_odd,
           w_o_odd, w_router, router_bias, w_gu, w_dn, ws_gu, ws_dn):
    batch, seq, d = x_prompt.shape
    dec_batch, dec_seq, _ = x_sample.shape
    depth = ln_g.shape[0]
    n_p = batch * seq
    n_s = dec_batch * dec_seq
    width = w_in_even.shape[2] // 5
    n_heads_a = width // HEAD_DIM
    n_heads_c = d // HEAD_DIM
    alpha = (2 * depth) ** 0.25
    assert dec_seq >= POOL_PREV and seq >= max(POOL_PREV, WINDOW)

    x = (x_prompt.reshape(n_p, d), x_sample.reshape(n_s, d))
    xb = jnp.concatenate([part.astype(BF16) for part in x], axis=0)

    p = jax.nn.softmax(lb_even.astype(F32), axis=0)
    lbs = jnp.cumsum(p, axis=0) - p[0]

    hgrn_p, pool_p, k_p, v_p, hgrn_s, pool_s, k_s, v_s = [], [], [], [], [], [], [], []
    for l in range(depth):
        j = l // 2
        if l % 2 == 0:
            proj = matmul(xb, w_in_even[j].astype(BF16))
            zero_state = jnp.zeros((batch, n_heads_a, HEAD_DIM, HEAD_DIM), F32)
            mixed, sp = hgrn_mixer(proj, 0, batch, seq, zero_state, lbs[j], gnorm_even[j])
            mixed, ss = hgrn_mixer(proj, n_p, dec_batch, dec_seq,
                                   jnp.swapaxes(state_hgrn[j].astype(F32), -1, -2), lbs[j],
                                   gnorm_even[j], mixed)
            pw = pool_w_even[j].astype(BF16)
            mixed = pool_mixer(proj, 0, batch, seq, 0, jnp.zeros((batch, POOL_CARRY, width), F32),
                               pw, pool_scale_even[j], mixed)
            prev_s = jnp.pad(state_pool[j].astype(F32),
                             ((0, 0), (POOL_CARRY - POOL_PREV, 0), (0, 0)))
            mixed = pool_mixer(proj, n_p, dec_batch, dec_seq, PAST_LEN, prev_s, pw,
                               pool_scale_even[j], mixed)
            w_mix = w_out_even[j]
            hgrn_p.append(jnp.swapaxes(sp, -1, -2))
            hgrn_s.append(jnp.swapaxes(ss, -1, -2))
            pool_p.append(jnp.stack([proj[(b + 1) * seq - POOL_PREV:(b + 1) * seq, 4 * width:]
                                     for b in range(batch)]))
            pool_s.append(proj[n_p:].reshape(dec_batch, dec_seq, 5 * width)
                          [:, dec_seq - POOL_PREV:, 4 * width:])
        else:
            qkv = matmul(xb, w_qkv_odd[j].astype(BF16), out_dtype=BF16)
            mixed = band_attention_prompt(qkv, batch, seq, rel_bias_odd[j])
            mixed = band_attention_sample(qkv, n_p, dec_batch, dec_seq, cache_k, cache_v, j,
                                          rel_bias_odd[j], mixed)
            w_mix = w_o_odd[j]
            keep = min(WINDOW, seq)
            tail_p = jnp.stack([qkv[(b + 1) * seq - keep:(b + 1) * seq]
                                for b in range(batch)]).astype(F32)
            new_s = qkv[n_p:].reshape(dec_batch, dec_seq, 3 * d).astype(F32)
            k_p.append(tail_p[:, :, d:2 * d].reshape(batch, keep, n_heads_c, HEAD_DIM))
            v_p.append(tail_p[:, :, 2 * d:].reshape(batch, keep, n_heads_c, HEAD_DIM))
            k_s.append(new_s[:, :, d:2 * d].reshape(dec_batch, dec_seq, n_heads_c, HEAD_DIM))
            v_s.append(new_s[:, :, 2 * d:].reshape(dec_batch, dec_seq, n_heads_c, HEAD_DIM))
        x, xb, xpk = matmul_residual_ln(mixed, w_mix.astype(BF16), x, ln_g[l, 0], ln_b[l, 0],
                                        alpha)
        x, xb = moe_ffn_ln(x, xb, xpk, w_router[l], router_bias[l], w_gu, w_dn, l,
                           ws_gu[l].astype(BF16), ws_dn[l].astype(BF16), ln_g[l, 1], ln_b[l, 1],
                           alpha)

    y_p = x[:n_p].reshape(batch, seq, d)
    y_s = x[n_p:].reshape(dec_batch, dec_seq, d)
    return (y_p, y_s, jnp.stack(hgrn_p), jnp.stack(pool_p), jnp.stack(k_p), jnp.stack(v_p),
            jnp.stack(hgrn_s), jnp.stack(pool_s), jnp.stack(k_s), jnp.stack(v_s))
```

```python
import functools

import numpy as np
import jax
import jax.numpy as jnp
from jax import lax
from jax.experimental import pallas as pl
from jax.experimental.pallas import tpu as pltpu

F32 = jnp.float32
BF16 = jnp.bfloat16
I32 = jnp.int32
U32 = jnp.uint32

CHUNK = 64
HEAD_DIM = 128
LANES = 128
POOL_WINDOWS = (2, 4, 8, 16)
POOL_PREV = max(POOL_WINDOWS) - 1
POOL_CARRY = POOL_PREV + 1
LEFT_CHUNKS = 8
WINDOW = LEFT_CHUNKS * CHUNK
MAX_REL = 128
N_GROUPS = 8
TOPK_GROUPS = 4
TOP_K = 8
ROUTE_SCALE = 2.5
LN_EPS = 1e-5
RMS_EPS = 1e-6
PAST_LEN = 1024
NEG = -1e30

VMEM_LIMIT_BYTES = 56 * 1024 * 1024
MM_BLOCK_M = 1024
MM_BLOCK_N = 1024
LN_BLOCK_M = 512
HGRN_BLOCK_T = 128
HGRN_HEADS_INTERLEAVED = 8
ATTN_BLOCK_Q = 256
ATTN_HEADS_PER_STEP = 4
ATTN_HEADS_INTERLEAVED = 4
POOL_BLOCK_T = 512
ROUTER_BLOCK_T = 512
EXPERT_BLOCK = 512
COMBINE_BLOCK_T = 128


def _params(**kw):
    return pltpu.CompilerParams(vmem_limit_bytes=VMEM_LIMIT_BYTES, **kw)


def _sigmoid(x):
    return 1.0 / (1.0 + jnp.exp(-x))


def _silu(x):
    return x * _sigmoid(x)


def _dot(a, b):
    return jnp.dot(a, b, preferred_element_type=F32)


def _dot_nt(a, b):
    return lax.dot_general(a, b, (((1,), (1,)), ((), ())), preferred_element_type=F32)


def _dot_tn(a, b):
    return lax.dot_general(a, b, (((0,), (0,)), ((), ())), preferred_element_type=F32)


def _layer_norm(y, g, b):
    mu = jnp.mean(y, axis=-1, keepdims=True)
    yc = y - mu
    var = jnp.mean(yc * yc, axis=-1, keepdims=True)
    return yc * lax.rsqrt(var + LN_EPS) * g + b


def _mm_kernel(x_ref, w_ref, o_ref):
    o_ref[...] = _dot(x_ref[...], w_ref[...]).astype(o_ref.dtype)


def matmul(x, w, out_dtype=F32, bm=MM_BLOCK_M, bn=MM_BLOCK_N):
    m, k = x.shape
    n = w.shape[1]
    bm, bn = min(bm, m), min(bn, n)
    assert m % bm == 0 and n % bn == 0
    return pl.pallas_call(
        _mm_kernel,
        grid=(m // bm, n // bn),
        in_specs=[pl.BlockSpec((bm, k), lambda i, j: (i, 0)),
                  pl.BlockSpec((k, bn), lambda i, j: (0, j))],
        out_specs=pl.BlockSpec((bm, bn), lambda i, j: (i, j)),
        out_shape=jax.ShapeDtypeStruct((m, n), out_dtype),
        compiler_params=_params(),
        name="matmul",
    )(x, w)


def _pack_pairs(x):
    n = x.shape[1] // 2
    lo = lax.bitcast_convert_type(x[:, :n].astype(jnp.bfloat16).astype(F32), U32)
    hi = lax.bitcast_convert_type(x[:, n:].astype(jnp.bfloat16).astype(F32), U32)
    return (lo >> 16) | hi


def _unpack_pairs(u):
    lo = lax.bitcast_convert_type(u << 16, F32)
    hi = lax.bitcast_convert_type(u & jnp.uint32(0xFFFF0000), F32)
    return lo, hi


def _store_row_tiles(ref, u):
    rows = u.shape[0]
    sub = ref.shape[0] // rows
    for c in range(sub):
        ref[pl.ds(c, rows, stride=sub), :] = u[:, c * LANES:(c + 1) * LANES]


def _load_row_tiles(ref, rows):
    sub = ref.shape[0] // rows
    return jnp.concatenate([ref[pl.ds(c, rows, stride=sub), :] for c in range(sub)], axis=1)


def _mm_ln_kernel(alpha, a_ref, w_ref, res_ref, g_ref, b_ref, x_ref, xb_ref, xpk_ref):
    y = alpha * res_ref[...] + _dot(a_ref[...], w_ref[...])
    out = _layer_norm(y, g_ref[...], b_ref[...])
    x_ref[...] = out
    xb_ref[...] = out.astype(BF16)
    _store_row_tiles(xpk_ref, _pack_pairs(out))


def _mm_ln_split_kernel(alpha, n_first, a_ref, w_ref, res_a_ref, res_b_ref, g_ref, b_ref, x_ref,
                        xb_ref, xpk_ref):
    first = pl.program_id(0) < n_first
    res = jnp.where(first, res_a_ref[...], res_b_ref[...])
    y = alpha * res + _dot(a_ref[...], w_ref[...])
    out = _layer_norm(y, g_ref[...], b_ref[...])
    x_ref[...] = out
    xb_ref[...] = out.astype(BF16)
    _store_row_tiles(xpk_ref, _pack_pairs(out))


def matmul_residual_ln(a, w, res, g, b, alpha, bm=LN_BLOCK_M):
    m, k = a.shape
    d = w.shape[1]
    bm = min(bm, m)
    assert m % bm == 0 and d % (2 * LANES) == 0
    sub = d // 2 // LANES
    if isinstance(res, tuple):
        res_a, res_b = res
        n_first = res_a.shape[0] // bm
        assert res_a.shape[0] % bm == 0 and res_b.shape[0] % bm == 0
        body = functools.partial(_mm_ln_split_kernel, alpha, n_first)
        res_args = [res_a, res_b]
        res_specs = [pl.BlockSpec((bm, d), lambda i: (jnp.minimum(i, n_first - 1), 0)),
                     pl.BlockSpec((bm, d), lambda i: (jnp.maximum(i - n_first, 0), 0))]
    else:
        body = functools.partial(_mm_ln_kernel, alpha)
        res_args = [res]
        res_specs = [pl.BlockSpec((bm, d), lambda i: (i, 0))]
    return pl.pallas_call(
        body,
        grid=(m // bm,),
        in_specs=[pl.BlockSpec((bm, k), lambda i: (i, 0)),
                  pl.BlockSpec((k, d), lambda i: (0, 0)),
                  *res_specs,
                  pl.BlockSpec((1, d), lambda i: (0, 0)),
                  pl.BlockSpec((1, d), lambda i: (0, 0))],
        out_specs=[pl.BlockSpec((bm, d), lambda i: (i, 0)),
                   pl.BlockSpec((bm, d), lambda i: (i, 0)),
                   pl.BlockSpec((bm * sub, LANES), lambda i: (i, 0))],
        out_shape=[jax.ShapeDtypeStruct((m, d), F32), jax.ShapeDtypeStruct((m, d), BF16),
                   jax.ShapeDtypeStruct((m * sub, LANES), U32)],
        compiler_params=_params(),
        name="matmul_residual_ln",
    )(a, w, *res_args, g.reshape(1, d), b.reshape(1, d))


def _hgrn_tables(c):
    t = np.arange(c)
    levels = []
    b = c // 2
    while b >= 1:
        levels.append(b)
        b //= 2
    sel = [(t[None, :] <= t[:, None]).astype(np.float32)]
    lvl = np.where(np.eye(c, dtype=bool), 0, -1).astype(np.int32)
    for l, b in enumerate(levels):
        blk = t // b
        odd = (blk % 2) == 1
        ref_row = np.where(odd, blk * b - 1, blk * b + b - 1)
        sel.append((t[None, :] <= ref_row[:, None]).astype(np.float32))
        same = (t[:, None] // (2 * b)) == (t[None, :] // (2 * b))
        lvl = np.where(same & odd[:, None] & (~odd)[None, :], l + 1, lvl)
    sel = np.concatenate(sel, axis=0)
    return np.concatenate([sel, sel, sel], axis=1), lvl


def _hgrn_kernel(n_heads, c, q_ref, f_ref, i_ref, g_ref, s0_ref, lb_ref, gn_ref, sel_ref, lvl_ref,
                 o_ref, sfin_ref, state_ref, cums_ref, qw_ref, kw_ref):
    j = pl.program_id(1)

    @pl.when(j == 0)
    def _():
        state_ref[...] = s0_ref[0]

    n_lvl = qw_ref.shape[0]
    lb = lb_ref[...]
    f = lb + (1.0 - lb) * _sigmoid(f_ref[...])
    k = 1.0 - f
    q = _silu(q_ref[...])
    lg = jnp.log(f)
    hi = lg.astype(BF16)
    rest = lg - hi.astype(F32)
    mid = rest.astype(BF16)
    lo = (rest - mid.astype(F32)).astype(BF16)
    cums_ref[...] = _dot(sel_ref[...], jnp.concatenate([hi, mid, lo], axis=0))
    cum = cums_ref[0:c]
    qw_ref[0] = q.astype(BF16)
    kw_ref[0] = k.astype(BF16)
    for l in range(1, n_lvl):
        w = jnp.exp(-jnp.abs(cum - cums_ref[l * c:(l + 1) * c]))
        qw_ref[l] = (q * w).astype(BF16)
        kw_ref[l] = (k * w).astype(BF16)
    last = cum[c - 1:c, :]
    q_in = (q * jnp.exp(cum)).astype(BF16)
    k_end = (k * jnp.exp(last - cum)).astype(BF16)
    decay = jnp.exp(last)
    lvl = lvl_ref[...]
    for h0 in range(0, n_heads, HGRN_HEADS_INTERLEAVED):
        heads = list(range(h0, min(h0 + HGRN_HEADS_INTERLEAVED, n_heads)))
        cols = [slice(h * HEAD_DIM, (h + 1) * HEAD_DIM) for h in heads]
        st = [state_ref[h] for h in heads]
        o = [_dot_nt(q_in[:, hs], s.astype(BF16)) for hs, s in zip(cols, st)]
        scores = [jnp.zeros((c, c), F32) for _ in heads]
        for l in range(n_lvl):
            scores = [jnp.where(lvl == l, _dot_nt(qw_ref[l, :, hs], kw_ref[l, :, hs]), sc)
                      for hs, sc in zip(cols, scores)]
        v = [i_ref[:, hs].astype(BF16) for hs in cols]
        o = [oh + _dot(sc.astype(BF16), vh) for oh, sc, vh in zip(o, scores, v)]
        for h, hs, s, vh in zip(heads, cols, st, v):
            state_ref[h] = s * decay[:, hs] + _dot_tn(vh, k_end[:, hs])
        o = [oh * lax.rsqrt(jnp.mean(oh * oh, axis=-1, keepdims=True) + RMS_EPS) * gn_ref[...]
             for oh in o]
        for hs, oh in zip(cols, o):
            o_ref[:, hs] = (oh * _silu(g_ref[:, hs])).astype(o_ref.dtype)

    @pl.when(j == pl.num_programs(1) - 1)
    def _():
        sfin_ref[0] = state_ref[...]


def _fill_in_place(body, n_inputs, base):
    if base is None:
        return body
    return lambda *refs: body(*refs[:n_inputs], *refs[n_inputs + 1:])


def hgrn_mixer(proj, row0, batch, seq, s0_t, lb, gnorm, base=None):
    width = proj.shape[1] // 5
    n_heads = width // HEAD_DIM
    c = HGRN_BLOCK_T if seq % HGRN_BLOCK_T == 0 else seq
    n = seq // c
    assert row0 % c == 0 and c & (c - 1) == 0
    blk0 = row0 // c
    sel, lvl = _hgrn_tables(c)
    n_lvl = sel.shape[0] // c

    def sec(s):
        return pl.BlockSpec((c, width), lambda b, j: (blk0 + b * n + j, s))

    state_spec = pl.BlockSpec((1, n_heads, HEAD_DIM, HEAD_DIM), lambda b, j: (b, 0, 0, 0))
    inputs = [proj, proj, proj, proj, s0_t, lb.reshape(1, width), gnorm.reshape(1, HEAD_DIM),
              jnp.asarray(sel).astype(BF16), jnp.asarray(lvl)]
    in_specs = [sec(0), sec(1), sec(2), sec(3), state_spec,
                pl.BlockSpec((1, width), lambda b, j: (0, 0)),
                pl.BlockSpec((1, HEAD_DIM), lambda b, j: (0, 0)),
                pl.BlockSpec(sel.shape, lambda b, j: (0, 0)),
                pl.BlockSpec(lvl.shape, lambda b, j: (0, 0))]
    return pl.pallas_call(
        _fill_in_place(functools.partial(_hgrn_kernel, n_heads, c), len(inputs), base),
        grid=(batch, n),
        in_specs=in_specs + ([] if base is None else [pl.BlockSpec(memory_space=pl.ANY)]),
        out_specs=[pl.BlockSpec((c, width), lambda b, j: (blk0 + b * n + j, 0)), state_spec],
        out_shape=[jax.ShapeDtypeStruct((proj.shape[0], 2 * width), BF16),
                   jax.ShapeDtypeStruct((batch, n_heads, HEAD_DIM, HEAD_DIM), F32)],
        input_output_aliases={} if base is None else {len(inputs): 0},
        scratch_shapes=[pltpu.VMEM((n_heads, HEAD_DIM, HEAD_DIM), F32),
                        pltpu.VMEM((n_lvl * c, width), F32),
                        pltpu.VMEM((n_lvl, c, width), BF16),
                        pltpu.VMEM((n_lvl, c, width), BF16)],
        compiler_params=_params(),
        name="hgrn_mixer",
    )(*inputs, *([] if base is None else [base]))


def _pool_kernel(pos0, bt, group, u_ref, prev_ref, w_ref, scale_ref, o_ref, ext_ref):
    j = pl.program_id(1)

    @pl.when(j == 0)
    def _():
        ext_ref[0:POOL_CARRY] = prev_ref[0]

    u = u_ref[...]
    ext_ref[POOL_CARRY:POOL_CARRY + bt] = u
    pos = pos0 + j * bt + lax.broadcasted_iota(I32, (bt, 1), 0)
    for gi, win in enumerate(POOL_WINDOWS):
        cs = slice(gi * group, (gi + 1) * group)
        s = ext_ref[:, cs]
        step = 1
        while step < win:
            s = s + pltpu.roll(s, step, axis=0)
            step *= 2
        cnt = jnp.minimum(pos + 1, win).astype(F32)
        d = s[POOL_CARRY:] / cnt - u[:, cs]
        y = _dot(d.astype(BF16), w_ref[gi]) * scale_ref[:, cs]
        o_ref[:, cs] = y.astype(o_ref.dtype)
    ext_ref[0:POOL_CARRY] = ext_ref[bt:bt + POOL_CARRY]


def pool_mixer(proj, row0, batch, seq, pos0, prev, pool_w, pool_scale, base):
    width = proj.shape[1] // 5
    group = width // len(POOL_WINDOWS)
    bt = min(POOL_BLOCK_T, seq)
    n = seq // bt
    assert seq % bt == 0 and row0 % bt == 0 and sum(POOL_WINDOWS) // 2 <= POOL_CARRY
    blk0 = row0 // bt
    inputs = [proj, prev, pool_w, pool_scale.reshape(1, width)]
    return pl.pallas_call(
        _fill_in_place(functools.partial(_pool_kernel, pos0, bt, group), len(inputs), base),
        grid=(batch, n),
        in_specs=[pl.BlockSpec((bt, width), lambda b, j: (blk0 + b * n + j, 4)),
                  pl.BlockSpec((1, POOL_CARRY, width), lambda b, j: (b, 0, 0)),
                  pl.BlockSpec(pool_w.shape, lambda b, j: (0, 0, 0)),
                  pl.BlockSpec((1, width), lambda b, j: (0, 0)),
                  pl.BlockSpec(memory_space=pl.ANY)],
        out_specs=pl.BlockSpec((bt, width), lambda b, j: (blk0 + b * n + j, 1)),
        out_shape=jax.ShapeDtypeStruct(base.shape, base.dtype),
        input_output_aliases={len(inputs): 0},
        scratch_shapes=[pltpu.VMEM((bt + POOL_CARRY, width), F32)],
        compiler_params=_params(),
        name="pool_mixer",
    )(*inputs, base)


def _rel_bias(table, dist):
    idx = jnp.clip(dist, -(CHUNK - 1), MAX_REL) + (CHUNK - 1)
    return table[:, idx].astype(F32)


def _attn_prompt_kernel(bq, hps, scale, q_ref, k0_ref, k1_ref, k2_ref, v0_ref, v1_ref, v2_ref,
                        vals_ref, o_ref, bias_ref, kcat_ref, vcat_ref):
    b = pl.program_id(1)
    i = pl.program_id(2)
    band = WINDOW + CHUNK

    @pl.when(jnp.logical_and(b == 0, i == 0))
    def _():
        span = vals_ref.shape[-1]
        for hh in range(hps):
            rows = jnp.broadcast_to(vals_ref[0, hh:hh + 1, :], (CHUNK, span))
            toep = pltpu.roll(rows, span - (CHUNK - 1), 1, stride=1, stride_axis=0)
            bias_ref[hh] = toep[:, :band]

    for p, (k_ref, v_ref) in enumerate(((k0_ref, v0_ref), (k1_ref, v1_ref), (k2_ref, v2_ref))):
        kcat_ref[p * bq:(p + 1) * bq] = k_ref[...]
        vcat_ref[p * bq:(p + 1) * bq] = v_ref[...]

    col = lax.broadcasted_iota(I32, (CHUNK, band), 1)
    starts = [c * CHUNK for c in range(bq // CHUNK)]
    for h0 in range(0, hps, ATTN_HEADS_INTERLEAVED):
        heads = range(h0, min(h0 + ATTN_HEADS_INTERLEAVED, hps))
        work = [(hh, slice(hh * HEAD_DIM, (hh + 1) * HEAD_DIM), r0)
                for hh in heads for r0 in starts]
        s = [_dot_nt(q_ref[r0:r0 + CHUNK, hs], kcat_ref[r0:r0 + band, hs]) * scale + bias_ref[hh]
             for hh, hs, r0 in work]
        s = [jnp.where(col >= WINDOW - r0 - i * bq, sc, NEG) for (_, _, r0), sc in zip(work, s)]
        m = [jnp.max(sc, axis=-1, keepdims=True) for sc in s]
        e = [jnp.exp(sc - mc) for sc, mc in zip(s, m)]
        inv = [1.0 / jnp.sum(ec, axis=-1, keepdims=True) for ec in e]
        o = [_dot((ec * ic).astype(BF16), vcat_ref[r0:r0 + band, hs])
             for (_, hs, r0), ec, ic in zip(work, e, inv)]
        for n, hh in enumerate(heads):
            o_ref[:, hh * HEAD_DIM:(hh + 1) * HEAD_DIM] = jnp.concatenate(
                o[n * len(starts):(n + 1) * len(starts)], axis=0).astype(o_ref.dtype)


def band_attention_prompt(qkv, batch, seq, rel_bias):
    d = qkv.shape[1] // 3
    n_heads = d // HEAD_DIM
    bq = ATTN_BLOCK_Q
    hps = min(ATTN_HEADS_PER_STEP, n_heads)
    assert seq % bq == 0 and WINDOW == 2 * bq and bq % CHUNK == 0 and n_heads % hps == 0
    nq = seq // bq
    ng = n_heads // hps
    band = WINDOW + CHUNK
    span = -(-(band + CHUNK - 1) // LANES) * LANES
    offs = jnp.arange(span) - (CHUNK - 1)
    vals = _rel_bias(rel_bias, WINDOW - offs).reshape(ng, hps, span)

    def kv(sec, p):
        return pl.BlockSpec((bq, hps * HEAD_DIM),
                            lambda g, b, i: (b * nq + jnp.maximum(i - 2 + p, 0), sec * ng + g))

    return pl.pallas_call(
        functools.partial(_attn_prompt_kernel, bq, hps, HEAD_DIM ** -0.5),
        grid=(ng, batch, nq),
        in_specs=[pl.BlockSpec((bq, hps * HEAD_DIM), lambda g, b, i: (b * nq + i, g)),
                  kv(1, 0), kv(1, 1), kv(1, 2), kv(2, 0), kv(2, 1), kv(2, 2),
                  pl.BlockSpec((1, hps, span), lambda g, b, i: (g, 0, 0))],
        out_specs=pl.BlockSpec((bq, hps * HEAD_DIM), lambda g, b, i: (b * nq + i, g)),
        out_shape=jax.ShapeDtypeStruct((qkv.shape[0], d), BF16),
        scratch_shapes=[pltpu.VMEM((hps, CHUNK, band), F32),
                        pltpu.VMEM((3 * bq, hps * HEAD_DIM), BF16),
                        pltpu.VMEM((3 * bq, hps * HEAD_DIM), BF16)],
        compiler_params=_params(),
        name="band_attention_prompt",
    )(qkv, qkv, qkv, qkv, qkv, qkv, qkv, vals)


def _attn_sample_kernel(n_heads, lc, scale, qkv_ref, ck_ref, cv_ref, bias_ref, base_ref, o_ref):
    del base_ref
    d = n_heads * HEAD_DIM

    def cols(sec, h):
        return slice(sec * d + h * HEAD_DIM, sec * d + (h + 1) * HEAD_DIM)

    for h0 in range(0, n_heads, ATTN_HEADS_INTERLEAVED):
        heads = range(h0, min(h0 + ATTN_HEADS_INTERLEAVED, n_heads))
        sc = [_dot_nt(qkv_ref[:, cols(0, h)], ck_ref[0, 0, :, h, :].astype(BF16)) * scale
              + bias_ref[h, :, 0:lc] for h in heads]
        sn = [_dot_nt(qkv_ref[:, cols(0, h)], qkv_ref[:, cols(1, h)]) * scale + bias_ref[h, :, lc:]
              for h in heads]
        m = [jnp.maximum(jnp.max(a, axis=-1, keepdims=True), jnp.max(b, axis=-1, keepdims=True))
             for a, b in zip(sc, sn)]
        ec = [jnp.exp(a - mh) for a, mh in zip(sc, m)]
        en = [jnp.exp(b - mh) for b, mh in zip(sn, m)]
        inv = [1.0 / (jnp.sum(a, axis=-1, keepdims=True) + jnp.sum(b, axis=-1, keepdims=True))
               for a, b in zip(ec, en)]
        for h, a, b, ih in zip(heads, ec, en, inv):
            o = (_dot((a * ih).astype(BF16), cv_ref[0, 0, :, h, :].astype(BF16))
                 + _dot((b * ih).astype(BF16), qkv_ref[:, cols(2, h)]))
            o_ref[:, cols(0, h)] = o.astype(o_ref.dtype)


def band_attention_sample(qkv, row0, batch, seq, cache_k, cache_v, layer, rel_bias, base):
    d = qkv.shape[1] // 3
    n_heads = d // HEAD_DIM
    lc = cache_k.shape[2]
    assert row0 % seq == 0 and lc % 128 == 0
    blk0 = row0 // seq
    dist = jnp.arange(seq)[:, None] + lc - jnp.arange(lc + seq)[None, :]
    bias = _rel_bias(rel_bias, dist)
    cache_spec = pl.BlockSpec((1, 1, lc, n_heads, HEAD_DIM), lambda b: (layer, b, 0, 0, 0))
    return pl.pallas_call(
        functools.partial(_attn_sample_kernel, n_heads, lc, HEAD_DIM ** -0.5),
        grid=(batch,),
        in_specs=[pl.BlockSpec((seq, 3 * d), lambda b: (blk0 + b, 0)),
                  cache_spec, cache_spec,
                  pl.BlockSpec(bias.shape, lambda b: (0, 0, 0)),
                  pl.BlockSpec(memory_space=pl.ANY)],
        out_specs=pl.BlockSpec((seq, d), lambda b: (blk0 + b, 0)),
        out_shape=jax.ShapeDtypeStruct(base.shape, base.dtype),
        input_output_aliases={4: 0},
        compiler_params=_params(),
        name="band_attention_sample",
    )(qkv, cache_k, cache_v, bias, base)


def _first_max(vals, idx, sentinel):
    m = jnp.max(vals, axis=0, keepdims=True)
    first = jnp.min(jnp.where(vals == m, idx, sentinel), axis=0, keepdims=True)
    return m, first


def _stack_rows(rows, lanes):
    ridx = lax.broadcasted_iota(I32, (len(rows), lanes), 0)
    out = jnp.zeros((len(rows), lanes), rows[0].dtype)
    for r, row in enumerate(rows):
        out = jnp.where(ridx == r, row, out)
    return out


def _router_kernel(n_experts, bt, x_ref, wt_ref, bias_ref, e_ref, gate_ref, rank_ref, cnt_ref,
                   tile_cnt_ref, carry_ref):
    i = pl.program_id(0)

    @pl.when(i == 0)
    def _():
        carry_ref[...] = jnp.zeros_like(carry_ref)

    per = n_experts // N_GROUPS
    logits = _dot_nt(wt_ref[...], x_ref[...])
    scores = _sigmoid(logits)
    sel = scores + bias_ref[:, 0:1]

    sub = lax.broadcasted_iota(I32, (per, bt), 0)
    grp_rows = []
    for g in range(N_GROUPS):
        blk = sel[g * per:(g + 1) * per]
        m1, i1 = _first_max(blk, sub, per)
        m2 = jnp.max(jnp.where(sub == i1, -jnp.inf, blk), axis=0, keepdims=True)
        grp_rows.append(m1 + m2)
    grp = _stack_rows(grp_rows, bt)
    gidx = lax.broadcasted_iota(I32, (N_GROUPS, bt), 0)
    keep = jnp.zeros((N_GROUPS, bt), F32)
    for _ in range(TOPK_GROUPS):
        _, first = _first_max(grp, gidx, N_GROUPS)
        hit = gidx == first
        keep = jnp.where(hit, 1.0, keep)
        grp = jnp.where(hit, -jnp.inf, grp)
    masked = jnp.concatenate(
        [jnp.where(keep[g:g + 1] > 0.0, sel[g * per:(g + 1) * per], -jnp.inf)
         for g in range(N_GROUPS)], axis=0)

    eidx = lax.broadcasted_iota(I32, (n_experts, bt), 0)
    chosen = jnp.zeros((n_experts, bt), F32)
    picks = []
    gates = []
    for _ in range(TOP_K):
        _, first = _first_max(masked, eidx, n_experts)
        hit = eidx == first
        picks.append(first)
        gates.append(jnp.sum(jnp.where(hit, scores, 0.0), axis=0, keepdims=True))
        chosen = jnp.where(hit, 1.0, chosen)
        masked = jnp.where(hit, -jnp.inf, masked)
    gate = _stack_rows(gates, bt)
    gate = gate / jnp.sum(gate, axis=0, keepdims=True) * ROUTE_SCALE

    earlier = (lax.broadcasted_iota(I32, (bt, bt), 0) < lax.broadcasted_iota(I32, (bt, bt), 1))
    before = (_dot(chosen.astype(BF16), jnp.where(earlier, 1.0, 0.0).astype(BF16))
              + carry_ref[:, 0:1])
    ranks = [jnp.sum(jnp.where(eidx == p, before, 0.0), axis=0, keepdims=True) for p in picks]

    e_ref[...] = _stack_rows(picks, bt)
    gate_ref[...] = gate
    rank_ref[...] = _stack_rows(ranks, bt).astype(I32)
    in_tile = jnp.sum(chosen, axis=1, keepdims=True)
    tile_cnt_ref[0] = jnp.broadcast_to(in_tile, tile_cnt_ref.shape[1:])
    carry_ref[...] = carry_ref[...] + in_tile
    cnt_ref[...] = carry_ref[...]


def moe_router(x, w_router, router_bias, bt=ROUTER_BLOCK_T):
    n, d = x.shape
    n_experts = w_router.shape[1]
    bt = min(bt, n)
    assert n % bt == 0 and n_experts % N_GROUPS == 0
    lanes = LANES
    tok = pl.BlockSpec((TOP_K, bt), lambda i: (0, i))
    e, gate, rank, cnt, tile_cnt = pl.pallas_call(
        functools.partial(_router_kernel, n_experts, bt),
        grid=(n // bt,),
        in_specs=[pl.BlockSpec((bt, d), lambda i: (i, 0)),
                  pl.BlockSpec((n_experts, d), lambda i: (0, 0)),
                  pl.BlockSpec((n_experts, lanes), lambda i: (0, 0))],
        out_specs=[tok, tok, tok, pl.BlockSpec((n_experts, lanes), lambda i: (0, 0)),
                   pl.BlockSpec((1, n_experts, lanes), lambda i: (i, 0, 0))],
        out_shape=[jax.ShapeDtypeStruct((TOP_K, n), I32), jax.ShapeDtypeStruct((TOP_K, n), F32),
                   jax.ShapeDtypeStruct((TOP_K, n), I32),
                   jax.ShapeDtypeStruct((n_experts, lanes), F32),
                   jax.ShapeDtypeStruct((n // bt, n_experts, lanes), F32)],
        scratch_shapes=[pltpu.VMEM((n_experts, lanes), F32)],
        compiler_params=_params(),
        name="moe_router",
    )(x, w_router.T.astype(BF16),
      jnp.broadcast_to(router_bias.astype(F32)[:, None], (n_experts, lanes)))
    return e, gate, rank, cnt[:, 0].astype(I32), tile_cnt[:, :, 0].astype(I32)


def _dispatch_kernel(bt, n_experts, sub, fill_start_ref, fill_len_ref, cnt_ref, local_ref, row_ref,
                     lp_ref, x_ref, xs_ref, sort_ref, zero_ref, sems, fill_sem):
    i = pl.program_id(0)
    last = pl.num_programs(0) - 1
    slot = i % 2
    piece_sizes = [1 << b for b in range(bt.bit_length() - 1, -1, -1)]

    def wait_slot(s):
        pltpu.make_async_copy(sort_ref.at[s], sort_ref.at[s], sems.at[s]).wait()

    @pl.when(i == 0)
    def _():
        zero_ref[...] = jnp.zeros_like(zero_ref)

        max_piece = zero_ref.shape[0] // sub
        zero_pieces = [1 << b for b in range(max_piece.bit_length() - 1, -1, -1)]

        def fill(starting, e, carry):
            base = fill_start_ref[e]
            n_rows = fill_len_ref[e]
            done = 0
            for p in zero_pieces:
                take = (n_rows & p) != 0

                @pl.when(take)
                def _(done=done, p=p):
                    row0 = pl.multiple_of((base + done) * sub, sub)
                    copy = pltpu.make_async_copy(zero_ref.at[pl.ds(0, p * sub)],
                                                 xs_ref.at[pl.ds(row0, p * sub)], fill_sem)
                    if starting:
                        copy.start()
                    else:
                        copy.wait()

                done = done + jnp.where(take, p, 0)
            return carry

        lax.fori_loop(0, n_experts, functools.partial(fill, True), 0)
        lax.fori_loop(0, n_experts, functools.partial(fill, False), 0)

    @pl.when(i >= 2)
    def _():
        wait_slot(slot)

    for t in range(bt):
        tile = x_ref[pl.ds(t * sub, sub), :]
        for k in range(TOP_K):
            at = pl.multiple_of(lp_ref[t * TOP_K + k] * sub, sub)
            sort_ref[slot, pl.ds(at, sub), :] = tile

    def send(e, carry):
        n_rows = cnt_ref[i * n_experts + e]
        src0 = local_ref[i * n_experts + e]
        dst0 = row_ref[i * n_experts + e]
        done = 0
        for p in piece_sizes:
            take = (n_rows & p) != 0

            @pl.when(take)
            def _(done=done, p=p):
                pltpu.make_async_copy(
                    sort_ref.at[slot, pl.ds(pl.multiple_of((src0 + done) * sub, sub), p * sub)],
                    xs_ref.at[pl.ds(pl.multiple_of((dst0 + done) * sub, sub), p * sub)],
                    sems.at[slot]).start()

            done = done + jnp.where(take, p, 0)
        return carry

    lax.fori_loop(0, n_experts, send, 0)

    @pl.when(i == last)
    def _():
        @pl.when(i >= 1)
        def _():
            wait_slot(1 - slot)
        wait_slot(slot)


def moe_dispatch(xpk, lp_flat, tile_cnt, tile_local, tile_row, fill_start, fill_len, rows,
                 bt=ROUTER_BLOCK_T):
    n = lp_flat.shape[0] // TOP_K
    sub = xpk.shape[0] // n
    bt = min(bt, n)
    assert n % bt == 0 and bt & (bt - 1) == 0
    n_experts = fill_start.shape[0]
    return pl.pallas_call(
        functools.partial(_dispatch_kernel, bt, n_experts, sub),
        grid_spec=pltpu.PrefetchScalarGridSpec(
            num_scalar_prefetch=5,
            grid=(n // bt,),
            in_specs=[pl.BlockSpec((bt * TOP_K,), lambda i, *_: (i,), memory_space=pltpu.SMEM),
                      pl.BlockSpec((bt * sub, LANES), lambda i, *_: (i, 0))],
            out_specs=pl.BlockSpec(memory_space=pl.ANY),
            scratch_shapes=[pltpu.VMEM((2, bt * TOP_K * sub, LANES), U32),
                            pltpu.VMEM((EXPERT_BLOCK // 2 * sub, LANES), U32),
                            pltpu.SemaphoreType.DMA((2,)), pltpu.SemaphoreType.DMA]),
        out_shape=jax.ShapeDtypeStruct((rows * sub, LANES), U32),
        compiler_params=_params(),
        name="moe_dispatch",
    )(fill_start, fill_len, tile_cnt.reshape(-1), tile_local.reshape(-1), tile_row.reshape(-1),
      lp_flat, xpk)


def _expert_kernel(blk, tile_e_ref, n_used_ref, xs_ref, wgu_ref, wdn_ref, ys_ref, wgu_b, wdn_b):
    i = pl.program_id(0)
    de = wdn_b.shape[0]
    changed = jnp.logical_or(i == 0, tile_e_ref[i] != tile_e_ref[jnp.maximum(i - 1, 0)])

    @pl.when(changed)
    def _():
        wgu_b[...] = wgu_ref[0, 0].astype(BF16)
        wdn_b[...] = wdn_ref[0, 0].astype(BF16)

    @pl.when(i < n_used_ref[0])
    def _():
        lo, hi = _unpack_pairs(_load_row_tiles(xs_ref, blk))
        x = jnp.concatenate([lo.astype(BF16), hi.astype(BF16)], axis=1)
        gu = _dot(x, wgu_b[...])
        a = _silu(gu[:, :de]) * gu[:, de:]
        _store_row_tiles(ys_ref, _pack_pairs(_dot(a.astype(BF16), wdn_b[...])))

    @pl.when(i >= n_used_ref[0])
    def _():
        ys_ref[...] = jnp.zeros_like(ys_ref)


def moe_experts(xs, tile_e, n_used, w_gu, w_dn, layer, blk=EXPERT_BLOCK):
    sub = w_gu.shape[2] // 2 // LANES
    d = 2 * sub * LANES
    rows = xs.shape[0] // sub
    n_tiles = rows // blk
    de = w_dn.shape[2]
    return pl.pallas_call(
        functools.partial(_expert_kernel, blk),
        grid_spec=pltpu.PrefetchScalarGridSpec(
            num_scalar_prefetch=2,
            grid=(n_tiles,),
            in_specs=[pl.BlockSpec((blk * sub, LANES),
                                   lambda i, te, nu: (jnp.minimum(i, nu[0] - 1), 0)),
                      pl.BlockSpec((1, 1, d, 2 * de), lambda i, te, nu: (layer, te[i], 0, 0)),
                      pl.BlockSpec((1, 1, de, d), lambda i, te, nu: (layer, te[i], 0, 0))],
            out_specs=pl.BlockSpec((blk * sub, LANES), lambda i, te, nu: (i, 0)),
            scratch_shapes=[pltpu.VMEM((d, 2 * de), BF16), pltpu.VMEM((de, d), BF16)]),
        out_shape=jax.ShapeDtypeStruct((rows * sub, LANES), U32),
        compiler_params=_params(),
        name="moe_experts",
    )(tile_e, n_used, xs, w_gu, w_dn)


def _combine_kernel(alpha, bt, dest_ref, dest_next_ref, ys_ref, gate_ref, x_ref, xb_ref, wsgu_ref,
                    wsdn_ref, g_ref, b_ref, xo_ref, xbo_ref, buf_a, buf_b, sem_a, sem_b):
    i = pl.program_id(0)
    de = wsdn_ref.shape[0]
    sub = ys_ref.shape[1]
    per_tile = bt * TOP_K

    def row_copy(idx_ref, base, buf, sem, t, k):
        return pltpu.make_async_copy(ys_ref.at[idx_ref[base + t * TOP_K + k]],
                                     buf.at[k, pl.ds(t * sub, sub)], sem)

    def issue_unrolled(idx_ref, base, buf, sem):
        for t in range(bt):
            for k in range(TOP_K):
                row_copy(idx_ref, base, buf, sem, t, k).start(priority=k % 2)

    def drain(buf, other, sem):
        pltpu.make_async_copy(other, buf, sem).wait()

    def compute(r0, buf):
        rows = pl.ds(r0, bt)
        sgu = _dot(xb_ref[rows, :], wsgu_ref[...])
        shared = _dot((_silu(sgu[:, :de]) * sgu[:, de:]).astype(BF16), wsdn_ref[...])
        routed_lo, routed_hi = None, None
        for k in range(TOP_K):
            gate = gate_ref[rows, k:k + 1]
            lo, hi = _unpack_pairs(_load_row_tiles(buf.at[k], bt))
            routed_lo = gate * lo if k == 0 else routed_lo + gate * lo
            routed_hi = gate * hi if k == 0 else routed_hi + gate * hi
        routed = jnp.concatenate([routed_lo, routed_hi], axis=1)
        out = _layer_norm(alpha * x_ref[rows, :] + (routed + shared), g_ref[...], b_ref[...])
        xo_ref[rows, :] = out
        xbo_ref[rows, :] = out.astype(BF16)

    @pl.when(i == 0)
    def _():
        def body(t, carry):
            for k in range(TOP_K):
                row_copy(dest_ref, 0, buf_a, sem_a, t, k).start()
            return carry
        lax.fori_loop(0, bt, body, 0)

    drain(buf_a, buf_b, sem_a)
    issue_unrolled(dest_ref, per_tile, buf_b, sem_b)
    compute(0, buf_a)
    drain(buf_b, buf_a, sem_b)
    issue_unrolled(dest_next_ref, 0, buf_a, sem_a)
    compute(bt, buf_b)

    @pl.when(i == pl.num_programs(0) - 1)
    def _():
        drain(buf_a, buf_b, sem_a)


def moe_combine_ln(ys, dest_flat, gate_tm, x, xb, ws_gu, ws_dn, g, b, alpha, bt=COMBINE_BLOCK_T):
    n, d = x.shape
    bt = min(bt, n // 2)
    assert n % (2 * bt) == 0
    n_tiles = n // bt
    de = ws_dn.shape[0]
    sub = ys.shape[1]
    row = pl.BlockSpec((2 * bt, d), lambda i: (i, 0))
    vec = pl.BlockSpec((1, d), lambda i: (0, 0))
    return pl.pallas_call(
        functools.partial(_combine_kernel, alpha, bt),
        grid=(n_tiles // 2,),
        in_specs=[pl.BlockSpec((2 * bt * TOP_K,), lambda i: (i,), memory_space=pltpu.SMEM),
                  pl.BlockSpec((bt * TOP_K,), lambda i: (jnp.minimum(2 * i + 2, n_tiles - 1),),
                               memory_space=pltpu.SMEM),
                  pl.BlockSpec(memory_space=pl.ANY),
                  pl.BlockSpec((2 * bt, TOP_K), lambda i: (i, 0)),
                  row, row,
                  pl.BlockSpec((d, 2 * de), lambda i: (0, 0)),
                  pl.BlockSpec((de, d), lambda i: (0, 0)),
                  vec, vec],
        out_specs=[row, row],
        out_shape=[jax.ShapeDtypeStruct((n, d), F32), jax.ShapeDtypeStruct((n, d), BF16)],
        scratch_shapes=[pltpu.VMEM((TOP_K, bt * sub, LANES), U32),
                        pltpu.VMEM((TOP_K, bt * sub, LANES), U32),
                        pltpu.SemaphoreType.DMA, pltpu.SemaphoreType.DMA],
        compiler_params=_params(),
        name="moe_combine_ln",
    )(dest_flat, dest_flat, ys, gate_tm, x, xb, ws_gu, ws_dn, g.reshape(1, d), b.reshape(1, d))


def moe_ffn_ln(x, xb, xpk, w_router, router_bias, w_gu, w_dn, layer, ws_gu, ws_dn, g, b, alpha):
    n, d = x.shape
    n_experts = w_router.shape[1]
    blk = EXPERT_BLOCK
    top_e, gate, rank, counts, tile_cnt = moe_router(xb, w_router, router_bias)
    padded = (counts + blk - 1) // blk * blk
    pend = jnp.cumsum(padded)
    pstart = pend - padded
    hot = top_e[:, :, None] == jnp.arange(n_experts, dtype=I32)
    dest = jnp.sum(jnp.where(hot, pstart, 0), axis=-1) + rank
    dest_flat = dest.T.reshape(-1)
    earlier_tiles = jnp.cumsum(tile_cnt, axis=0) - tile_cnt
    tile_local = jnp.cumsum(tile_cnt, axis=1) - tile_cnt
    tile_row = pstart[None, :] + earlier_tiles
    per_token = jnp.repeat(tile_local - earlier_tiles, n // tile_cnt.shape[0], axis=0)
    lp = jnp.sum(jnp.where(hot, per_token[None], 0), axis=-1) + rank
    lp_flat = lp.T.reshape(-1)
    n_tiles = (n * TOP_K + n_experts * (blk - 1) + blk - 1) // blk
    tile_row0 = jnp.arange(n_tiles, dtype=I32) * blk
    tile_e = jnp.minimum(jnp.sum((pend[None, :] <= tile_row0[:, None]).astype(I32), axis=1),
                         n_experts - 1)
    n_used = (pend[-1:] // blk).astype(I32)
    sub = d // 2 // LANES
    rows = n_tiles * blk
    xs = moe_dispatch(xpk, lp_flat, tile_cnt, tile_local, tile_row, (pstart + counts).astype(I32),
                      (padded - counts).astype(I32), rows)
    ys = moe_experts(xs, tile_e, n_used, w_gu, w_dn, layer, blk)
    ys = ys.reshape(rows, sub, LANES)
    return moe_combine_ln(ys, dest_flat, gate.T, x, xb, ws_gu, ws_dn, g, b, alpha)


def kernel(x_prompt, x_sample, state_hgrn, state_pool, cache_k, cache_v, ln_g, ln_b, w_in_even,
           lb_even, gnorm_even, pool_w_even, pool_scale_even, w_out_even, w_qkv_odd, rel_bias_odd,
           w_o_odd, w_router, router_bias, w_gu, w_dn, ws_gu, ws_dn):
    batch, seq, d = x_prompt.shape
    dec_batch, dec_seq, _ = x_sample.shape
    depth = ln_g.shape[0]
    n_p = batch * seq
    n_s = dec_batch * dec_seq
    width = w_in_even.shape[2] // 5
    n_heads_a = width // HEAD_DIM
    n_heads_c = d // HEAD_DIM
    alpha = (2 * depth) ** 0.25
    assert dec_seq >= POOL_PREV and seq >= max(POOL_PREV, WINDOW)

    x = (x_prompt.reshape(n_p, d), x_sample.reshape(n_s, d))
    xb = jnp.concatenate([part.astype(BF16) for part in x], axis=0)

    p = jax.nn.softmax(lb_even.astype(F32), axis=0)
    lbs = jnp.cumsum(p, axis=0) - p[0]

    hgrn_p, pool_p, k_p, v_p, hgrn_s, pool_s, k_s, v_s = [], [], [], [], [], [], [], []
    for l in range(depth):
        j = l // 2
        if l % 2 == 0:
            proj = matmul(xb, w_in_even[j].astype(BF16))
            zero_state = jnp.zeros((batch, n_heads_a, HEAD_DIM, HEAD_DIM), F32)
            mixed, sp = hgrn_mixer(proj, 0, batch, seq, zero_state, lbs[j], gnorm_even[j])
            mixed, ss = hgrn_mixer(proj, n_p, dec_batch, dec_seq,
                                   jnp.swapaxes(state_hgrn[j].astype(F32), -1, -2), lbs[j],
                                   gnorm_even[j], mixed)
            pw = pool_w_even[j].astype(BF16)
            mixed = pool_mixer(proj, 0, batch, seq, 0, jnp.zeros((batch, POOL_CARRY, width), F32),
                               pw, pool_scale_even[j], mixed)
            prev_s = jnp.pad(state_pool[j].astype(F32),
                             ((0, 0), (POOL_CARRY - POOL_PREV, 0), (0, 0)))
            mixed = pool_mixer(proj, n_p, dec_batch, dec_seq, PAST_LEN, prev_s, pw,
                               pool_scale_even[j], mixed)
            w_mix = w_out_even[j]
            hgrn_p.append(jnp.swapaxes(sp, -1, -2))
            hgrn_s.append(jnp.swapaxes(ss, -1, -2))
            pool_p.append(jnp.stack([proj[(b + 1) * seq - POOL_PREV:(b + 1) * seq, 4 * width:]
                                     for b in range(batch)]))
            pool_s.append(proj[n_p:].reshape(dec_batch, dec_seq, 5 * width)
                          [:, dec_seq - POOL_PREV:, 4 * width:])
        else:
            qkv = matmul(xb, w_qkv_odd[j].astype(BF16), out_dtype=BF16)
            mixed = band_attention_prompt(qkv, batch, seq, rel_bias_odd[j])
            mixed = band_attention_sample(qkv, n_p, dec_batch, dec_seq, cache_k, cache_v, j,
                                          rel_bias_odd[j], mixed)
            w_mix = w_o_odd[j]
            keep = min(WINDOW, seq)
            tail_p = jnp.stack([qkv[(b + 1) * seq - keep:(b + 1) * seq]
                                for b in range(batch)]).astype(F32)
            new_s = qkv[n_p:].reshape(dec_batch, dec_seq, 3 * d).astype(F32)
            k_p.append(tail_p[:, :, d:2 * d].reshape(batch, keep, n_heads_c, HEAD_DIM))
            v_p.append(tail_p[:, :, 2 * d:].reshape(batch, keep, n_heads_c, HEAD_DIM))
            k_s.append(new_s[:, :, d:2 * d].reshape(dec_batch, dec_seq, n_heads_c, HEAD_DIM))
            v_s.append(new_s[:, :, 2 * d:].reshape(dec_batch, dec_seq, n_heads_c, HEAD_DIM))
        x, xb, xpk = matmul_residual_ln(mixed, w_mix.astype(BF16), x, ln_g[l, 0], ln_b[l, 0],
                                        alpha)
        x, xb = moe_ffn_ln(x, xb, xpk, w_router[l], router_bias[l], w_gu, w_dn, l,
                           ws_gu[l].astype(BF16), ws_dn[l].astype(BF16), ln_g[l, 1], ln_b[l, 1],
                           alpha)

    y_p = x[:n_p].reshape(batch, seq, d)
    y_s = x[n_p:].reshape(dec_batch, dec_seq, d)
    return (y_p, y_s, jnp.stack(hgrn_p), jnp.stack(pool_p), jnp.stack(k_p), jnp.stack(v_p),
            jnp.stack(hgrn_s), jnp.stack(pool_s), jnp.stack(k_s), jnp.stack(v_s))
```
